```python
import math
import jax, jax.numpy as jnp
from jax import lax
import numpy as np


D_MODEL = 4096
BATCH = 1
SEQ = 8192
DEPTH = 4

N_MIXERS = 3
HEAD_DIM = 128
MIX_WIDTH = D_MODEL
MEM_WIDTH = MIX_WIDTH // 4
TOK_WIDTH = MIX_WIDTH - MEM_WIDTH
MEM_TOKENS = 256
MEM_HEADS = 4
MEM_HEAD_DIM = MEM_WIDTH // MEM_HEADS
DIFF_HEADS = TOK_WIDTH // (2 * HEAD_DIM)
DIFF_QK_DIM = HEAD_DIM
DIFF_V_DIM = 2 * HEAD_DIM
FOX_HEADS = TOK_WIDTH // HEAD_DIM
SSD_HEAD_DIM = 64
SSD_INNER = TOK_WIDTH
SSD_HEADS = SSD_INNER // SSD_HEAD_DIM
SSD_GROUPS = 8
SSD_STATE = 128
SSD_CONV = 4
SSD_CHUNK = 128
SSD_CONV_CH = SSD_INNER + 2 * SSD_GROUPS * SSD_STATE
ROPE_THETA = 500000.0
ROPE_DIM = HEAD_DIM // 4
D_FF = 256 * ((8 * D_MODEL // 3 + 255) // 256)
FFN_CONV = 3
Q_BLOCK = 128
NORM_EPS = 1e-6

A_COLS = 3 * TOK_WIDTH + MEM_WIDTH
B_COLS = 3 * TOK_WIDTH + FOX_HEADS + MEM_WIDTH
C_COLS = SSD_INNER + SSD_CONV_CH + SSD_HEADS + MEM_WIDTH

kernel_name = "hybrid_diff_fox_ssd_block"


def rms_norm(x, g):
    xf = x.astype(jnp.float32)
    y = xf * lax.rsqrt(jnp.mean(xf * xf, axis=-1, keepdims=True) + NORM_EPS)
    return (y * g.astype(jnp.float32)).astype(x.dtype)


def partial_rotary(t, positions):
    half = ROPE_DIM // 2
    inv_freq = jnp.power(jnp.float32(ROPE_THETA), -jnp.arange(half, dtype=jnp.float32) * (2.0 / ROPE_DIM))
    ang = positions.astype(jnp.float32)[..., None] * inv_freq
    ang = ang.reshape(ang.shape[:2] + (1,) * (t.ndim - 3) + (half,))
    cos, sin = jnp.cos(ang), jnp.sin(ang)
    t1 = t[..., :half].astype(jnp.float32)
    t2 = t[..., half:ROPE_DIM].astype(jnp.float32)
    rot = jnp.concatenate([t1 * cos - t2 * sin, t2 * cos + t1 * sin], axis=-1).astype(t.dtype)
    return jnp.concatenate([rot, t[..., ROPE_DIM:]], axis=-1)


def causal_block_sweep(block_fn, seq):
    return jnp.concatenate([block_fn(s0, s0 + Q_BLOCK) for s0 in range(0, seq, Q_BLOCK)], axis=1)


def block_causal_mask(s0, s1):
    return jnp.arange(s1)[None, :] <= jnp.arange(s0, s1)[:, None]


def causal_dwconv(x, w, b):
    K, S = w.shape[0], x.shape[1]
    xp = jnp.pad(x, ((0, 0), (K - 1, 0), (0, 0)))
    out = b + xp[:, K - 1:K - 1 + S] * w[K - 1]
    for k in range(K - 1):
        out = out + xp[:, k:k + S] * w[k]
    return out


def lambda_init(layer_idx):
    return 0.8 - 0.6 * math.exp(-0.3 * layer_idx)


def diff_attention(proj, positions, lam_vecs, subln_g, lam_init):
    B, S, _ = proj.shape
    q = proj[..., :TOK_WIDTH].reshape(B, S, DIFF_HEADS, 2, DIFF_QK_DIM)
    k = proj[..., TOK_WIDTH:2 * TOK_WIDTH].reshape(B, S, DIFF_HEADS, 2, DIFF_QK_DIM)
    v = proj[..., 2 * TOK_WIDTH:3 * TOK_WIDTH].reshape(B, S, DIFF_HEADS, DIFF_V_DIM)
    q = partial_rotary(q, positions)
    k = partial_rotary(k, positions)
    lv = lam_vecs.astype(jnp.float32)
    lam = jnp.exp(jnp.sum(lv[0] * lv[1])) - jnp.exp(jnp.sum(lv[2] * lv[3])) + lam_init
    scale = DIFF_QK_DIM ** -0.5

    def block(s0, s1):
        s = jnp.einsum('bqhmd,bkhmd->bhmqk', q[:, s0:s1], k[:, :s1]).astype(jnp.float32) * scale
        s = jnp.where(block_causal_mask(s0, s1), s, -jnp.inf)
        p = jax.nn.softmax(s, axis=-1)
        w = (p[:, :, 0] - lam * p[:, :, 1]).astype(v.dtype)
        return jnp.einsum('bhqk,bkhe->bqhe', w, v[:, :s1])

    o = causal_block_sweep(block, S)
    o = rms_norm(o, subln_g) * (1.0 - lam_init)
    return o.reshape(B, S, TOK_WIDTH)


def forgetting_attention(proj, forget_bias):
    B, S, _ = proj.shape
    q = proj[..., :TOK_WIDTH].reshape(B, S, FOX_HEADS, HEAD_DIM)
    k = proj[..., TOK_WIDTH:2 * TOK_WIDTH].reshape(B, S, FOX_HEADS, HEAD_DIM)
    v = proj[..., 2 * TOK_WIDTH:3 * TOK_WIDTH].reshape(B, S, FOX_HEADS, HEAD_DIM)
    log_f = jax.nn.log_sigmoid(proj[..., 3 * TOK_WIDTH:].astype(jnp.float32) + forget_bias.astype(jnp.float32))
    cum = jnp.swapaxes(jnp.cumsum(log_f, axis=1), 1, 2)
    scale = HEAD_DIM ** -0.5

    def block(s0, s1):
        s = jnp.einsum('bqhd,bkhd->bhqk', q[:, s0:s1], k[:, :s1]).astype(jnp.float32) * scale
        s = s + cum[:, :, s0:s1, None] - cum[:, :, None, :s1]
        s = jnp.where(block_causal_mask(s0, s1), s, -jnp.inf)
        p = jax.nn.softmax(s, axis=-1).astype(v.dtype)
        return jnp.einsum('bhqk,bkhd->bqhd', p, v[:, :s1])

    return causal_block_sweep(block, S).reshape(B, S, TOK_WIDTH)


def ssd_chunked(X, A, Bm, Cm):
    B, S, H, P = X.shape
    G, N = Bm.shape[2], Bm.shape[3]
    R, L = H // G, SSD_CHUNK
    nc = S // L
    X = X.reshape(B, nc, L, G, R, P)
    A = A.reshape(B, nc, L, G, R)
    Bm = Bm.reshape(B, nc, L, G, N)
    Cm = Cm.reshape(B, nc, L, G, N)
    a_cs = jnp.cumsum(A, axis=2)
    mask = (jnp.arange(L)[None, :] <= jnp.arange(L)[:, None])[None, None, :, :, None, None]
    seg = a_cs[:, :, :, None] - a_cs[:, :, None, :]
    decay = jnp.exp(jnp.where(mask, seg, -jnp.inf))
    cb = jnp.einsum('bclgn,bcsgn->bclsg', Cm, Bm)
    y_diag = jnp.einsum('bclsg,bclsgr,bcsgrp->bclgrp', cb, decay, X)
    decay_to_end = jnp.exp(a_cs[:, :, -1:] - a_cs)
    states = jnp.einsum('bclgn,bclgr,bclgrp->bcgrpn', Bm, decay_to_end, X)
    chunk_decay = jnp.exp(a_cs[:, :, -1])

    def step(h, inp):
        st, dec = inp
        return h * dec[..., None, None] + st, h

    h0 = jnp.zeros((B, G, R, P, N), jnp.float32)
    _, h_in = lax.scan(step, h0, (jnp.moveaxis(states, 1, 0), jnp.moveaxis(chunk_decay, 1, 0)))
    h_in = jnp.moveaxis(h_in, 0, 1)
    y_off = jnp.einsum('bclgn,bcgrpn,bclgr->bclgrp', Cm, h_in, jnp.exp(a_cs))
    return (y_diag + y_off).reshape(B, S, H, P)


def ssd_mixer(proj, conv_w, conv_b, dt_bias, a_log, d_skip, norm_g):
    B, S, _ = proj.shape
    f32 = jnp.float32
    z = proj[..., :SSD_INNER].astype(f32)
    xbc = proj[..., SSD_INNER:SSD_INNER + SSD_CONV_CH]
    dt_raw = proj[..., SSD_INNER + SSD_CONV_CH:].astype(f32)
    xbc = jax.nn.silu(causal_dwconv(xbc, conv_w, conv_b)).astype(f32)
    xs = xbc[..., :SSD_INNER].reshape(B, S, SSD_HEADS, SSD_HEAD_DIM)
    Bm = xbc[..., SSD_INNER:SSD_INNER + SSD_GROUPS * SSD_STATE].reshape(B, S, SSD_GROUPS, SSD_STATE)
    Cm = xbc[..., SSD_INNER + SSD_GROUPS * SSD_STATE:].reshape(B, S, SSD_GROUPS, SSD_STATE)
    dt = jax.nn.softplus(dt_raw + dt_bias.astype(f32))
    A = -jnp.exp(a_log.astype(f32))
    y = ssd_chunked(xs * dt[..., None], dt * A, Bm, Cm)
    y = y + xs * d_skip.astype(f32)[:, None]
    g = (y.reshape(B, S, SSD_INNER) * jax.nn.silu(z)).reshape(B, S, SSD_GROUPS, SSD_INNER // SSD_GROUPS)
    g = g * lax.rsqrt(jnp.mean(g * g, axis=-1, keepdims=True) + NORM_EPS)
    out = g.reshape(B, S, SSD_INNER) * norm_g.astype(f32)
    return out.astype(proj.dtype)


def memory_attention(qm, mem_kv):
    B, S, _ = qm.shape
    M = mem_kv.shape[1]
    q = qm.reshape(B, S, MEM_HEADS, MEM_HEAD_DIM)
    k = mem_kv[..., :MEM_WIDTH].reshape(B, M, MEM_HEADS, MEM_HEAD_DIM)
    v = mem_kv[..., MEM_WIDTH:].reshape(B, M, MEM_HEADS, MEM_HEAD_DIM)
    s = jnp.einsum('bqhd,bmhd->bhqm', q, k).astype(jnp.float32) * (MEM_HEAD_DIM ** -0.5)
    p = jax.nn.softmax(s, axis=-1).astype(v.dtype)
    return jnp.einsum('bhqm,bmhd->bqhd', p, v).reshape(B, S, MEM_WIDTH)


def conv_glu_ffn(h, w_up, conv_w, conv_b, w_down):
    u = causal_dwconv(h @ w_up, conv_w, conv_b)
    gate, val = u[..., :D_FF], u[..., D_FF:]
    return (jax.nn.silu(gate) * val) @ w_down


def setup_inputs(seed: int = 0) -> dict:
    key = jax.random.key(seed)
    ks = jax.random.split(key, 32)
    f32 = jnp.float32
    n_a = len(range(0, DEPTH, N_MIXERS))
    n_b = len(range(1, DEPTH, N_MIXERS))
    n_c = len(range(2, DEPTH, N_MIXERS))

    def nrm(k, shape, scale):
        return jax.random.normal(k, shape, f32) * scale

    def gain(k, shape):
        return 1.0 + 0.02 * jax.random.normal(k, shape, f32)

    dt = jnp.exp(jax.random.uniform(ks[20], (n_c, SSD_HEADS), f32, math.log(1e-3), math.log(1e-1)))
    return {
        'x': nrm(ks[0], (BATCH, SEQ, D_MODEL), 1.0),
        'mem': nrm(ks[1], (BATCH, MEM_TOKENS, D_MODEL), 1.0),
        'positions': jnp.arange(SEQ, dtype=jnp.int32)[None, :] + jax.random.randint(ks[2], (BATCH, 1), 0, 1024, dtype=jnp.int32),
        'norm_mix': gain(ks[3], (DEPTH, D_MODEL)),
        'norm_mem': gain(ks[4], (DEPTH, D_MODEL)),
        'w_mem_kv': nrm(ks[5], (DEPTH, D_MODEL, 2 * MEM_WIDTH), D_MODEL ** -0.5),
        'w_out': nrm(ks[6], (DEPTH, MIX_WIDTH, D_MODEL), MIX_WIDTH ** -0.5),
        'norm_ffn': gain(ks[7], (DEPTH, D_MODEL)),
        'w_up': nrm(ks[8], (DEPTH, D_MODEL, 2 * D_FF), D_MODEL ** -0.5),
        'conv_ffn_w': nrm(ks[9], (DEPTH, FFN_CONV, 2 * D_FF), FFN_CONV ** -0.5),
        'conv_ffn_b': nrm(ks[10], (DEPTH, 2 * D_FF), 0.02),
        'w_down': nrm(ks[11], (DEPTH, D_FF, D_MODEL), D_FF ** -0.5),
        'a_w_in': nrm(ks[12], (n_a, D_MODEL, A_COLS), D_MODEL ** -0.5),
        'a_lambda': nrm(ks[13], (n_a, 4, DIFF_QK_DIM), 0.1),
        'a_subln': gain(ks[14], (n_a, DIFF_V_DIM)),
        'b_w_in': nrm(ks[15], (n_b, D_MODEL, B_COLS), D_MODEL ** -0.5),
        'b_forget_bias': jax.random.uniform(ks[16], (n_b, FOX_HEADS), f32, 1.0, 5.0),
        'c_w_in': nrm(ks[17], (n_c, D_MODEL, C_COLS), D_MODEL ** -0.5),
        'c_conv_w': nrm(ks[18], (n_c, SSD_CONV, SSD_CONV_CH), SSD_CONV ** -0.5),
        'c_conv_b': nrm(ks[19], (n_c, SSD_CONV_CH), 0.02),
        'c_dt_bias': dt + jnp.log(-jnp.expm1(-dt)),
        'c_a_log': jnp.log(jax.random.uniform(ks[21], (n_c, SSD_HEADS), f32, 1.0, 16.0)),
        'c_d_skip': gain(ks[22], (n_c, SSD_HEADS)),
        'c_norm_gate': gain(ks[23], (n_c, SSD_INNER)),
        'final_norm': gain(ks[24], (D_MODEL,)),
    }


def reference(x, mem, positions, norm_mix, norm_mem, w_mem_kv, w_out, norm_ffn, w_up, conv_ffn_w,
              conv_ffn_b, w_down, a_w_in, a_lambda, a_subln, b_w_in, b_forget_bias, c_w_in, c_conv_w,
              c_conv_b, c_dt_bias, c_a_log, c_d_skip, c_norm_gate, final_norm):
    for i in range(DEPTH):
        kind, j = i % N_MIXERS, i // N_MIXERS
        h = rms_norm(x, norm_mix[i])
        if kind == 0:
            proj = h @ a_w_in[j]
            tok = diff_attention(proj[..., :-MEM_WIDTH], positions, a_lambda[j], a_subln[j], lambda_init(i))
        elif kind == 1:
            proj = h @ b_w_in[j]
            tok = forgetting_attention(proj[..., :-MEM_WIDTH], b_forget_bias[j])
        else:
            proj = h @ c_w_in[j]
            tok = ssd_mixer(proj[..., :-MEM_WIDTH], c_conv_w[j], c_conv_b[j], c_dt_bias[j],
                            c_a_log[j], c_d_skip[j], c_norm_gate[j])
        mem_kv = rms_norm(mem, norm_mem[i]) @ w_mem_kv[i]
        ctx = memory_attention(proj[..., -MEM_WIDTH:], mem_kv)
        x = x + jnp.concatenate([tok, ctx], axis=-1) @ w_out[i]
        x = x + conv_glu_ffn(rms_norm(x, norm_ffn[i]), w_up[i], conv_ffn_w[i], conv_ffn_b[i], w_down[i])
    return rms_norm(x, final_norm)
```

```python
import functools
import math

import jax
import jax.numpy as jnp
from jax import lax
from jax.experimental import pallas as pl
from jax.experimental.pallas import tpu as pltpu

F32 = jnp.float32
BF16 = jnp.bfloat16

HEAD_DIM = 128
MEM_HEADS = 4
SSD_HEAD_DIM = 64
SSD_GROUPS = 8
SSD_STATE = 128
SSD_CONV = 4
SSD_CHUNK = 128
ROPE_THETA = 500000.0
ROPE_DIM = HEAD_DIM // 4
FFN_CONV = 3
NORM_EPS = 1e-6
N_MIXERS = 3

LANES = 128
SUBLANES = 8
VMEM_LIMIT_CAP = 56 * 1024 * 1024

MM_BM = 1024
MM_BN = 1024
FFN_BN = 512
FFN_PAD = 1024
DOWN_BK = 2816
ATT_T = 512
MEM_TQ = 512
ROW_BLK = 256
CUM_BLK = 256


def _vmem(nbytes):
    return int(min(VMEM_LIMIT_CAP, max(16 * 1024 * 1024, nbytes)))


def _params(nbytes, n_grid):
    return pltpu.CompilerParams(
        dimension_semantics=("arbitrary",) * n_grid, vmem_limit_bytes=_vmem(nbytes))


def _softplus(x):
    return jnp.maximum(x, 0.0) + jnp.log1p(jnp.exp(-jnp.abs(x)))


def _silu(x):
    return x / (1.0 + jnp.exp(-x))


def _split3(x):
    hi = x.astype(BF16)
    r = x - hi.astype(F32)
    mid = r.astype(BF16)
    lo = (r - mid.astype(F32)).astype(BF16)
    return hi, mid, lo


def _tri_lower(n):
    r = lax.broadcasted_iota(jnp.int32, (n, n), 0)
    c = lax.broadcasted_iota(jnp.int32, (n, n), 1)
    return jnp.where(c <= r, 1.0, 0.0).astype(BF16)


def _cumsum_rows(tri, x):
    hi, mid, lo = _split3(x)
    d = lambda a: jnp.dot(tri, a, preferred_element_type=F32)
    return d(hi) + d(mid) + d(lo)


def _cumsum_lanes(x, tri_t):
    hi, mid, lo = _split3(x)
    d = lambda a: jnp.dot(a, tri_t, preferred_element_type=F32)
    return d(hi) + d(mid) + d(lo)


def _rmsnorm_kernel(x_ref, g_ref, o_ref):
    x = x_ref[...].astype(F32)
    ms = jnp.mean(x * x, axis=-1, keepdims=True)
    o_ref[...] = (x * lax.rsqrt(ms + NORM_EPS) * g_ref[...]).astype(o_ref.dtype)


def rmsnorm(x, g, out_dtype):
    m, d = x.shape
    rb = min(ROW_BLK, m)
    return pl.pallas_call(
        _rmsnorm_kernel,
        grid=(m // rb,),
        in_specs=[pl.BlockSpec((rb, d), lambda i: (i, 0)),
                  pl.BlockSpec((1, d), lambda i: (0, 0))],
        out_specs=pl.BlockSpec((rb, d), lambda i: (i, 0)),
        out_shape=jax.ShapeDtypeStruct((m, d), out_dtype),
        compiler_params=_params(6 * rb * d * 4, 1),
        name="rmsnorm",
    )(x, g.reshape(1, d).astype(F32))


def _mm_kernel(*refs, nk, has_res):
    if has_res:
        x_ref, w_ref, r_ref, o_ref = refs[:4]
        scratch = refs[4:]
    else:
        x_ref, w_ref, o_ref = refs[:3]
        r_ref = None
        scratch = refs[3:]
    part = jnp.dot(x_ref[...], w_ref[...], preferred_element_type=F32)
    if nk == 1:
        if has_res:
            part = part + r_ref[...]
        o_ref[...] = part.astype(o_ref.dtype)
        return
    acc_ref = scratch[0]
    k = pl.program_id(2)

    @pl.when(k == 0)
    def _():
        acc_ref[...] = part

    @pl.when(jnp.logical_and(k > 0, k < nk - 1))
    def _():
        acc_ref[...] += part

    @pl.when(k == nk - 1)
    def _():
        tot = acc_ref[...] + part
        if has_res:
            tot = tot + r_ref[...]
        o_ref[...] = tot.astype(o_ref.dtype)


def matmul(x, w, out_dtype, res=None, bk=None):
    m, kdim = x.shape
    n = w.shape[1]
    bm = min(MM_BM, m)
    bn = min(MM_BN, n)
    bk = kdim if bk is None else bk
    nk = kdim // bk
    assert m % bm == 0 and n % bn == 0 and kdim % bk == 0
    in_specs = [pl.BlockSpec((bm, bk), lambda j, i, k: (i, k)),
                pl.BlockSpec((bk, bn), lambda j, i, k: (k, j))]
    args = [x, w]
    if res is not None:
        in_specs.append(pl.BlockSpec((bm, bn), lambda j, i, k: (i, j)))
        args.append(res)
    osz = jnp.dtype(out_dtype).itemsize
    nbytes = 2 * (bm * bk * 2 + bk * bn * 2 + bm * bn * osz) + 2 * bm * bn * 4
    if res is not None:
        nbytes += 2 * bm * bn * 4
    scratch = []
    if nk > 1:
        scratch.append(pltpu.VMEM((bm, bn), F32))
        nbytes += bm * bn * 4
    return pl.pallas_call(
        functools.partial(_mm_kernel, nk=nk, has_res=res is not None),
        grid=(n // bn, m // bm, nk),
        in_specs=in_specs,
        out_specs=pl.BlockSpec((bm, bn), lambda j, i, k: (i, j)),
        out_shape=jax.ShapeDtypeStruct((m, n), out_dtype),
        scratch_shapes=scratch,
        compiler_params=_params(nbytes + (4 << 20), 3),
        name="matmul",
    )(*args)


def _ffn_up_kernel(x_ref, wg_ref, wv_ref, cg_ref, cv_ref, o_ref, tail_g, tail_v, *, bm):
    @pl.when(pl.program_id(1) == 0)
    def _():
        tail_g[...] = jnp.zeros_like(tail_g)
        tail_v[...] = jnp.zeros_like(tail_v)

    x = x_ref[...]

    def branch(w_ref, c_ref, tail):
        u = jnp.dot(x, w_ref[...], preferred_element_type=F32)
        ext = jnp.concatenate([tail[...], u], axis=0)
        u1 = pltpu.roll(ext, 1, 0)[SUBLANES:]
        u2 = pltpu.roll(ext, 2, 0)[SUBLANES:]
        tail[...] = u[bm - SUBLANES:]
        c = c_ref[...]
        return c[3:4] + c[2:3] * u + c[1:2] * u1 + c[0:1] * u2

    g = branch(wg_ref, cg_ref, tail_g)
    v = branch(wv_ref, cv_ref, tail_v)
    o_ref[...] = (_silu(g) * v).astype(o_ref.dtype)


def ffn_up(h, wg, wv, cg, cv):
    m, d = h.shape
    fp = wg.shape[1]
    bm = min(MM_BM, m)
    bn = FFN_BN
    nbytes = 2 * (bm * d * 2 + 2 * d * bn * 2 + bm * bn * 2) + 10 * bm * bn * 4
    return pl.pallas_call(
        functools.partial(_ffn_up_kernel, bm=bm),
        grid=(fp // bn, m // bm),
        in_specs=[pl.BlockSpec((bm, d), lambda j, i: (i, 0)),
                  pl.BlockSpec((d, bn), lambda j, i: (0, j)),
                  pl.BlockSpec((d, bn), lambda j, i: (0, j)),
                  pl.BlockSpec((SUBLANES, bn), lambda j, i: (0, j)),
                  pl.BlockSpec((SUBLANES, bn), lambda j, i: (0, j))],
        out_specs=pl.BlockSpec((bm, bn), lambda j, i: (i, j)),
        out_shape=jax.ShapeDtypeStruct((m, fp), BF16),
        scratch_shapes=[pltpu.VMEM((SUBLANES, bn), F32), pltpu.VMEM((SUBLANES, bn), F32)],
        compiler_params=_params(nbytes, 2),
        name="ffn_up",
    )(h, wg, wv, cg, cv)


def _dwconv_silu_kernel(x_ref, w_ref, o_ref, tail, *, taps, rb):
    @pl.when(pl.program_id(1) == 0)
    def _():
        tail[...] = jnp.zeros_like(tail)

    x = x_ref[...]
    ext = jnp.concatenate([tail[...], x], axis=0)
    w = w_ref[...]
    out = w[taps:taps + 1] + w[taps - 1:taps] * x
    for k in range(taps - 1):
        d = taps - 1 - k
        out = out + pltpu.roll(ext, d, 0)[SUBLANES:] * w[k:k + 1]
    tail[...] = x[rb - SUBLANES:]
    o_ref[...] = _silu(out).astype(o_ref.dtype)


def dwconv_silu(x, col0, ncols, w8, taps):
    m = x.shape[0]
    rb = min(ROW_BLK, m)
    cb = 1024
    assert col0 % cb == 0 and ncols % cb == 0
    off = col0 // cb
    return pl.pallas_call(
        functools.partial(_dwconv_silu_kernel, taps=taps, rb=rb),
        grid=(ncols // cb, m // rb),
        in_specs=[pl.BlockSpec((rb, cb), lambda j, i: (i, j + off)),
                  pl.BlockSpec((SUBLANES, cb), lambda j, i: (0, j))],
        out_specs=pl.BlockSpec((rb, cb), lambda j, i: (i, j)),
        out_shape=jax.ShapeDtypeStruct((m, ncols), F32),
        scratch_shapes=[pltpu.VMEM((SUBLANES, cb), F32)],
        compiler_params=_params(16 * rb * cb * 4, 2),
        name="dwconv_silu",
    )(x, w8)


def _rope_kernel(x_ref, cos_ref, sa_ref, sb_ref, o_ref, *, groups):
    cos = cos_ref[...]
    sa = sa_ref[...]
    sb = sb_ref[...]
    half = ROPE_DIM // 2
    for gi in range(groups):
        t = x_ref[:, gi * LANES:(gi + 1) * LANES]
        up = pltpu.roll(t, LANES - half, 1)
        dn = pltpu.roll(t, half, 1)
        o_ref[:, gi * LANES:(gi + 1) * LANES] = (t * cos + up * sa + dn * sb).astype(o_ref.dtype)


def rope(x, cos, sa, sb):
    m, c = x.shape
    rb = min(ROW_BLK, m)
    cb = 1024
    tab = pl.BlockSpec((rb, LANES), lambda i, j: (i, 0))
    return pl.pallas_call(
        functools.partial(_rope_kernel, groups=cb // LANES),
        grid=(m // rb, c // cb),
        in_specs=[pl.BlockSpec((rb, cb), lambda i, j: (i, j)), tab, tab, tab],
        out_specs=pl.BlockSpec((rb, cb), lambda i, j: (i, j)),
        out_shape=jax.ShapeDtypeStruct((m, c), BF16),
        compiler_params=_params(12 * rb * cb * 4, 2),
        name="rope",
    )(x, cos, sa, sb)


def rope_tables(positions):
    half = ROPE_DIM // 2
    inv_freq = jnp.power(jnp.float32(ROPE_THETA), -jnp.arange(half, dtype=F32) * (2.0 / ROPE_DIM))
    ang = positions.astype(F32)[:, None] * inv_freq
    cos, sin = jnp.cos(ang), jnp.sin(ang)
    s = positions.shape[0]
    pad = LANES - ROPE_DIM
    cos_t = jnp.concatenate([cos, cos, jnp.ones((s, pad), F32)], axis=1)
    sa = jnp.concatenate([-sin, jnp.zeros((s, LANES - half), F32)], axis=1)
    sb = jnp.concatenate([jnp.zeros((s, half), F32), sin, jnp.zeros((s, pad), F32)], axis=1)
    return cos_t, sa, sb


_NT = (((1,), (1,)), ((), ()))


def _softmax_step(q, k, v, state, scale, bias, mask):
    m_i, l_i, acc = state
    s = lax.dot_general(q, k, _NT, preferred_element_type=F32) * scale
    if bias is not None:
        s = s + bias
    if mask is not None:
        s = jnp.where(mask, s, -jnp.inf)
    m_new = jnp.maximum(m_i, jnp.max(s, axis=-1, keepdims=True))
    p = jnp.exp(s - m_new)
    alpha = jnp.exp(m_i - m_new)
    l_new = alpha * l_i + jnp.sum(p, axis=-1, keepdims=True)
    acc_new = alpha * acc + jnp.dot(p.astype(BF16), v, preferred_element_type=F32)
    return m_new, l_new, acc_new


def _init_state(t, dv):
    return (jnp.full((t, 1), -jnp.inf, F32), jnp.zeros((t, 1), F32), jnp.zeros((t, dv), F32))


def _diag_mask(t):
    r = lax.broadcasted_iota(jnp.int32, (t, t), 0)
    c = lax.broadcasted_iota(jnp.int32, (t, t), 1)
    return c <= r


def _fox_kernel(q_ref, k_ref, v_ref, b_ref, o_ref, *, t, scale):
    qi = pl.program_id(1)
    q = q_ref[...]

    def step(j, state, mask):
        off = pl.multiple_of(j * t, t)
        k = k_ref[pl.ds(off, t), :]
        v = v_ref[pl.ds(off, t), :]
        bias = b_ref[:, pl.ds(off, t)]
        return _softmax_step(q, k, v, state, scale, bias, mask)

    state = lax.fori_loop(0, qi, lambda j, st: step(j, st, None), _init_state(t, HEAD_DIM))
    _, l_i, acc = step(qi, state, _diag_mask(t))
    o_ref[...] = (acc / l_i).astype(o_ref.dtype)


def fox_attention(qkv, negcum, n_heads):
    s = qkv.shape[0]
    t = min(ATT_T, s)
    hd = HEAD_DIM
    nbytes = 2 * (2 * s * hd * 2 + 2 * t * hd * 2 + s * 4) + 12 * t * t * 4
    return pl.pallas_call(
        functools.partial(_fox_kernel, t=t, scale=hd ** -0.5),
        grid=(n_heads, s // t),
        in_specs=[pl.BlockSpec((t, hd), lambda h, i: (i, h)),
                  pl.BlockSpec((s, hd), lambda h, i: (0, n_heads + h)),
                  pl.BlockSpec((s, hd), lambda h, i: (0, 2 * n_heads + h)),
                  pl.BlockSpec((None, 1, s), lambda h, i: (h, 0, 0))],
        out_specs=pl.BlockSpec((t, hd), lambda h, i: (i, h)),
        out_shape=jax.ShapeDtypeStruct((s, n_heads * hd), BF16),
        compiler_params=_params(nbytes, 2),
        name="fox_attention",
    )(qkv, qkv, qkv, negcum)


def _diff_kernel(q_ref, k_ref, v_ref, lam_ref, g_ref, o_ref, *, t, scale, lam_init):
    qi = pl.program_id(1)
    hd = HEAD_DIM
    q0 = q_ref[:, :hd]
    q1 = q_ref[:, hd:]

    def step(j, states, mask):
        off = pl.multiple_of(j * t, t)
        v = v_ref[pl.ds(off, t), :]
        k0 = k_ref[pl.ds(off, t), :hd]
        k1 = k_ref[pl.ds(off, t), hd:]
        return (_softmax_step(q0, k0, v, states[0], scale, None, mask),
                _softmax_step(q1, k1, v, states[1], scale, None, mask))

    init = (_init_state(t, 2 * hd), _init_state(t, 2 * hd))
    states = lax.fori_loop(0, qi, lambda j, st: step(j, st, None), init)
    (_, l0, a0), (_, l1, a1) = step(qi, states, _diag_mask(t))
    lv = lam_ref[...]
    lam = (jnp.exp(jnp.sum(lv[0:1] * lv[1:2], axis=-1, keepdims=True))
           - jnp.exp(jnp.sum(lv[2:3] * lv[3:4], axis=-1, keepdims=True)) + lam_init)
    o = a0 / l0 - lam * (a1 / l1)
    o = o * lax.rsqrt(jnp.mean(o * o, axis=-1, keepdims=True) + NORM_EPS) * g_ref[...]
    o_ref[...] = (o * (1.0 - lam_init)).astype(o_ref.dtype)


def diff_attention(qk, v, lam_vecs, subln_g, lam_init, n_heads):
    s = qk.shape[0]
    t = min(ATT_T, s)
    w = 2 * HEAD_DIM
    nbytes = 2 * (2 * s * w * 2 + 2 * t * w * 2) + 20 * t * t * 4
    return pl.pallas_call(
        functools.partial(_diff_kernel, t=t, scale=HEAD_DIM ** -0.5, lam_init=lam_init),
        grid=(n_heads, s // t),
        in_specs=[pl.BlockSpec((t, w), lambda h, i: (i, h)),
                  pl.BlockSpec((s, w), lambda h, i: (0, n_heads + h)),
                  pl.BlockSpec((s, w), lambda h, i: (0, h)),
                  pl.BlockSpec((4, HEAD_DIM), lambda h, i: (0, 0)),
                  pl.BlockSpec((1, w), lambda h, i: (0, 0))],
        out_specs=pl.BlockSpec((t, w), lambda h, i: (i, h)),
        out_shape=jax.ShapeDtypeStruct((s, n_heads * w), BF16),
        compiler_params=_params(nbytes, 2),
        name="diff_attention",
    )(qk, qk, v, lam_vecs.astype(F32), subln_g.reshape(1, w).astype(F32))


def _mem_attn_kernel(q_ref, k_ref, v_ref, o_ref, *, scale):
    s = lax.dot_general(q_ref[...], k_ref[...], _NT, preferred_element_type=F32) * scale
    p = jnp.exp(s - jnp.max(s, axis=-1, keepdims=True))
    l = jnp.sum(p, axis=-1, keepdims=True)
    o = jnp.dot(p.astype(BF16), v_ref[...], preferred_element_type=F32)
    o_ref[...] = (o / l).astype(o_ref.dtype)


def mem_attention(qarr, q_col0, mem_kv):
    s = qarr.shape[0]
    mem_width = mem_kv.shape[1] // 2
    hd = mem_width // MEM_HEADS
    mt = mem_kv.shape[0]
    tq = min(MEM_TQ, s)
    q_off = q_col0 // hd
    assert q_col0 % hd == 0
    return pl.pallas_call(
        functools.partial(_mem_attn_kernel, scale=hd ** -0.5),
        grid=(s // tq, MEM_HEADS),
        in_specs=[pl.BlockSpec((tq, hd), lambda i, h: (i, q_off + h)),
                  pl.BlockSpec((mt, hd), lambda i, h: (0, h)),
                  pl.BlockSpec((mt, hd), lambda i, h: (0, MEM_HEADS + h))],
        out_specs=pl.BlockSpec((tq, hd), lambda i, h: (i, h)),
        out_shape=jax.ShapeDtypeStruct((s, mem_width), BF16),
        compiler_params=_params(32 * tq * hd * 4, 2),
        name="mem_attention",
    )(qarr, mem_kv, mem_kv)


def _fox_cum_kernel(x_ref, b_ref, o_ref, carry, *, cb):
    @pl.when(pl.program_id(0) == 0)
    def _():
        carry[...] = jnp.zeros_like(carry)

    x = x_ref[...] + b_ref[...]
    log_f = jnp.minimum(x, 0.0) - jnp.log1p(jnp.exp(-jnp.abs(x)))
    cs = _cumsum_rows(_tri_lower(cb), log_f) + carry[...]
    o_ref[...] = cs
    carry[...] = cs[cb - 1:cb, :]


def fox_cum(logits, bias):
    s = logits.shape[0]
    cb = min(CUM_BLK, s)
    return pl.pallas_call(
        functools.partial(_fox_cum_kernel, cb=cb),
        grid=(s // cb,),
        in_specs=[pl.BlockSpec((cb, LANES), lambda i: (i, 0)),
                  pl.BlockSpec((1, LANES), lambda i: (0, 0))],
        out_specs=pl.BlockSpec((cb, LANES), lambda i: (i, 0)),
        out_shape=jax.ShapeDtypeStruct((s, LANES), F32),
        scratch_shapes=[pltpu.VMEM((1, LANES), F32)],
        compiler_params=_params(16 << 20, 1),
        name="fox_cum",
    )(logits, bias)


def _ssd_kernel(z_ref, xs_ref, b_ref, c_ref, dte_ref, dtr_ref, bias_e_ref, alog_e_ref, dskip_e_ref,
                bias_r_ref, alog_r_ref, ng_ref, o_ref, h_ref, *, heads, hdim, chunk):
    @pl.when(pl.program_id(1) == 0)
    def _():
        h_ref[...] = jnp.zeros_like(h_ref)

    L = chunk
    width = heads * hdim
    xs = xs_ref[...]
    bm = b_ref[...]
    cm = c_ref[...]
    tri = _tri_lower(L)

    dt_e = _softplus(dte_ref[...] + bias_e_ref[...])
    a_e = dt_e * (-jnp.exp(alog_e_ref[...]))
    acs_e = _cumsum_rows(tri, a_e)
    acs_last = acs_e[L - 1:L, :]
    dt_r = _softplus(dtr_ref[...] + bias_r_ref[...])
    a_r = dt_r * (-jnp.exp(alog_r_ref[...]))
    acs_r = _cumsum_lanes(a_r, tri.T)

    xd = xs * dt_e
    xd_b = xd.astype(BF16)
    cb = lax.dot_general(cm.astype(BF16), bm.astype(BF16), _NT, preferred_element_type=F32)
    causal = _diag_mask(L)
    head_id = lax.broadcasted_iota(jnp.int32, (L, width), 1) // hdim

    y = jnp.zeros((L, width), F32)
    for r in range(heads):
        seg = acs_e[:, r * hdim:r * hdim + 1] - acs_r[r:r + 1, :]
        decay = jnp.exp(jnp.where(causal, seg, -jnp.inf))
        yr = jnp.dot((cb * decay).astype(BF16), xd_b, preferred_element_type=F32)
        y = jnp.where(head_id == r, yr, y)

    h_in = h_ref[...]
    y_off = jnp.dot(cm.astype(BF16), h_in.astype(BF16), preferred_element_type=F32) * jnp.exp(acs_e)
    to_end = jnp.exp(acs_last - acs_e)
    states = jnp.dot(bm.T.astype(BF16), (xd * to_end).astype(BF16), preferred_element_type=F32)
    h_ref[...] = h_in * jnp.exp(acs_last) + states

    y = y + y_off + xs * dskip_e_ref[...]
    g = y * _silu(z_ref[...])
    g = g * lax.rsqrt(jnp.mean(g * g, axis=-1, keepdims=True) + NORM_EPS)
    o_ref[...] = (g * ng_ref[...]).astype(o_ref.dtype)


def ssd_scan(zx, xbc, dt_e, dt_r, bias_e, alog_e, dskip_e, bias_r, alog_r, norm_g, inner):
    s = zx.shape[0]
    L = SSD_CHUNK
    G = SSD_GROUPS
    width = inner // G
    heads = width // SSD_HEAD_DIM
    N = SSD_STATE
    b_off = inner // N
    c_off = b_off + G
    row = lambda g, c: (0, g)
    return pl.pallas_call(
        functools.partial(_ssd_kernel, heads=heads, hdim=SSD_HEAD_DIM, chunk=L),
        grid=(G, s // L),
        in_specs=[pl.BlockSpec((L, width), lambda g, c: (c, g)),
                  pl.BlockSpec((L, width), lambda g, c: (c, g)),
                  pl.BlockSpec((L, N), lambda g, c: (c, b_off + g)),
                  pl.BlockSpec((L, N), lambda g, c: (c, c_off + g)),
                  pl.BlockSpec((L, width), lambda g, c: (c, g)),
                  pl.BlockSpec((None, SUBLANES, L), lambda g, c: (g, 0, c)),
                  pl.BlockSpec((1, width), row),
                  pl.BlockSpec((1, width), row),
                  pl.BlockSpec((1, width), row),
                  pl.BlockSpec((None, SUBLANES, 1), lambda g, c: (g, 0, 0)),
                  pl.BlockSpec((None, SUBLANES, 1), lambda g, c: (g, 0, 0)),
                  pl.BlockSpec((1, width), row)],
        out_specs=pl.BlockSpec((L, width), lambda g, c: (c, g)),
        out_shape=jax.ShapeDtypeStruct((s, inner), BF16),
        scratch_shapes=[pltpu.VMEM((N, width), F32)],
        compiler_params=_params(32 << 20, 2),
        name="ssd_scan",
    )(zx, xbc, xbc, xbc, dt_e, dt_r, bias_e, alog_e, dskip_e, bias_r, alog_r, norm_g)


def _conv_rows(w, b):
    taps, c = w.shape
    return jnp.concatenate([w.astype(F32), b.reshape(1, c).astype(F32),
                            jnp.zeros((SUBLANES - taps - 1, c), F32)], axis=0)


def _pad_cols(a, n):
    return jnp.pad(a, ((0, 0), (0, n - a.shape[1])))


def kernel(x, mem, positions, norm_mix, norm_mem, w_mem_kv, w_out, norm_ffn, w_up, conv_ffn_w, conv_ffn_b,
           w_down, a_w_in, a_lambda, a_subln, b_w_in, b_forget_bias, c_w_in, c_conv_w, c_conv_b, c_dt_bias,
           c_a_log, c_d_skip, c_norm_gate, final_norm):
    batch, seq, d_model = x.shape
    depth = norm_mix.shape[0]
    mem_width = w_mem_kv.shape[2] // 2
    tok_width = d_model - mem_width
    d_ff = w_down.shape[1]
    ffp = FFN_PAD * ((d_ff + FFN_PAD - 1) // FFN_PAD)
    diff_heads = tok_width // (2 * HEAD_DIM)
    fox_heads = tok_width // HEAD_DIM
    ssd_heads = tok_width // SSD_HEAD_DIM
    heads_per_group = ssd_heads // SSD_GROUPS
    conv_ch = tok_width + 2 * SSD_GROUPS * SSD_STATE

    outs = []
    for b in range(batch):
        xb = x[b]
        mem_b = mem[b]
        cos_t, sa, sb = rope_tables(positions[b])
        for i in range(depth):
            kind, j = i % N_MIXERS, i // N_MIXERS
            h = rmsnorm(xb, norm_mix[i], BF16)
            if kind == 0:
                w = a_w_in[j]
                qk = matmul(h, w[:, :2 * tok_width].astype(BF16), F32)
                vq = matmul(h, w[:, 2 * tok_width:].astype(BF16), BF16)
                qk = rope(qk, cos_t, sa, sb)
                lam_init = 0.8 - 0.6 * math.exp(-0.3 * i)
                tok = diff_attention(qk, vq, a_lambda[j], a_subln[j], lam_init, diff_heads)
                q_arr, q_col0 = vq, tok_width
            elif kind == 1:
                w = b_w_in[j]
                w_main = jnp.concatenate([w[:, :3 * tok_width], w[:, 3 * tok_width + fox_heads:]], axis=1)
                qkvq = matmul(h, w_main.astype(BF16), BF16)
                w_gate = _pad_cols(w[:, 3 * tok_width:3 * tok_width + fox_heads], LANES)
                logits = matmul(h, w_gate.astype(BF16), F32)
                bias = _pad_cols(b_forget_bias[j].reshape(1, fox_heads).astype(F32), LANES)
                cum = fox_cum(logits, bias)
                negcum = (-cum[:, :fox_heads]).T.reshape(fox_heads, 1, seq)
                tok = fox_attention(qkvq, negcum, fox_heads)
                q_arr, q_col0 = qkvq, 3 * tok_width
            else:
                w = c_w_in[j]
                n_zx = tok_width + conv_ch
                zx = matmul(h, w[:, :n_zx].astype(BF16), F32)
                dt_raw = matmul(h, _pad_cols(w[:, n_zx:n_zx + ssd_heads], LANES).astype(BF16), F32)[:, :ssd_heads]
                q_arr = matmul(h, w[:, n_zx + ssd_heads:].astype(BF16), BF16)
                q_col0 = 0
                xbc = dwconv_silu(zx, tok_width, conv_ch, _conv_rows(c_conv_w[j], c_conv_b[j]), SSD_CONV)
                rep = lambda a: jnp.repeat(a.astype(F32), SSD_HEAD_DIM, axis=-1)
                dt_e = rep(dt_raw)
                pad_h = SUBLANES - heads_per_group
                by_group = lambda a: jnp.pad(a.astype(F32).reshape(SSD_GROUPS, heads_per_group, -1),
                                             ((0, 0), (0, pad_h), (0, 0)))
                dt_r = by_group(dt_raw.T)
                tok = ssd_scan(zx, xbc, dt_e, dt_r,
                               rep(c_dt_bias[j].reshape(1, -1)), rep(c_a_log[j].reshape(1, -1)),
                               rep(c_d_skip[j].reshape(1, -1)),
                               by_group(c_dt_bias[j].reshape(-1, 1)), by_group(c_a_log[j].reshape(-1, 1)),
                               c_norm_gate[j].reshape(1, -1).astype(F32), tok_width)
            mem_n = rmsnorm(mem_b, norm_mem[i], BF16)
            mem_kv = matmul(mem_n, w_mem_kv[i].astype(BF16), BF16)
            ctx = mem_attention(q_arr, q_col0, mem_kv)
            mix = jnp.concatenate([tok, ctx], axis=-1)
            xb = matmul(mix, w_out[i].astype(BF16), F32, res=xb)

            hf = rmsnorm(xb, norm_ffn[i], BF16)
            wu = w_up[i]
            wg = _pad_cols(wu[:, :d_ff], ffp).astype(BF16)
            wv = _pad_cols(wu[:, d_ff:], ffp).astype(BF16)
            cg = _pad_cols(_conv_rows(conv_ffn_w[i][:, :d_ff], conv_ffn_b[i][:d_ff]), ffp)
            cv = _pad_cols(_conv_rows(conv_ffn_w[i][:, d_ff:], conv_ffn_b[i][d_ff:]), ffp)
            hidden = ffn_up(hf, wg, wv, cg, cv)
            wd = jnp.pad(w_down[i], ((0, ffp - d_ff), (0, 0))).astype(BF16)
            xb = matmul(hidden, wd, F32, res=xb, bk=DOWN_BK if ffp % DOWN_BK == 0 else None)
        outs.append(rmsnorm(xb, final_norm, x.dtype))
    return jnp.stack(outs, axis=0)
```

```python
import functools
import math

import jax
import jax.numpy as jnp
from jax import lax
from jax.experimental import pallas as pl
from jax.experimental.pallas import tpu as pltpu

F32 = jnp.float32
BF16 = jnp.bfloat16

HEAD_DIM = 128
MEM_HEADS = 4
SSD_HEAD_DIM = 64
SSD_GROUPS = 8
SSD_STATE = 128
SSD_CONV = 4
SSD_CHUNK = 128
ROPE_THETA = 500000.0
ROPE_DIM = HEAD_DIM // 4
FFN_CONV = 3
NORM_EPS = 1e-6
N_MIXERS = 3
LOG2E = math.log2(math.e)

LANES = 128
SUBLANES = 8
VMEM_LIMIT_CAP = 56 * 1024 * 1024

MM_BM = 1024
MM_BN = 1024
FFN_BN = 512
DOWN_BK = 2816
ATT_TQ = 1024
ATT_TK = 512
ATT_SUB = 256
MEM_TQ = 512
ROW_BLK = 256
CUM_BLK = 256


def _vmem(nbytes):
    return int(min(VMEM_LIMIT_CAP, max(16 * 1024 * 1024, nbytes)))


def _params(nbytes, n_grid):
    return pltpu.CompilerParams(
        dimension_semantics=("arbitrary",) * n_grid, vmem_limit_bytes=_vmem(nbytes))


def _softplus(x):
    return jnp.maximum(x, 0.0) + jnp.log1p(jnp.exp(-jnp.abs(x)))


def _silu(x):
    return x / (1.0 + jnp.exp(-x))


def _split3(x):
    hi = x.astype(BF16)
    r = x - hi.astype(F32)
    mid = r.astype(BF16)
    lo = (r - mid.astype(F32)).astype(BF16)
    return hi, mid, lo


def _tri_lower(n):
    r = lax.broadcasted_iota(jnp.int32, (n, n), 0)
    c = lax.broadcasted_iota(jnp.int32, (n, n), 1)
    return jnp.where(c <= r, 1.0, 0.0).astype(BF16)


def _cumsum_rows(tri, x):
    hi, mid, lo = _split3(x)
    d = lambda a: jnp.dot(tri, a, preferred_element_type=F32)
    return d(hi) + d(mid) + d(lo)


def _cumsum_lanes(x, tri_t):
    hi, mid, lo = _split3(x)
    d = lambda a: jnp.dot(a, tri_t, preferred_element_type=F32)
    return d(hi) + d(mid) + d(lo)


def _rmsnorm_kernel(x_ref, g_ref, o_ref):
    x = x_ref[...].astype(F32)
    ms = jnp.mean(x * x, axis=-1, keepdims=True)
    o_ref[...] = (x * lax.rsqrt(ms + NORM_EPS) * g_ref[...]).astype(o_ref.dtype)


def rmsnorm(x, g, out_dtype):
    m, d = x.shape
    rb = min(ROW_BLK, m)
    return pl.pallas_call(
        _rmsnorm_kernel,
        grid=(m // rb,),
        in_specs=[pl.BlockSpec((rb, d), lambda i: (i, 0)),
                  pl.BlockSpec((1, d), lambda i: (0, 0))],
        out_specs=pl.BlockSpec((rb, d), lambda i: (i, 0)),
        out_shape=jax.ShapeDtypeStruct((m, d), out_dtype),
        compiler_params=_params(6 * rb * d * 4, 1),
        name="rmsnorm",
    )(x, g.reshape(1, d).astype(F32))


def _mm_kernel(*refs, nk, bk, k_true, has_res):
    if has_res:
        x_ref, w_ref, r_ref, o_ref = refs[:4]
        scratch = refs[4:]
    else:
        x_ref, w_ref, o_ref = refs[:3]
        r_ref = None
        scratch = refs[3:]
    w = w_ref[...]
    if k_true is not None:
        row = lax.broadcasted_iota(jnp.int32, w.shape, 0) + pl.program_id(2) * bk
        w = jnp.where(row < k_true, w, jnp.zeros_like(w))
    part = jnp.dot(x_ref[...], w, preferred_element_type=F32)
    if nk == 1:
        if has_res:
            part = part + r_ref[...]
        o_ref[...] = part.astype(o_ref.dtype)
        return
    acc_ref = scratch[0]
    k = pl.program_id(2)

    @pl.when(k == 0)
    def _():
        acc_ref[...] = part

    @pl.when(jnp.logical_and(k > 0, k < nk - 1))
    def _():
        acc_ref[...] += part

    @pl.when(k == nk - 1)
    def _():
        tot = acc_ref[...] + part
        if has_res:
            tot = tot + r_ref[...]
        o_ref[...] = tot.astype(o_ref.dtype)


def matmul(x, w, out_dtype, col0=0, ncols=None, res=None, bk=None):
    m, kdim = x.shape
    n = w.shape[1] - col0 if ncols is None else ncols
    bm = min(MM_BM, m)
    bn = min(MM_BN, n)
    bk = kdim if bk is None else bk
    nk = kdim // bk
    assert m % bm == 0 and n % bn == 0 and kdim % bk == 0 and col0 % bn == 0
    k_true = None if w.shape[0] == kdim else w.shape[0]
    assert w.shape[0] <= kdim and w.shape[0] > (nk - 1) * bk
    joff = col0 // bn
    in_specs = [pl.BlockSpec((bm, bk), lambda j, i, k: (i, k)),
                pl.BlockSpec((bk, bn), lambda j, i, k: (k, j + joff))]
    args = [x, w]
    if res is not None:
        in_specs.append(pl.BlockSpec((bm, bn), lambda j, i, k: (i, j)))
        args.append(res)
    osz = jnp.dtype(out_dtype).itemsize
    nbytes = 2 * (bm * bk * 2 + bk * bn * 2 + bm * bn * osz) + 2 * bm * bn * 4
    if res is not None:
        nbytes += 2 * bm * bn * 4
    scratch = []
    if nk > 1:
        scratch.append(pltpu.VMEM((bm, bn), F32))
        nbytes += bm * bn * 4
    return pl.pallas_call(
        functools.partial(_mm_kernel, nk=nk, bk=bk, k_true=k_true, has_res=res is not None),
        grid=(n // bn, m // bm, nk),
        in_specs=in_specs,
        out_specs=pl.BlockSpec((bm, bn), lambda j, i, k: (i, j)),
        out_shape=jax.ShapeDtypeStruct((m, n), out_dtype),
        scratch_shapes=scratch,
        compiler_params=_params(nbytes + (4 << 20), 3),
        name="matmul",
    )(*args)


def _ffn_up_kernel(x_ref, wg_ref, wv0_ref, wv1_ref, cg_ref, cv0_ref, cv1_ref, o_ref, tail_g, tail_v,
                   *, bm, bn, d_ff):
    @pl.when(pl.program_id(1) == 0)
    def _():
        tail_g[...] = jnp.zeros_like(tail_g)
        tail_v[...] = jnp.zeros_like(tail_v)

    x = x_ref[...]

    def conv(u, c, tail):
        ext = jnp.concatenate([tail[...], u], axis=0)
        u1 = pltpu.roll(ext, 1, 0)[SUBLANES:]
        u2 = pltpu.roll(ext, 2, 0)[SUBLANES:]
        tail[...] = u[bm - SUBLANES:]
        return c[3:4] + c[2:3] * u + c[1:2] * u1 + c[0:1] * u2

    ug = jnp.dot(x, wg_ref[...], preferred_element_type=F32)
    uv = jnp.concatenate([jnp.dot(x, wv0_ref[...], preferred_element_type=F32),
                          jnp.dot(x, wv1_ref[...], preferred_element_type=F32)], axis=1)
    g = conv(ug, cg_ref[...], tail_g)
    v = conv(uv, jnp.concatenate([cv0_ref[...], cv1_ref[...]], axis=1), tail_v)
    col = lax.broadcasted_iota(jnp.int32, (bm, bn), 1) + pl.program_id(0) * bn
    o_ref[...] = jnp.where(col < d_ff, _silu(g) * v, 0.0).astype(o_ref.dtype)


def ffn_up(h, w_up, c8, d_ff, ffp):
    m, d = h.shape
    bm = min(MM_BM, m)
    bn = FFN_BN
    hb = bn // 2
    assert ffp % bn == 0 and d_ff % hb == 0
    v0 = d_ff // hb
    vlast = (2 * d_ff) // hb - 1
    vmap0 = lambda j, i: (0, jnp.minimum(v0 + 2 * j, vlast))
    vmap1 = lambda j, i: (0, jnp.minimum(v0 + 2 * j + 1, vlast))
    nbytes = 2 * (bm * d * 2 + 2 * d * bn * 2 + bm * bn * 2) + 10 * bm * bn * 4
    return pl.pallas_call(
        functools.partial(_ffn_up_kernel, bm=bm, bn=bn, d_ff=d_ff),
        grid=(ffp // bn, m // bm),
        in_specs=[pl.BlockSpec((bm, d), lambda j, i: (i, 0)),
                  pl.BlockSpec((d, bn), lambda j, i: (0, j)),
                  pl.BlockSpec((d, hb), vmap0),
                  pl.BlockSpec((d, hb), vmap1),
                  pl.BlockSpec((SUBLANES, bn), lambda j, i: (0, j)),
                  pl.BlockSpec((SUBLANES, hb), vmap0),
                  pl.BlockSpec((SUBLANES, hb), vmap1)],
        out_specs=pl.BlockSpec((bm, bn), lambda j, i: (i, j)),
        out_shape=jax.ShapeDtypeStruct((m, ffp), BF16),
        scratch_shapes=[pltpu.VMEM((SUBLANES, bn), F32), pltpu.VMEM((SUBLANES, bn), F32)],
        compiler_params=_params(nbytes, 2),
        name="ffn_up",
    )(h, w_up, w_up, w_up, c8, c8, c8)


def _dwconv_silu_kernel(x_ref, w_ref, o_ref, tail, *, taps, rb):
    @pl.when(pl.program_id(1) == 0)
    def _():
        tail[...] = jnp.zeros_like(tail)

    x = x_ref[...]
    ext = jnp.concatenate([tail[...], x], axis=0)
    w = w_ref[...]
    out = w[taps:taps + 1] + w[taps - 1:taps] * x
    for k in range(taps - 1):
        d = taps - 1 - k
        out = out + pltpu.roll(ext, d, 0)[SUBLANES:] * w[k:k + 1]
    tail[...] = x[rb - SUBLANES:]
    o_ref[...] = _silu(out).astype(o_ref.dtype)


def dwconv_silu(x, col0, ncols, w8, taps):
    m = x.shape[0]
    rb = min(ROW_BLK, m)
    cb = 1024
    assert col0 % cb == 0 and ncols % cb == 0
    off = col0 // cb
    return pl.pallas_call(
        functools.partial(_dwconv_silu_kernel, taps=taps, rb=rb),
        grid=(ncols // cb, m // rb),
        in_specs=[pl.BlockSpec((rb, cb), lambda j, i: (i, j + off)),
                  pl.BlockSpec((SUBLANES, cb), lambda j, i: (0, j))],
        out_specs=pl.BlockSpec((rb, cb), lambda j, i: (i, j)),
        out_shape=jax.ShapeDtypeStruct((m, ncols), F32),
        scratch_shapes=[pltpu.VMEM((SUBLANES, cb), F32)],
        compiler_params=_params(16 * rb * cb * 4, 2),
        name="dwconv_silu",
    )(x, w8)


def _rope_kernel(x_ref, cos_ref, sa_ref, sb_ref, o_ref, *, groups, q_blocks, q_scale):
    cos = cos_ref[...]
    sa = sa_ref[...]
    sb = sb_ref[...]
    half = ROPE_DIM // 2
    scale = jnp.where(pl.program_id(1) < q_blocks, q_scale, 1.0)
    for gi in range(groups):
        t = x_ref[:, gi * LANES:(gi + 1) * LANES]
        up = pltpu.roll(t, LANES - half, 1)
        dn = pltpu.roll(t, half, 1)
        o_ref[:, gi * LANES:(gi + 1) * LANES] = ((t * cos + up * sa + dn * sb) * scale).astype(o_ref.dtype)


def rope(x, cos, sa, sb, q_cols, q_scale):
    m, c = x.shape
    rb = min(ROW_BLK, m)
    cb = 1024
    assert q_cols % cb == 0
    tab = pl.BlockSpec((rb, LANES), lambda i, j: (i, 0))
    return pl.pallas_call(
        functools.partial(_rope_kernel, groups=cb // LANES, q_blocks=q_cols // cb, q_scale=q_scale),
        grid=(m // rb, c // cb),
        in_specs=[pl.BlockSpec((rb, cb), lambda i, j: (i, j)), tab, tab, tab],
        out_specs=pl.BlockSpec((rb, cb), lambda i, j: (i, j)),
        out_shape=jax.ShapeDtypeStruct((m, c), BF16),
        compiler_params=_params(12 * rb * cb * 4, 2),
        name="rope",
    )(x, cos, sa, sb)


def rope_tables(positions):
    half = ROPE_DIM // 2
    inv_freq = jnp.power(jnp.float32(ROPE_THETA), -jnp.arange(half, dtype=F32) * (2.0 / ROPE_DIM))
    ang = positions.astype(F32)[:, None] * inv_freq
    cos, sin = jnp.cos(ang), jnp.sin(ang)
    s = positions.shape[0]
    pad = LANES - ROPE_DIM
    cos_t = jnp.concatenate([cos, cos, jnp.ones((s, pad), F32)], axis=1)
    sa = jnp.concatenate([-sin, jnp.zeros((s, LANES - half), F32)], axis=1)
    sb = jnp.concatenate([jnp.zeros((s, half), F32), sin, jnp.zeros((s, pad), F32)], axis=1)
    return cos_t, sa, sb


_NT = (((1,), (1,)), ((), ()))


def _flash_softmax(s, m_ref, l_ref, rows):
    tk = s.shape[1]
    sc = [s[:, c * LANES:(c + 1) * LANES] for c in range(tk // LANES)]
    m_prev = m_ref[rows, :]
    m_new = jnp.maximum(m_prev, jnp.max(functools.reduce(jnp.maximum, sc), axis=-1, keepdims=True))
    alpha = jnp.exp2(m_prev - m_new)
    pc = [jnp.exp2(c - m_new) for c in sc]
    l_ref[rows, :] = alpha * l_ref[rows, :] + functools.reduce(jnp.add, pc)
    m_ref[rows, :] = m_new
    return jnp.concatenate([c.astype(BF16) for c in pc], axis=1), alpha


def _flash_accumulate(p, alpha, v, acc_ref, rows):
    pv = jnp.dot(p, v, preferred_element_type=F32)
    a = jnp.concatenate([alpha] * (pv.shape[1] // LANES), axis=1) if pv.shape[1] > LANES else alpha
    acc_ref[rows, :] = a * acc_ref[rows, :] + pv


def _flash_block(chains, v):
    pa = [_flash_softmax(s, m_ref, l_ref, rows) for s, m_ref, l_ref, _, rows in chains]
    for (p, alpha), (_, _, _, acc_ref, rows) in zip(pa, chains):
        _flash_accumulate(p, alpha, v, acc_ref, rows)


def _flash_init(m_ref, l_ref, acc_ref):
    m_ref[...] = jnp.full(m_ref.shape, -jnp.inf, F32)
    l_ref[...] = jnp.zeros(l_ref.shape, F32)
    acc_ref[...] = jnp.zeros(acc_ref.shape, F32)


def _flash_result(l_ref, acc_ref):
    return acc_ref[...] / jnp.sum(l_ref[...], axis=-1, keepdims=True)


def _causal_mask(tr, tk, delta):
    r = lax.broadcasted_iota(jnp.int32, (tr, tk), 0)
    c = lax.broadcasted_iota(jnp.int32, (tr, tk), 1)
    return c <= r + delta


def _diag_plan(tq, tk, tr):
    plan = []
    for co in range(0, tq, tk):
        subs = []
        for r0 in range(0, tq, tr):
            if r0 + tr - 1 < co:
                continue
            subs.append((r0, None if co + tk - 1 <= r0 else r0 - co))
        plan.append(subs)
    return plan


def _causal_sweep(qi, tq, tk, tr, block):
    full = [(r0, None) for r0 in range(0, tq, tr)]
    per_q = tq // tk

    def body(j, carry):
        block(j, full)
        return carry

    lax.fori_loop(0, qi * per_q, body, 0)
    for b, subs in enumerate(_diag_plan(tq, tk, tr)):
        block(qi * per_q + b, subs)


def _fox_kernel(q_ref, k_ref, v_ref, b_ref, o_ref, q_scr, m_ref, l_ref, acc_ref, *, tq, tk, tr, q_scale):
    q_scr[...] = (q_ref[...] * q_scale).astype(BF16)
    _flash_init(m_ref, l_ref, acc_ref)

    def block(j, subs):
        off = pl.multiple_of(j * tk, tk)
        k = k_ref[pl.ds(off, tk), :]
        v = v_ref[pl.ds(off, tk), :]
        bias = b_ref[:, pl.ds(off, tk)] * LOG2E
        chains = []
        for r0, delta in subs:
            rows = pl.ds(r0, tr)
            s = lax.dot_general(q_scr[rows, :], k, _NT, preferred_element_type=F32) + bias
            if delta is not None:
                s = jnp.where(_causal_mask(tr, tk, delta), s, -jnp.inf)
            chains.append((s, m_ref, l_ref, acc_ref, rows))
        _flash_block(chains, v)

    _causal_sweep(pl.program_id(1), tq, tk, tr, block)
    o_ref[...] = _flash_result(l_ref, acc_ref).astype(o_ref.dtype)


def _att_tiles(s):
    tq = min(ATT_TQ, s)
    tk = min(ATT_TK, tq)
    tr = min(ATT_SUB, tk)
    assert s % tq == 0 and tq % tk == 0 and tk % tr == 0
    return tq, tk, tr


def fox_attention(q, kv, negcum, n_heads):
    s = q.shape[0]
    tq, tk, tr = _att_tiles(s)
    hd = HEAD_DIM
    nbytes = 2 * (2 * s * hd * 2 + tq * hd * 4 + tq * hd * 2 + s * 4) + 12 * tq * tk * 4
    return pl.pallas_call(
        functools.partial(_fox_kernel, tq=tq, tk=tk, tr=tr, q_scale=hd ** -0.5 * LOG2E),
        grid=(n_heads, s // tq),
        in_specs=[pl.BlockSpec((tq, hd), lambda h, i: (i, h)),
                  pl.BlockSpec((s, hd), lambda h, i: (0, h)),
                  pl.BlockSpec((s, hd), lambda h, i: (0, n_heads + h)),
                  pl.BlockSpec((None, 1, s), lambda h, i: (h, 0, 0))],
        out_specs=pl.BlockSpec((tq, hd), lambda h, i: (i, h)),
        out_shape=jax.ShapeDtypeStruct((s, n_heads * hd), BF16),
        scratch_shapes=[pltpu.VMEM((tq, hd), BF16), pltpu.VMEM((tq, LANES), F32),
                        pltpu.VMEM((tq, LANES), F32), pltpu.VMEM((tq, hd), F32)],
        compiler_params=_params(nbytes, 2),
        name="fox_attention",
    )(q, kv, kv, negcum)


def _diff_kernel(q_ref, k_ref, v_ref, lam_ref, g_ref, o_ref, m0, l0, a0, m1, l1, a1, *, tq, tk, tr, lam_init):
    hd = HEAD_DIM
    _flash_init(m0, l0, a0)
    _flash_init(m1, l1, a1)

    def block(j, subs):
        off = pl.multiple_of(j * tk, tk)
        v = v_ref[pl.ds(off, tk), :]
        chains = []
        for lo, st in ((0, (m0, l0, a0)), (hd, (m1, l1, a1))):
            k = k_ref[pl.ds(off, tk), lo:lo + hd]
            for r0, delta in subs:
                rows = pl.ds(r0, tr)
                s = lax.dot_general(q_ref[rows, lo:lo + hd], k, _NT, preferred_element_type=F32)
                if delta is not None:
                    s = jnp.where(_causal_mask(tr, tk, delta), s, -jnp.inf)
                chains.append((s,) + st + (rows,))
        _flash_block(chains, v)

    _causal_sweep(pl.program_id(1), tq, tk, tr, block)
    lv = lam_ref[...]
    lam = (jnp.exp(jnp.sum(lv[0:1] * lv[1:2], axis=-1, keepdims=True))
           - jnp.exp(jnp.sum(lv[2:3] * lv[3:4], axis=-1, keepdims=True)) + lam_init)
    o = _flash_result(l0, a0) - lam * _flash_result(l1, a1)
    o = o * lax.rsqrt(jnp.mean(o * o, axis=-1, keepdims=True) + NORM_EPS) * g_ref[...]
    o_ref[...] = (o * (1.0 - lam_init)).astype(o_ref.dtype)


def diff_attention(qk, v, lam_vecs, subln_g, lam_init, n_heads):
    s = qk.shape[0]
    tq, tk, tr = _att_tiles(s)
    t = tq
    w = 2 * HEAD_DIM
    nbytes = 2 * (2 * s * w * 2 + 2 * tq * w * 2) + 24 * tq * tk * 4
    stats = lambda: [pltpu.VMEM((t, LANES), F32), pltpu.VMEM((t, LANES), F32), pltpu.VMEM((t, w), F32)]
    return pl.pallas_call(
        functools.partial(_diff_kernel, tq=tq, tk=tk, tr=tr, lam_init=lam_init),
        grid=(n_heads, s // t),
        in_specs=[pl.BlockSpec((t, w), lambda h, i: (i, h)),
                  pl.BlockSpec((s, w), lambda h, i: (0, n_heads + h)),
                  pl.BlockSpec((s, w), lambda h, i: (0, h)),
                  pl.BlockSpec((4, HEAD_DIM), lambda h, i: (0, 0)),
                  pl.BlockSpec((1, w), lambda h, i: (0, 0))],
        out_specs=pl.BlockSpec((t, w), lambda h, i: (i, h)),
        out_shape=jax.ShapeDtypeStruct((s, n_heads * w), BF16),
        scratch_shapes=stats() + stats(),
        compiler_params=_params(nbytes, 2),
        name="diff_attention",
    )(qk, qk, v, lam_vecs.astype(F32), subln_g.reshape(1, w).astype(F32))


def _mem_attn_kernel(q_ref, k_ref, v_ref, o_ref, *, scale):
    s = lax.dot_general(q_ref[...], k_ref[...], _NT, preferred_element_type=F32) * scale
    p = jnp.exp(s - jnp.max(s, axis=-1, keepdims=True))
    l = jnp.sum(p, axis=-1, keepdims=True)
    o = jnp.dot(p.astype(BF16), v_ref[...], preferred_element_type=F32)
    o_ref[...] = (o / l).astype(o_ref.dtype)


def mem_attention(qarr, q_col0, mem_kv):
    s = qarr.shape[0]
    mem_width = mem_kv.shape[1] // 2
    hd = mem_width // MEM_HEADS
    mt = mem_kv.shape[0]
    tq = min(MEM_TQ, s)
    q_off = q_col0 // hd
    assert q_col0 % hd == 0
    return pl.pallas_call(
        functools.partial(_mem_attn_kernel, scale=hd ** -0.5),
        grid=(s // tq, MEM_HEADS),
        in_specs=[pl.BlockSpec((tq, hd), lambda i, h: (i, q_off + h)),
                  pl.BlockSpec((mt, hd), lambda i, h: (0, h)),
                  pl.BlockSpec((mt, hd), lambda i, h: (0, MEM_HEADS + h))],
        out_specs=pl.BlockSpec((tq, hd), lambda i, h: (i, h)),
        out_shape=jax.ShapeDtypeStruct((s, mem_width), BF16),
        compiler_params=_params(32 * tq * hd * 4, 2),
        name="mem_attention",
    )(qarr, mem_kv, mem_kv)


def _fox_cum_kernel(x_ref, b_ref, o_ref, carry, *, cb):
    @pl.when(pl.program_id(0) == 0)
    def _():
        carry[...] = jnp.zeros_like(carry)

    x = x_ref[...] + b_ref[...]
    log_f = jnp.minimum(x, 0.0) - jnp.log1p(jnp.exp(-jnp.abs(x)))
    cs = _cumsum_rows(_tri_lower(cb), log_f) + carry[...]
    o_ref[...] = cs
    carry[...] = cs[cb - 1:cb, :]


def fox_cum(logits, bias):
    s = logits.shape[0]
    cb = min(CUM_BLK, s)
    return pl.pallas_call(
        functools.partial(_fox_cum_kernel, cb=cb),
        grid=(s // cb,),
        in_specs=[pl.BlockSpec((cb, LANES), lambda i: (i, 0)),
                  pl.BlockSpec((1, LANES), lambda i: (0, 0))],
        out_specs=pl.BlockSpec((cb, LANES), lambda i: (i, 0)),
        out_shape=jax.ShapeDtypeStruct((s, LANES), F32),
        scratch_shapes=[pltpu.VMEM((1, LANES), F32)],
        compiler_params=_params(16 << 20, 1),
        name="fox_cum",
    )(logits, bias)


def _ssd_kernel(z_ref, xs_ref, b_ref, c_ref, dte_ref, dtr_ref, bias_e_ref, alog_e_ref, dskip_e_ref,
                bias_r_ref, alog_r_ref, ng_ref, o_ref, h_ref, *, heads, hdim, chunk):
    @pl.when(pl.program_id(1) == 0)
    def _():
        h_ref[...] = jnp.zeros_like(h_ref)

    L = chunk
    width = heads * hdim
    xs = xs_ref[...]
    bm = b_ref[...]
    cm = c_ref[...]
    tri = _tri_lower(L)

    dt_e = _softplus(dte_ref[...] + bias_e_ref[...])
    a_e = dt_e * (-jnp.exp(alog_e_ref[...]))
    acs_e = _cumsum_rows(tri, a_e)
    acs_last = acs_e[L - 1:L, :]
    dt_r = _softplus(dtr_ref[...] + bias_r_ref[...])
    a_r = dt_r * (-jnp.exp(alog_r_ref[...]))
    acs_r = _cumsum_lanes(a_r, tri.T)

    xd = xs * dt_e
    xd_b = xd.astype(BF16)
    cb = lax.dot_general(cm.astype(BF16), bm.astype(BF16), _NT, preferred_element_type=F32)
    causal = _causal_mask(L, L, 0)
    head_id = lax.broadcasted_iota(jnp.int32, (L, width), 1) // hdim

    y = jnp.zeros((L, width), F32)
    for r in range(heads):
        seg = acs_e[:, r * hdim:r * hdim + 1] - acs_r[r:r + 1, :]
        decay = jnp.exp(jnp.where(causal, seg, -jnp.inf))
        yr = jnp.dot((cb * decay).astype(BF16), xd_b, preferred_element_type=F32)
        y = jnp.where(head_id == r, yr, y)

    h_in = h_ref[...]
    y_off = jnp.dot(cm.astype(BF16), h_in.astype(BF16), preferred_element_type=F32) * jnp.exp(acs_e)
    to_end = jnp.exp(acs_last - acs_e)
    states = jnp.dot(bm.T.astype(BF16), (xd * to_end).astype(BF16), preferred_element_type=F32)
    h_ref[...] = h_in * jnp.exp(acs_last) + states

    y = y + y_off + xs * dskip_e_ref[...]
    g = y * _silu(z_ref[...])
    g = g * lax.rsqrt(jnp.mean(g * g, axis=-1, keepdims=True) + NORM_EPS)
    o_ref[...] = (g * ng_ref[...]).astype(o_ref.dtype)


def ssd_scan(zx, xbc, dt_e, dt_r, bias_e, alog_e, dskip_e, bias_r, alog_r, norm_g, inner):
    s = zx.shape[0]
    L = SSD_CHUNK
    G = SSD_GROUPS
    width = inner // G
    heads = width // SSD_HEAD_DIM
    N = SSD_STATE
    b_off = inner // N
    c_off = b_off + G
    row = lambda g, c: (0, g)
    return pl.pallas_call(
        functools.partial(_ssd_kernel, heads=heads, hdim=SSD_HEAD_DIM, chunk=L),
        grid=(G, s // L),
        in_specs=[pl.BlockSpec((L, width), lambda g, c: (c, g)),
                  pl.BlockSpec((L, width), lambda g, c: (c, g)),
                  pl.BlockSpec((L, N), lambda g, c: (c, b_off + g)),
                  pl.BlockSpec((L, N), lambda g, c: (c, c_off + g)),
                  pl.BlockSpec((L, width), lambda g, c: (c, g)),
                  pl.BlockSpec((None, SUBLANES, L), lambda g, c: (g, 0, c)),
                  pl.BlockSpec((1, width), row),
                  pl.BlockSpec((1, width), row),
                  pl.BlockSpec((1, width), row),
                  pl.BlockSpec((None, SUBLANES, 1), lambda g, c: (g, 0, 0)),
                  pl.BlockSpec((None, SUBLANES, 1), lambda g, c: (g, 0, 0)),
                  pl.BlockSpec((1, width), row)],
        out_specs=pl.BlockSpec((L, width), lambda g, c: (c, g)),
        out_shape=jax.ShapeDtypeStruct((s, inner), BF16),
        scratch_shapes=[pltpu.VMEM((N, width), F32)],
        compiler_params=_params(32 << 20, 2),
        name="ssd_scan",
    )(zx, xbc, xbc, xbc, dt_e, dt_r, bias_e, alog_e, dskip_e, bias_r, alog_r, norm_g)


def _conv_rows(w, b):
    taps, c = w.shape
    return jnp.concatenate([w.astype(F32), b.reshape(1, c).astype(F32),
                            jnp.zeros((SUBLANES - taps - 1, c), F32)], axis=0)


def _pad_cols(a, n):
    return jnp.pad(a, ((0, 0), (0, n - a.shape[1])))


def kernel(x, mem, positions, norm_mix, norm_mem, w_mem_kv, w_out, norm_ffn, w_up, conv_ffn_w, conv_ffn_b,
           w_down, a_w_in, a_lambda, a_subln, b_w_in, b_forget_bias, c_w_in, c_conv_w, c_conv_b, c_dt_bias,
           c_a_log, c_d_skip, c_norm_gate, final_norm):
    batch, seq, d_model = x.shape
    depth = norm_mix.shape[0]
    mem_width = w_mem_kv.shape[2] // 2
    tok_width = d_model - mem_width
    d_ff = w_down.shape[1]
    ffp = DOWN_BK * ((d_ff + DOWN_BK - 1) // DOWN_BK)
    diff_heads = tok_width // (2 * HEAD_DIM)
    fox_heads = tok_width // HEAD_DIM
    ssd_heads = tok_width // SSD_HEAD_DIM
    heads_per_group = ssd_heads // SSD_GROUPS
    conv_ch = tok_width + 2 * SSD_GROUPS * SSD_STATE
    q_scale = HEAD_DIM ** -0.5 * LOG2E

    outs = []
    for b in range(batch):
        xb = x[b]
        mem_b = mem[b]
        cos_t, sa, sb = rope_tables(positions[b])
        for i in range(depth):
            kind, j = i % N_MIXERS, i // N_MIXERS
            h = rmsnorm(xb, norm_mix[i], BF16)
            if kind == 0:
                w = a_w_in[j].astype(BF16)
                qk = matmul(h, w, F32, 0, 2 * tok_width)
                vq = matmul(h, w, BF16, 2 * tok_width, tok_width + mem_width)
                qk = rope(qk, cos_t, sa, sb, tok_width, q_scale)
                lam_init = 0.8 - 0.6 * math.exp(-0.3 * i)
                tok = diff_attention(qk, vq, a_lambda[j], a_subln[j], lam_init, diff_heads)
                q_arr, q_col0 = vq, tok_width
            elif kind == 1:
                w = b_w_in[j]
                w_main = w[:, :3 * tok_width].astype(BF16)
                q = matmul(h, w_main, F32, 0, tok_width)
                kv = matmul(h, w_main, BF16, tok_width, 2 * tok_width)
                w_gate = _pad_cols(w[:, 3 * tok_width:3 * tok_width + fox_heads], LANES).astype(BF16)
                logits = matmul(h, w_gate, F32)
                q_arr = matmul(h, w[:, 3 * tok_width + fox_heads:].astype(BF16), BF16)
                q_col0 = 0
                bias = _pad_cols(b_forget_bias[j].reshape(1, fox_heads).astype(F32), LANES)
                cum = fox_cum(logits, bias)
                negcum = (-cum[:, :fox_heads]).T.reshape(fox_heads, 1, seq)
                tok = fox_attention(q, kv, negcum, fox_heads)
            else:
                w = c_w_in[j]
                n_zx = tok_width + conv_ch
                zx = matmul(h, w[:, :n_zx].astype(BF16), F32)
                dt_raw = matmul(h, _pad_cols(w[:, n_zx:n_zx + ssd_heads], LANES).astype(BF16), F32)[:, :ssd_heads]
                q_arr = matmul(h, w[:, n_zx + ssd_heads:].astype(BF16), BF16)
                q_col0 = 0
                xbc = dwconv_silu(zx, tok_width, conv_ch, _conv_rows(c_conv_w[j], c_conv_b[j]), SSD_CONV)
                rep = lambda a: jnp.repeat(a.astype(F32), SSD_HEAD_DIM, axis=-1)
                dt_e = rep(dt_raw)
                pad_h = SUBLANES - heads_per_group
                by_group = lambda a: jnp.pad(a.astype(F32).reshape(SSD_GROUPS, heads_per_group, -1),
                                             ((0, 0), (0, pad_h), (0, 0)))
                dt_r = by_group(dt_raw.T)
                tok = ssd_scan(zx, xbc, dt_e, dt_r,
                               rep(c_dt_bias[j].reshape(1, -1)), rep(c_a_log[j].reshape(1, -1)),
                               rep(c_d_skip[j].reshape(1, -1)),
                               by_group(c_dt_bias[j].reshape(-1, 1)), by_group(c_a_log[j].reshape(-1, 1)),
                               c_norm_gate[j].reshape(1, -1).astype(F32), tok_width)
            mem_n = rmsnorm(mem_b, norm_mem[i], BF16)
            mem_kv = matmul(mem_n, w_mem_kv[i].astype(BF16), BF16)
            ctx = mem_attention(q_arr, q_col0, mem_kv)
            mix = jnp.concatenate([tok, ctx], axis=-1)
            xb = matmul(mix, w_out[i].astype(BF16), F32, res=xb)

            hf = rmsnorm(xb, norm_ffn[i], BF16)
            hidden = ffn_up(hf, w_up[i].astype(BF16), _conv_rows(conv_ffn_w[i], conv_ffn_b[i]), d_ff, ffp)
            xb = matmul(hidden, w_down[i].astype(BF16), F32, res=xb, bk=DOWN_BK)
        outs.append(rmsnorm(xb, final_norm, x.dtype))
    return jnp.stack(outs, axis=0)
```

```python
import functools
import math

import jax
import jax.numpy as jnp
from jax import lax
from jax.experimental import pallas as pl
from jax.experimental.pallas import tpu as pltpu

F32 = jnp.float32
BF16 = jnp.bfloat16

HEAD_DIM = 128
MEM_HEADS = 4
SSD_HEAD_DIM = 64
SSD_GROUPS = 8
SSD_STATE = 128
SSD_CONV = 4
SSD_CHUNK = 128
ROPE_THETA = 500000.0
ROPE_DIM = HEAD_DIM // 4
FFN_CONV = 3
NORM_EPS = 1e-6
N_MIXERS = 3
LOG2E = math.log2(math.e)

LANES = 128
SUBLANES = 8
VMEM_LIMIT_CAP = 56 * 1024 * 1024

MM_BM = 1024
MM_BN = 1024
FFN_BN = 512
DOWN_BK = 2816
DIFF_TQ = 1024
FOX_TQ = 2048
ATT_TK = 512
ATT_SUB = 256
MEM_TQ = 512
ROW_BLK = 256
CUM_BLK = 256


def _vmem(nbytes):
    return int(min(VMEM_LIMIT_CAP, max(16 * 1024 * 1024, nbytes)))


def _params(nbytes, n_grid):
    return pltpu.CompilerParams(
        dimension_semantics=("arbitrary",) * n_grid, vmem_limit_bytes=_vmem(nbytes))


def _softplus(x):
    return jnp.maximum(x, 0.0) + jnp.log1p(jnp.exp(-jnp.abs(x)))


def _silu(x):
    return x / (1.0 + jnp.exp(-x))


def _split3(x):
    hi = x.astype(BF16)
    r = x - hi.astype(F32)
    mid = r.astype(BF16)
    lo = (r - mid.astype(F32)).astype(BF16)
    return hi, mid, lo


def _tri_lower(n):
    r = lax.broadcasted_iota(jnp.int32, (n, n), 0)
    c = lax.broadcasted_iota(jnp.int32, (n, n), 1)
    return jnp.where(c <= r, 1.0, 0.0).astype(BF16)


def _cumsum_rows(tri, x):
    hi, mid, lo = _split3(x)
    d = lambda a: jnp.dot(tri, a, preferred_element_type=F32)
    return d(hi) + d(mid) + d(lo)


def _cumsum_lanes(x, tri_t):
    hi, mid, lo = _split3(x)
    d = lambda a: jnp.dot(a, tri_t, preferred_element_type=F32)
    return d(hi) + d(mid) + d(lo)


def _rmsnorm_kernel(x_ref, g_ref, o_ref):
    x = x_ref[...].astype(F32)
    ms = jnp.mean(x * x, axis=-1, keepdims=True)
    o_ref[...] = (x * lax.rsqrt(ms + NORM_EPS) * g_ref[...]).astype(o_ref.dtype)


def rmsnorm(x, g, out_dtype):
    m, d = x.shape
    rb = min(ROW_BLK, m)
    return pl.pallas_call(
        _rmsnorm_kernel,
        grid=(m // rb,),
        in_specs=[pl.BlockSpec((rb, d), lambda i: (i, 0)),
                  pl.BlockSpec((1, d), lambda i: (0, 0))],
        out_specs=pl.BlockSpec((rb, d), lambda i: (i, 0)),
        out_shape=jax.ShapeDtypeStruct((m, d), out_dtype),
        compiler_params=_params(6 * rb * d * 4, 1),
        name="rmsnorm",
    )(x, g.reshape(1, d).astype(F32))


def _mm_kernel(*refs, nk, bk, k_true, has_res):
    if has_res:
        x_ref, w_ref, r_ref, o_ref = refs[:4]
        scratch = refs[4:]
    else:
        x_ref, w_ref, o_ref = refs[:3]
        r_ref = None
        scratch = refs[3:]
    w = w_ref[...]
    if k_true is not None:
        row = lax.broadcasted_iota(jnp.int32, w.shape, 0) + pl.program_id(2) * bk
        w = jnp.where(row < k_true, w, jnp.zeros_like(w))
    part = jnp.dot(x_ref[...], w, preferred_element_type=F32)
    if nk == 1:
        if has_res:
            part = part + r_ref[...]
        o_ref[...] = part.astype(o_ref.dtype)
        return
    acc_ref = scratch[0]
    k = pl.program_id(2)

    @pl.when(k == 0)
    def _():
        acc_ref[...] = part

    @pl.when(jnp.logical_and(k > 0, k < nk - 1))
    def _():
        acc_ref[...] += part

    @pl.when(k == nk - 1)
    def _():
        tot = acc_ref[...] + part
        if has_res:
            tot = tot + r_ref[...]
        o_ref[...] = tot.astype(o_ref.dtype)


def matmul(x, w, out_dtype, col0=0, ncols=None, res=None, bk=None):
    m, kdim = x.shape
    n = w.shape[1] - col0 if ncols is None else ncols
    bm = min(MM_BM, m)
    bn = min(MM_BN, n)
    bk = kdim if bk is None else bk
    nk = kdim // bk
    assert m % bm == 0 and n % bn == 0 and kdim % bk == 0 and col0 % bn == 0
    k_true = None if w.shape[0] == kdim else w.shape[0]
    assert w.shape[0] <= kdim and w.shape[0] > (nk - 1) * bk
    joff = col0 // bn
    in_specs = [pl.BlockSpec((bm, bk), lambda j, i, k: (i, k)),
                pl.BlockSpec((bk, bn), lambda j, i, k: (k, j + joff))]
    args = [x, w]
    if res is not None:
        in_specs.append(pl.BlockSpec((bm, bn), lambda j, i, k: (i, j)))
        args.append(res)
    osz = jnp.dtype(out_dtype).itemsize
    nbytes = 2 * (bm * bk * 2 + bk * bn * 2 + bm * bn * osz) + 2 * bm * bn * 4
    if res is not None:
        nbytes += 2 * bm * bn * 4
    scratch = []
    if nk > 1:
        scratch.append(pltpu.VMEM((bm, bn), F32))
        nbytes += bm * bn * 4
    return pl.pallas_call(
        functools.partial(_mm_kernel, nk=nk, bk=bk, k_true=k_true, has_res=res is not None),
        grid=(n // bn, m // bm, nk),
        in_specs=in_specs,
        out_specs=pl.BlockSpec((bm, bn), lambda j, i, k: (i, j)),
        out_shape=jax.ShapeDtypeStruct((m, n), out_dtype),
        scratch_shapes=scratch,
        compiler_params=_params(nbytes + (4 << 20), 3),
        name="matmul",
    )(*args)


def _ffn_up_kernel(x_ref, wg_ref, wv0_ref, wv1_ref, cg_ref, cv0_ref, cv1_ref, o_ref, tail_g, tail_v,
                   *, bm, bn, d_ff):
    @pl.when(pl.program_id(1) == 0)
    def _():
        tail_g[...] = jnp.zeros_like(tail_g)
        tail_v[...] = jnp.zeros_like(tail_v)

    x = x_ref[...]

    def conv(u, c, tail):
        ext = jnp.concatenate([tail[...], u], axis=0)
        u1 = pltpu.roll(ext, 1, 0)[SUBLANES:]
        u2 = pltpu.roll(ext, 2, 0)[SUBLANES:]
        tail[...] = u[bm - SUBLANES:]
        return c[3:4] + c[2:3] * u + c[1:2] * u1 + c[0:1] * u2

    ug = jnp.dot(x, wg_ref[...], preferred_element_type=F32)
    uv = jnp.concatenate([jnp.dot(x, wv0_ref[...], preferred_element_type=F32),
                          jnp.dot(x, wv1_ref[...], preferred_element_type=F32)], axis=1)
    g = conv(ug, cg_ref[...], tail_g)
    v = conv(uv, jnp.concatenate([cv0_ref[...], cv1_ref[...]], axis=1), tail_v)
    col = lax.broadcasted_iota(jnp.int32, (bm, bn), 1) + pl.program_id(0) * bn
    o_ref[...] = jnp.where(col < d_ff, _silu(g) * v, 0.0).astype(o_ref.dtype)


def ffn_up(h, w_up, c8, d_ff, ffp):
    m, d = h.shape
    bm = min(MM_BM, m)
    bn = FFN_BN
    hb = bn // 2
    assert ffp % bn == 0 and d_ff % hb == 0
    v0 = d_ff // hb
    vlast = (2 * d_ff) // hb - 1
    vmap0 = lambda j, i: (0, jnp.minimum(v0 + 2 * j, vlast))
    vmap1 = lambda j, i: (0, jnp.minimum(v0 + 2 * j + 1, vlast))
    nbytes = 2 * (bm * d * 2 + 2 * d * bn * 2 + bm * bn * 2) + 10 * bm * bn * 4
    return pl.pallas_call(
        functools.partial(_ffn_up_kernel, bm=bm, bn=bn, d_ff=d_ff),
        grid=(ffp // bn, m // bm),
        in_specs=[pl.BlockSpec((bm, d), lambda j, i: (i, 0)),
                  pl.BlockSpec((d, bn), lambda j, i: (0, j)),
                  pl.BlockSpec((d, hb), vmap0),
                  pl.BlockSpec((d, hb), vmap1),
                  pl.BlockSpec((SUBLANES, bn), lambda j, i: (0, j)),
                  pl.BlockSpec((SUBLANES, hb), vmap0),
                  pl.BlockSpec((SUBLANES, hb), vmap1)],
        out_specs=pl.BlockSpec((bm, bn), lambda j, i: (i, j)),
        out_shape=jax.ShapeDtypeStruct((m, ffp), BF16),
        scratch_shapes=[pltpu.VMEM((SUBLANES, bn), F32), pltpu.VMEM((SUBLANES, bn), F32)],
        compiler_params=_params(nbytes, 2),
        name="ffn_up",
    )(h, w_up, w_up, w_up, c8, c8, c8)


def _dwconv_silu_kernel(x_ref, w_ref, o_ref, tail, *, taps, rb):
    @pl.when(pl.program_id(1) == 0)
    def _():
        tail[...] = jnp.zeros_like(tail)

    x = x_ref[...]
    ext = jnp.concatenate([tail[...], x], axis=0)
    w = w_ref[...]
    out = w[taps:taps + 1] + w[taps - 1:taps] * x
    for k in range(taps - 1):
        d = taps - 1 - k
        out = out + pltpu.roll(ext, d, 0)[SUBLANES:] * w[k:k + 1]
    tail[...] = x[rb - SUBLANES:]
    o_ref[...] = _silu(out).astype(o_ref.dtype)


def dwconv_silu(x, col0, ncols, w8, taps):
    m = x.shape[0]
    rb = min(ROW_BLK, m)
    cb = 1024
    assert col0 % cb == 0 and ncols % cb == 0
    off = col0 // cb
    return pl.pallas_call(
        functools.partial(_dwconv_silu_kernel, taps=taps, rb=rb),
        grid=(ncols // cb, m // rb),
        in_specs=[pl.BlockSpec((rb, cb), lambda j, i: (i, j + off)),
                  pl.BlockSpec((SUBLANES, cb), lambda j, i: (0, j))],
        out_specs=pl.BlockSpec((rb, cb), lambda j, i: (i, j)),
        out_shape=jax.ShapeDtypeStruct((m, ncols), F32),
        scratch_shapes=[pltpu.VMEM((SUBLANES, cb), F32)],
        compiler_params=_params(16 * rb * cb * 4, 2),
        name="dwconv_silu",
    )(x, w8)


def _rope_kernel(x_ref, cos_ref, sa_ref, sb_ref, o_ref, *, groups, q_blocks, q_scale):
    cos = cos_ref[...]
    sa = sa_ref[...]
    sb = sb_ref[...]
    half = ROPE_DIM // 2
    scale = jnp.where(pl.program_id(1) < q_blocks, q_scale, 1.0)
    for gi in range(groups):
        t = x_ref[:, gi * LANES:(gi + 1) * LANES]
        up = pltpu.roll(t, LANES - half, 1)
        dn = pltpu.roll(t, half, 1)
        o_ref[:, gi * LANES:(gi + 1) * LANES] = ((t * cos + up * sa + dn * sb) * scale).astype(o_ref.dtype)


def rope(x, cos, sa, sb, q_cols, q_scale):
    m, c = x.shape
    rb = min(ROW_BLK, m)
    cb = 1024
    assert q_cols % cb == 0
    tab = pl.BlockSpec((rb, LANES), lambda i, j: (i, 0))
    return pl.pallas_call(
        functools.partial(_rope_kernel, groups=cb // LANES, q_blocks=q_cols // cb, q_scale=q_scale),
        grid=(m // rb, c // cb),
        in_specs=[pl.BlockSpec((rb, cb), lambda i, j: (i, j)), tab, tab, tab],
        out_specs=pl.BlockSpec((rb, cb), lambda i, j: (i, j)),
        out_shape=jax.ShapeDtypeStruct((m, c), BF16),
        compiler_params=_params(12 * rb * cb * 4, 2),
        name="rope",
    )(x, cos, sa, sb)


def rope_tables(positions):
    half = ROPE_DIM // 2
    inv_freq = jnp.power(jnp.float32(ROPE_THETA), -jnp.arange(half, dtype=F32) * (2.0 / ROPE_DIM))
    ang = positions.astype(F32)[:, None] * inv_freq
    cos, sin = jnp.cos(ang), jnp.sin(ang)
    s = positions.shape[0]
    pad = LANES - ROPE_DIM
    cos_t = jnp.concatenate([cos, cos, jnp.ones((s, pad), F32)], axis=1)
    sa = jnp.concatenate([-sin, jnp.zeros((s, LANES - half), F32)], axis=1)
    sb = jnp.concatenate([jnp.zeros((s, half), F32), sin, jnp.zeros((s, pad), F32)], axis=1)
    return cos_t, sa, sb


_NT = (((1,), (1,)), ((), ()))


def _flash_softmax(s, m_ref, l_ref, rows):
    tk = s.shape[1]
    sc = [s[:, c * LANES:(c + 1) * LANES] for c in range(tk // LANES)]
    m_prev = m_ref[rows, :]
    m_new = jnp.maximum(m_prev, jnp.max(functools.reduce(jnp.maximum, sc), axis=-1, keepdims=True))
    alpha = jnp.exp2(m_prev - m_new)
    pc = [jnp.exp2(c - m_new) for c in sc]
    l_ref[rows, :] = alpha * l_ref[rows, :] + functools.reduce(jnp.add, pc)
    m_ref[rows, :] = m_new
    return jnp.concatenate([c.astype(BF16) for c in pc], axis=1), alpha


def _flash_accumulate(p, alpha, v, acc_ref, rows):
    pv = jnp.dot(p, v, preferred_element_type=F32)
    a = jnp.concatenate([alpha] * (pv.shape[1] // LANES), axis=1) if pv.shape[1] > LANES else alpha
    acc_ref[rows, :] = a * acc_ref[rows, :] + pv


def _flash_block(chains, v):
    pa = [_flash_softmax(s, m_ref, l_ref, rows) for s, m_ref, l_ref, _, rows in chains]
    for (p, alpha), (_, _, _, acc_ref, rows) in zip(pa, chains):
        _flash_accumulate(p, alpha, v, acc_ref, rows)


def _flash_init(m_ref, l_ref, acc_ref):
    m_ref[...] = jnp.full(m_ref.shape, -jnp.inf, F32)
    l_ref[...] = jnp.zeros(l_ref.shape, F32)
    acc_ref[...] = jnp.zeros(acc_ref.shape, F32)


def _flash_result(l_ref, acc_ref):
    return acc_ref[...] / jnp.sum(l_ref[...], axis=-1, keepdims=True)


def _causal_mask(tr, tk, delta):
    r = lax.broadcasted_iota(jnp.int32, (tr, tk), 0)
    c = lax.broadcasted_iota(jnp.int32, (tr, tk), 1)
    return c <= r + delta


def _diag_plan(tq, tk, tr):
    plan = []
    for co in range(0, tq, tk):
        subs = []
        for r0 in range(0, tq, tr):
            if r0 + tr - 1 < co:
                continue
            subs.append((r0, None if co + tk - 1 <= r0 else r0 - co))
        plan.append(subs)
    return plan


def _causal_sweep(qi, tq, tk, tr, block):
    full = [(r0, None) for r0 in range(0, tq, tr)]
    per_q = tq // tk

    def body(j, carry):
        block(j, full)
        return carry

    lax.fori_loop(0, qi * per_q, body, 0)
    for b, subs in enumerate(_diag_plan(tq, tk, tr)):
        block(qi * per_q + b, subs)


def _fox_kernel(q_ref, k_ref, v_ref, b_ref, o_ref, q_scr, m_ref, l_ref, acc_ref, *, tq, tk, tr, q_scale):
    q_scr[...] = (q_ref[...] * q_scale).astype(BF16)
    _flash_init(m_ref, l_ref, acc_ref)

    def block(j, subs):
        off = pl.multiple_of(j * tk, tk)
        k = k_ref[pl.ds(off, tk), :]
        v = v_ref[pl.ds(off, tk), :]
        bias = b_ref[:, pl.ds(off, tk)] * LOG2E
        chains = []
        for r0, delta in subs:
            rows = pl.ds(r0, tr)
            s = lax.dot_general(q_scr[rows, :], k, _NT, preferred_element_type=F32) + bias
            if delta is not None:
                s = jnp.where(_causal_mask(tr, tk, delta), s, -jnp.inf)
            chains.append((s, m_ref, l_ref, acc_ref, rows))
        _flash_block(chains, v)

    _causal_sweep(pl.program_id(1), tq, tk, tr, block)
    o_ref[...] = _flash_result(l_ref, acc_ref).astype(o_ref.dtype)


def _att_tiles(s, tq_max):
    tq = min(tq_max, s)
    tk = min(ATT_TK, tq)
    tr = min(ATT_SUB, tk)
    assert s % tq == 0 and tq % tk == 0 and tk % tr == 0
    return tq, tk, tr


def fox_attention(q, kv, negcum, n_heads, out_width):
    s = q.shape[0]
    tq, tk, tr = _att_tiles(s, FOX_TQ)
    hd = HEAD_DIM
    nbytes = 2 * (2 * s * hd * 2 + tq * hd * 4 + tq * hd * 2 + s * 4) + 12 * tq * tk * 4
    return pl.pallas_call(
        functools.partial(_fox_kernel, tq=tq, tk=tk, tr=tr, q_scale=hd ** -0.5 * LOG2E),
        grid=(n_heads, s // tq),
        in_specs=[pl.BlockSpec((tq, hd), lambda h, i: (i, h)),
                  pl.BlockSpec((s, hd), lambda h, i: (0, h)),
                  pl.BlockSpec((s, hd), lambda h, i: (0, n_heads + h)),
                  pl.BlockSpec((None, 1, s), lambda h, i: (h, 0, 0))],
        out_specs=pl.BlockSpec((tq, hd), lambda h, i: (i, h)),
        out_shape=jax.ShapeDtypeStruct((s, out_width), BF16),
        scratch_shapes=[pltpu.VMEM((tq, hd), BF16), pltpu.VMEM((tq, LANES), F32),
                        pltpu.VMEM((tq, LANES), F32), pltpu.VMEM((tq, hd), F32)],
        compiler_params=_params(nbytes, 2),
        name="fox_attention",
    )(q, kv, kv, negcum)


def _diff_kernel(q_ref, k_ref, v_ref, lam_ref, g_ref, o_ref, m0, l0, a0, m1, l1, a1, *, tq, tk, tr, lam_init):
    hd = HEAD_DIM
    _flash_init(m0, l0, a0)
    _flash_init(m1, l1, a1)

    def block(j, subs):
        off = pl.multiple_of(j * tk, tk)
        v = v_ref[pl.ds(off, tk), :]
        chains = []
        for lo, st in ((0, (m0, l0, a0)), (hd, (m1, l1, a1))):
            k = k_ref[pl.ds(off, tk), lo:lo + hd]
            for r0, delta in subs:
                rows = pl.ds(r0, tr)
                s = lax.dot_general(q_ref[rows, lo:lo + hd], k, _NT, preferred_element_type=F32)
                if delta is not None:
                    s = jnp.where(_causal_mask(tr, tk, delta), s, -jnp.inf)
                chains.append((s,) + st + (rows,))
        _flash_block(chains, v)

    _causal_sweep(pl.program_id(1), tq, tk, tr, block)
    lv = lam_ref[...]
    lam = (jnp.exp(jnp.sum(lv[0:1] * lv[1:2], axis=-1, keepdims=True))
           - jnp.exp(jnp.sum(lv[2:3] * lv[3:4], axis=-1, keepdims=True)) + lam_init)
    o = _flash_result(l0, a0) - lam * _flash_result(l1, a1)
    o = o * lax.rsqrt(jnp.mean(o * o, axis=-1, keepdims=True) + NORM_EPS) * g_ref[...]
    o_ref[...] = (o * (1.0 - lam_init)).astype(o_ref.dtype)


def diff_attention(qk, v, lam_vecs, subln_g, lam_init, n_heads, out_width):
    s = qk.shape[0]
    tq, tk, tr = _att_tiles(s, DIFF_TQ)
    t = tq
    w = 2 * HEAD_DIM
    nbytes = 2 * (2 * s * w * 2 + 2 * tq * w * 2) + 24 * tq * tk * 4
    stats = lambda: [pltpu.VMEM((t, LANES), F32), pltpu.VMEM((t, LANES), F32), pltpu.VMEM((t, w), F32)]
    return pl.pallas_call(
        functools.partial(_diff_kernel, tq=tq, tk=tk, tr=tr, lam_init=lam_init),
        grid=(n_heads, s // t),
        in_specs=[pl.BlockSpec((t, w), lambda h, i: (i, h)),
                  pl.BlockSpec((s, w), lambda h, i: (0, n_heads + h)),
                  pl.BlockSpec((s, w), lambda h, i: (0, h)),
                  pl.BlockSpec((4, HEAD_DIM), lambda h, i: (0, 0)),
                  pl.BlockSpec((1, w), lambda h, i: (0, 0))],
        out_specs=pl.BlockSpec((t, w), lambda h, i: (i, h)),
        out_shape=jax.ShapeDtypeStruct((s, out_width), BF16),
        scratch_shapes=stats() + stats(),
        compiler_params=_params(nbytes, 2),
        name="diff_attention",
    )(qk, qk, v, lam_vecs.astype(F32), subln_g.reshape(1, w).astype(F32))


def _mem_attn_kernel(q_ref, k_ref, v_ref, mix_ref, o_ref, *, scale):
    del mix_ref
    s = lax.dot_general(q_ref[...], k_ref[...], _NT, preferred_element_type=F32) * scale
    p = jnp.exp(s - jnp.max(s, axis=-1, keepdims=True))
    l = jnp.sum(p, axis=-1, keepdims=True)
    o = jnp.dot(p.astype(BF16), v_ref[...], preferred_element_type=F32)
    o_ref[...] = (o / l).astype(o_ref.dtype)


def mem_attention(qarr, q_col0, mem_kv, mix, out_col0):
    s = qarr.shape[0]
    mem_width = mem_kv.shape[1] // 2
    hd = mem_width // MEM_HEADS
    mt = mem_kv.shape[0]
    tq = min(MEM_TQ, s)
    q_off = q_col0 // hd
    o_off = out_col0 // hd
    assert q_col0 % hd == 0 and out_col0 % hd == 0
    return pl.pallas_call(
        functools.partial(_mem_attn_kernel, scale=hd ** -0.5),
        grid=(s // tq, MEM_HEADS),
        in_specs=[pl.BlockSpec((tq, hd), lambda i, h: (i, q_off + h)),
                  pl.BlockSpec((mt, hd), lambda i, h: (0, h)),
                  pl.BlockSpec((mt, hd), lambda i, h: (0, MEM_HEADS + h)),
                  pl.BlockSpec(memory_space=pl.ANY)],
        out_specs=pl.BlockSpec((tq, hd), lambda i, h: (i, o_off + h)),
        out_shape=jax.ShapeDtypeStruct(mix.shape, mix.dtype),
        input_output_aliases={3: 0},
        compiler_params=_params(32 * tq * hd * 4, 2),
        name="mem_attention",
    )(qarr, mem_kv, mem_kv, mix)


def _fox_cum_kernel(x_ref, b_ref, o_ref, carry, *, cb):
    @pl.when(pl.program_id(0) == 0)
    def _():
        carry[...] = jnp.zeros_like(carry)

    x = x_ref[...] + b_ref[...]
    log_f = jnp.minimum(x, 0.0) - jnp.log1p(jnp.exp(-jnp.abs(x)))
    cs = _cumsum_rows(_tri_lower(cb), log_f) + carry[...]
    o_ref[...] = cs
    carry[...] = cs[cb - 1:cb, :]


def fox_cum(logits, bias):
    s = logits.shape[0]
    cb = min(CUM_BLK, s)
    return pl.pallas_call(
        functools.partial(_fox_cum_kernel, cb=cb),
        grid=(s // cb,),
        in_specs=[pl.BlockSpec((cb, LANES), lambda i: (i, 0)),
                  pl.BlockSpec((1, LANES), lambda i: (0, 0))],
        out_specs=pl.BlockSpec((cb, LANES), lambda i: (i, 0)),
        out_shape=jax.ShapeDtypeStruct((s, LANES), F32),
        scratch_shapes=[pltpu.VMEM((1, LANES), F32)],
        compiler_params=_params(16 << 20, 1),
        name="fox_cum",
    )(logits, bias)


def _ssd_kernel(z_ref, xs_ref, b_ref, c_ref, dte_ref, dtr_ref, bias_e_ref, alog_e_ref, dskip_e_ref,
                bias_r_ref, alog_r_ref, ng_ref, o_ref, h_ref, *, heads, hdim, chunk):
    @pl.when(pl.program_id(1) == 0)
    def _():
        h_ref[...] = jnp.zeros_like(h_ref)

    L = chunk
    width = heads * hdim
    xs = xs_ref[...]
    bm = b_ref[...]
    cm = c_ref[...]
    tri = _tri_lower(L)

    dt_e = _softplus(dte_ref[...] + bias_e_ref[...])
    a_e = dt_e * (-jnp.exp(alog_e_ref[...]))
    acs_e = _cumsum_rows(tri, a_e)
    acs_last = acs_e[L - 1:L, :]
    dt_r = _softplus(dtr_ref[...] + bias_r_ref[...])
    a_r = dt_r * (-jnp.exp(alog_r_ref[...]))
    acs_r = _cumsum_lanes(a_r, tri.T)

    xd = xs * dt_e
    xd_b = xd.astype(BF16)
    cb = lax.dot_general(cm.astype(BF16), bm.astype(BF16), _NT, preferred_element_type=F32)
    causal = _causal_mask(L, L, 0)
    head_id = lax.broadcasted_iota(jnp.int32, (L, width), 1) // hdim

    y = jnp.zeros((L, width), F32)
    for r in range(heads):
        seg = acs_e[:, r * hdim:r * hdim + 1] - acs_r[r:r + 1, :]
        decay = jnp.exp(jnp.where(causal, seg, -jnp.inf))
        yr = jnp.dot((cb * decay).astype(BF16), xd_b, preferred_element_type=F32)
        y = jnp.where(head_id == r, yr, y)

    h_in = h_ref[...]
    y_off = jnp.dot(cm.astype(BF16), h_in.astype(BF16), preferred_element_type=F32) * jnp.exp(acs_e)
    to_end = jnp.exp(acs_last - acs_e)
    states = jnp.dot(bm.T.astype(BF16), (xd * to_end).astype(BF16), preferred_element_type=F32)
    h_ref[...] = h_in * jnp.exp(acs_last) + states

    y = y + y_off + xs * dskip_e_ref[...]
    g = y * _silu(z_ref[...])
    g = g * lax.rsqrt(jnp.mean(g * g, axis=-1, keepdims=True) + NORM_EPS)
    o_ref[...] = (g * ng_ref[...]).astype(o_ref.dtype)


def ssd_scan(zx, xbc, dt_e, dt_r, bias_e, alog_e, dskip_e, bias_r, alog_r, norm_g, inner, out_width):
    s = zx.shape[0]
    L = SSD_CHUNK
    G = SSD_GROUPS
    width = inner // G
    heads = width // SSD_HEAD_DIM
    N = SSD_STATE
    b_off = inner // N
    c_off = b_off + G
    row = lambda g, c: (0, g)
    return pl.pallas_call(
        functools.partial(_ssd_kernel, heads=heads, hdim=SSD_HEAD_DIM, chunk=L),
        grid=(G, s // L),
        in_specs=[pl.BlockSpec((L, width), lambda g, c: (c, g)),
                  pl.BlockSpec((L, width), lambda g, c: (c, g)),
                  pl.BlockSpec((L, N), lambda g, c: (c, b_off + g)),
                  pl.BlockSpec((L, N), lambda g, c: (c, c_off + g)),
                  pl.BlockSpec((L, width), lambda g, c: (c, g)),
                  pl.BlockSpec((None, SUBLANES, L), lambda g, c: (g, 0, c)),
                  pl.BlockSpec((1, width), row),
                  pl.BlockSpec((1, width), row),
                  pl.BlockSpec((1, width), row),
                  pl.BlockSpec((None, SUBLANES, 1), lambda g, c: (g, 0, 0)),
                  pl.BlockSpec((None, SUBLANES, 1), lambda g, c: (g, 0, 0)),
                  pl.BlockSpec((1, width), row)],
        out_specs=pl.BlockSpec((L, width), lambda g, c: (c, g)),
        out_shape=jax.ShapeDtypeStruct((s, out_width), BF16),
        scratch_shapes=[pltpu.VMEM((N, width), F32)],
        compiler_params=_params(32 << 20, 2),
        name="ssd_scan",
    )(zx, xbc, xbc, xbc, dt_e, dt_r, bias_e, alog_e, dskip_e, bias_r, alog_r, norm_g)


def _conv_rows(w, b):
    taps, c = w.shape
    return jnp.concatenate([w.astype(F32), b.reshape(1, c).astype(F32),
                            jnp.zeros((SUBLANES - taps - 1, c), F32)], axis=0)


def _pad_cols(a, n):
    return jnp.pad(a, ((0, 0), (0, n - a.shape[1])))


def kernel(x, mem, positions, norm_mix, norm_mem, w_mem_kv, w_out, norm_ffn, w_up, conv_ffn_w, conv_ffn_b,
           w_down, a_w_in, a_lambda, a_subln, b_w_in, b_forget_bias, c_w_in, c_conv_w, c_conv_b, c_dt_bias,
           c_a_log, c_d_skip, c_norm_gate, final_norm):
    batch, seq, d_model = x.shape
    depth = norm_mix.shape[0]
    mem_width = w_mem_kv.shape[2] // 2
    tok_width = d_model - mem_width
    d_ff = w_down.shape[1]
    ffp = DOWN_BK * ((d_ff + DOWN_BK - 1) // DOWN_BK)
    diff_heads = tok_width // (2 * HEAD_DIM)
    fox_heads = tok_width // HEAD_DIM
    ssd_heads = tok_width // SSD_HEAD_DIM
    heads_per_group = ssd_heads // SSD_GROUPS
    conv_ch = tok_width + 2 * SSD_GROUPS * SSD_STATE
    q_scale = HEAD_DIM ** -0.5 * LOG2E

    outs = []
    for b in range(batch):
        xb = x[b]
        mem_b = mem[b]
        cos_t, sa, sb = rope_tables(positions[b])
        for i in range(depth):
            kind, j = i % N_MIXERS, i // N_MIXERS
            h = rmsnorm(xb, norm_mix[i], BF16)
            if kind == 0:
                w = a_w_in[j].astype(BF16)
                qk = matmul(h, w, F32, 0, 2 * tok_width)
                vq = matmul(h, w, BF16, 2 * tok_width, tok_width + mem_width)
                qk = rope(qk, cos_t, sa, sb, tok_width, q_scale)
                lam_init = 0.8 - 0.6 * math.exp(-0.3 * i)
                tok = diff_attention(qk, vq, a_lambda[j], a_subln[j], lam_init, diff_heads, d_model)
                q_arr, q_col0 = vq, tok_width
            elif kind == 1:
                w = b_w_in[j]
                w_main = w[:, :3 * tok_width].astype(BF16)
                q = matmul(h, w_main, F32, 0, tok_width)
                kv = matmul(h, w_main, BF16, tok_width, 2 * tok_width)
                w_gate = _pad_cols(w[:, 3 * tok_width:3 * tok_width + fox_heads], LANES).astype(BF16)
                logits = matmul(h, w_gate, F32)
                q_arr = matmul(h, w[:, 3 * tok_width + fox_heads:].astype(BF16), BF16)
                q_col0 = 0
                bias = _pad_cols(b_forget_bias[j].reshape(1, fox_heads).astype(F32), LANES)
                cum = fox_cum(logits, bias)
                negcum = (-cum[:, :fox_heads]).T.reshape(fox_heads, 1, seq)
                tok = fox_attention(q, kv, negcum, fox_heads, d_model)
            else:
                w = c_w_in[j]
                n_zx = tok_width + conv_ch
                zx = matmul(h, w[:, :n_zx].astype(BF16), F32)
                dt_raw = matmul(h, _pad_cols(w[:, n_zx:n_zx + ssd_heads], LANES).astype(BF16), F32)[:, :ssd_heads]
                q_arr = matmul(h, w[:, n_zx + ssd_heads:].astype(BF16), BF16)
                q_col0 = 0
                xbc = dwconv_silu(zx, tok_width, conv_ch, _conv_rows(c_conv_w[j], c_conv_b[j]), SSD_CONV)
                rep = lambda a: jnp.repeat(a.astype(F32), SSD_HEAD_DIM, axis=-1)
                dt_e = rep(dt_raw)
                pad_h = SUBLANES - heads_per_group
                by_group = lambda a: jnp.pad(a.astype(F32).reshape(SSD_GROUPS, heads_per_group, -1),
                                             ((0, 0), (0, pad_h), (0, 0)))
                dt_r = by_group(dt_raw.T)
                tok = ssd_scan(zx, xbc, dt_e, dt_r,
                               rep(c_dt_bias[j].reshape(1, -1)), rep(c_a_log[j].reshape(1, -1)),
                               rep(c_d_skip[j].reshape(1, -1)),
                               by_group(c_dt_bias[j].reshape(-1, 1)), by_group(c_a_log[j].reshape(-1, 1)),
                               c_norm_gate[j].reshape(1, -1).astype(F32), tok_width, d_model)
            mem_n = rmsnorm(mem_b, norm_mem[i], BF16)
            mem_kv = matmul(mem_n, w_mem_kv[i].astype(BF16), BF16)
            mix = mem_attention(q_arr, q_col0, mem_kv, tok, tok_width)
            xb = matmul(mix, w_out[i].astype(BF16), F32, res=xb)

            hf = rmsnorm(xb, norm_ffn[i], BF16)
            hidden = ffn_up(hf, w_up[i].astype(BF16), _conv_rows(conv_ffn_w[i], conv_ffn_b[i]), d_ff, ffp)
            xb = matmul(hidden, w_down[i].astype(BF16), F32, res=xb, bk=DOWN_BK)
        outs.append(rmsnorm(xb, final_norm, x.dtype))
    return jnp.stack(outs, axis=0)
```

```python
import functools
import math

import jax
import jax.numpy as jnp
from jax import lax
from jax.experimental import pallas as pl
from jax.experimental.pallas import tpu as pltpu

F32 = jnp.float32
BF16 = jnp.bfloat16

HEAD_DIM = 128
MEM_HEADS = 4
SSD_HEAD_DIM = 64
SSD_GROUPS = 8
SSD_STATE = 128
SSD_CONV = 4
SSD_CHUNK = 128
ROPE_THETA = 500000.0
ROPE_DIM = HEAD_DIM // 4
FFN_CONV = 3
NORM_EPS = 1e-6
N_MIXERS = 3
LOG2E = math.log2(math.e)

LANES = 128
SUBLANES = 8
VMEM_LIMIT_CAP = 56 * 1024 * 1024

MM_BM = 1024
MM_BN = 1024
FFN_BN = 512
DOWN_BK = 2816
DIFF_TQ = 1024
FOX_TQ = 2048
ATT_TK = 512
ATT_SUB = 256
MEM_TQ = 512
ROW_BLK = 256
CUM_BLK = 256


def _vmem(nbytes):
    return int(min(VMEM_LIMIT_CAP, max(16 * 1024 * 1024, nbytes)))


def _params(nbytes, n_grid):
    return pltpu.CompilerParams(
        dimension_semantics=("arbitrary",) * n_grid, vmem_limit_bytes=_vmem(nbytes))


def _softplus(x):
    return jnp.maximum(x, 0.0) + jnp.log1p(jnp.exp(-jnp.abs(x)))


def _silu(x):
    return x / (1.0 + jnp.exp(-x))


def _split3(x):
    hi = x.astype(BF16)
    r = x - hi.astype(F32)
    mid = r.astype(BF16)
    lo = (r - mid.astype(F32)).astype(BF16)
    return hi, mid, lo


def _tri_lower(n):
    r = lax.broadcasted_iota(jnp.int32, (n, n), 0)
    c = lax.broadcasted_iota(jnp.int32, (n, n), 1)
    return jnp.where(c <= r, 1.0, 0.0).astype(BF16)


def _cumsum_rows(tri, x):
    hi, mid, lo = _split3(x)
    d = lambda a: jnp.dot(tri, a, preferred_element_type=F32)
    return d(hi) + d(mid) + d(lo)


def _cumsum_lanes(x, tri_t):
    hi, mid, lo = _split3(x)
    d = lambda a: jnp.dot(a, tri_t, preferred_element_type=F32)
    return d(hi) + d(mid) + d(lo)


def _rmsnorm_kernel(x_ref, g_ref, o_ref):
    x = x_ref[...].astype(F32)
    ms = jnp.mean(x * x, axis=-1, keepdims=True)
    o_ref[...] = (x * lax.rsqrt(ms + NORM_EPS) * g_ref[...]).astype(o_ref.dtype)


def rmsnorm(x, g, out_dtype):
    m, d = x.shape
    rb = min(ROW_BLK, m)
    return pl.pallas_call(
        _rmsnorm_kernel,
        grid=(m // rb,),
        in_specs=[pl.BlockSpec((rb, d), lambda i: (i, 0)),
                  pl.BlockSpec((1, d), lambda i: (0, 0))],
        out_specs=pl.BlockSpec((rb, d), lambda i: (i, 0)),
        out_shape=jax.ShapeDtypeStruct((m, d), out_dtype),
        compiler_params=_params(6 * rb * d * 4, 1),
        name="rmsnorm",
    )(x, g.reshape(1, d).astype(F32))


def _mm_kernel(*refs, nk, bk, k_true, has_res):
    if has_res:
        x_ref, w_ref, r_ref, o_ref = refs[:4]
        scratch = refs[4:]
    else:
        x_ref, w_ref, o_ref = refs[:3]
        r_ref = None
        scratch = refs[3:]
    def product(last):
        w = w_ref[...]
        if last and k_true is not None:
            row = lax.broadcasted_iota(jnp.int32, w.shape, 0)
            w = jnp.where(row < k_true - (nk - 1) * bk, w, jnp.zeros_like(w))
        return jnp.dot(x_ref[...], w, preferred_element_type=F32)

    if nk == 1:
        part = product(True)
        if has_res:
            part = part + r_ref[...]
        o_ref[...] = part.astype(o_ref.dtype)
        return
    acc_ref = scratch[0]
    k = pl.program_id(2)

    @pl.when(k == 0)
    def _():
        acc_ref[...] = product(False)

    @pl.when(jnp.logical_and(k > 0, k < nk - 1))
    def _():
        acc_ref[...] = acc_ref[...] + product(False)

    @pl.when(k == nk - 1)
    def _():
        tot = acc_ref[...] + product(True)
        if has_res:
            tot = tot + r_ref[...]
        o_ref[...] = tot.astype(o_ref.dtype)


def matmul(x, w, out_dtype, col0=0, ncols=None, res=None, bk=None):
    m, kdim = x.shape
    n = w.shape[1] - col0 if ncols is None else ncols
    bm = min(MM_BM, m)
    bn = min(MM_BN, n)
    bk = kdim if bk is None else bk
    nk = kdim // bk
    assert m % bm == 0 and n % bn == 0 and kdim % bk == 0 and col0 % bn == 0
    k_true = None if w.shape[0] == kdim else w.shape[0]
    assert w.shape[0] <= kdim and w.shape[0] > (nk - 1) * bk
    joff = col0 // bn
    in_specs = [pl.BlockSpec((bm, bk), lambda j, i, k: (i, k)),
                pl.BlockSpec((bk, bn), lambda j, i, k: (k, j + joff))]
    args = [x, w]
    if res is not None:
        in_specs.append(pl.BlockSpec((bm, bn), lambda j, i, k: (i, j)))
        args.append(res)
    osz = jnp.dtype(out_dtype).itemsize
    nbytes = 2 * (bm * bk * 2 + bk * bn * 2 + bm * bn * osz) + 2 * bm * bn * 4
    if res is not None:
        nbytes += 2 * bm * bn * 4
    scratch = []
    if nk > 1:
        scratch.append(pltpu.VMEM((bm, bn), F32))
        nbytes += bm * bn * 4
    return pl.pallas_call(
        functools.partial(_mm_kernel, nk=nk, bk=bk, k_true=k_true, has_res=res is not None),
        grid=(n // bn, m // bm, nk),
        in_specs=in_specs,
        out_specs=pl.BlockSpec((bm, bn), lambda j, i, k: (i, j)),
        out_shape=jax.ShapeDtypeStruct((m, n), out_dtype),
        scratch_shapes=scratch,
        compiler_params=_params(nbytes + (4 << 20), 3),
        name="matmul",
    )(*args)


def _ffn_up_kernel(x_ref, wg_ref, wv0_ref, wv1_ref, cg_ref, cv0_ref, cv1_ref, o_ref, tail_g, tail_v,
                   *, bm, bn, d_ff):
    @pl.when(pl.program_id(1) == 0)
    def _():
        tail_g[...] = jnp.zeros_like(tail_g)
        tail_v[...] = jnp.zeros_like(tail_v)

    x = x_ref[...]

    def conv(u, c, tail):
        ext = jnp.concatenate([tail[...], u], axis=0)
        u1 = pltpu.roll(ext, 1, 0)[SUBLANES:]
        u2 = pltpu.roll(ext, 2, 0)[SUBLANES:]
        tail[...] = u[bm - SUBLANES:]
        return c[3:4] + c[2:3] * u + c[1:2] * u1 + c[0:1] * u2

    ug = jnp.dot(x, wg_ref[...], preferred_element_type=F32)
    uv = jnp.concatenate([jnp.dot(x, wv0_ref[...], preferred_element_type=F32),
                          jnp.dot(x, wv1_ref[...], preferred_element_type=F32)], axis=1)
    g = conv(ug, cg_ref[...], tail_g)
    v = conv(uv, jnp.concatenate([cv0_ref[...], cv1_ref[...]], axis=1), tail_v)
    col = lax.broadcasted_iota(jnp.int32, (bm, bn), 1) + pl.program_id(0) * bn
    o_ref[...] = jnp.where(col < d_ff, _silu(g) * v, 0.0).astype(o_ref.dtype)


def ffn_up(h, w_up, c8, d_ff, ffp):
    m, d = h.shape
    bm = min(MM_BM, m)
    bn = FFN_BN
    hb = bn // 2
    assert ffp % bn == 0 and d_ff % hb == 0
    v0 = d_ff // hb
    vlast = (2 * d_ff) // hb - 1
    vmap0 = lambda j, i: (0, jnp.minimum(v0 + 2 * j, vlast))
    vmap1 = lambda j, i: (0, jnp.minimum(v0 + 2 * j + 1, vlast))
    nbytes = 2 * (bm * d * 2 + 2 * d * bn * 2 + bm * bn * 2) + 10 * bm * bn * 4
    return pl.pallas_call(
        functools.partial(_ffn_up_kernel, bm=bm, bn=bn, d_ff=d_ff),
        grid=(ffp // bn, m // bm),
        in_specs=[pl.BlockSpec((bm, d), lambda j, i: (i, 0)),
                  pl.BlockSpec((d, bn), lambda j, i: (0, j)),
                  pl.BlockSpec((d, hb), vmap0),
                  pl.BlockSpec((d, hb), vmap1),
                  pl.BlockSpec((SUBLANES, bn), lambda j, i: (0, j)),
                  pl.BlockSpec((SUBLANES, hb), vmap0),
                  pl.BlockSpec((SUBLANES, hb), vmap1)],
        out_specs=pl.BlockSpec((bm, bn), lambda j, i: (i, j)),
        out_shape=jax.ShapeDtypeStruct((m, ffp), BF16),
        scratch_shapes=[pltpu.VMEM((SUBLANES, bn), F32), pltpu.VMEM((SUBLANES, bn), F32)],
        compiler_params=_params(nbytes, 2),
        name="ffn_up",
    )(h, w_up, w_up, w_up, c8, c8, c8)


def _dwconv_silu_kernel(x_ref, w_ref, o_ref, tail, *, taps, rb):
    @pl.when(pl.program_id(1) == 0)
    def _():
        tail[...] = jnp.zeros_like(tail)

    x = x_ref[...]
    ext = jnp.concatenate([tail[...], x], axis=0)
    w = w_ref[...]
    out = w[taps:taps + 1] + w[taps - 1:taps] * x
    for k in range(taps - 1):
        d = taps - 1 - k
        out = out + pltpu.roll(ext, d, 0)[SUBLANES:] * w[k:k + 1]
    tail[...] = x[rb - SUBLANES:]
    o_ref[...] = _silu(out).astype(o_ref.dtype)


def dwconv_silu(x, col0, ncols, w8, taps):
    m = x.shape[0]
    rb = min(ROW_BLK, m)
    cb = 1024
    assert col0 % cb == 0 and ncols % cb == 0
    off = col0 // cb
    return pl.pallas_call(
        functools.partial(_dwconv_silu_kernel, taps=taps, rb=rb),
        grid=(ncols // cb, m // rb),
        in_specs=[pl.BlockSpec((rb, cb), lambda j, i: (i, j + off)),
                  pl.BlockSpec((SUBLANES, cb), lambda j, i: (0, j))],
        out_specs=pl.BlockSpec((rb, cb), lambda j, i: (i, j)),
        out_shape=jax.ShapeDtypeStruct((m, ncols), F32),
        scratch_shapes=[pltpu.VMEM((SUBLANES, cb), F32)],
        compiler_params=_params(16 * rb * cb * 4, 2),
        name="dwconv_silu",
    )(x, w8)


def _mm_rope_kernel(x_ref, w_ref, cos_ref, sa_ref, sb_ref, o_ref, *, groups, q_blocks, q_scale):
    acc = jnp.dot(x_ref[...], w_ref[...], preferred_element_type=F32)
    cos = cos_ref[...]
    sa = sa_ref[...]
    sb = sb_ref[...]
    half = ROPE_DIM // 2
    scale = jnp.where(pl.program_id(0) < q_blocks, q_scale, 1.0)
    for gi in range(groups):
        t = acc[:, gi * LANES:(gi + 1) * LANES]
        up = pltpu.roll(t, LANES - half, 1)
        dn = pltpu.roll(t, half, 1)
        o_ref[:, gi * LANES:(gi + 1) * LANES] = ((t * cos + up * sa + dn * sb) * scale).astype(o_ref.dtype)


def matmul_rope(x, w, ncols, cos, sa, sb, q_cols, q_scale):
    m, kdim = x.shape
    bm = min(MM_BM, m)
    bn = MM_BN
    assert m % bm == 0 and ncols % bn == 0 and q_cols % bn == 0
    tab = pl.BlockSpec((bm, LANES), lambda j, i: (i, 0))
    nbytes = 2 * (bm * kdim * 2 + kdim * bn * 2 + bm * bn * 2 + 3 * bm * LANES * 4) + 3 * bm * bn * 4
    return pl.pallas_call(
        functools.partial(_mm_rope_kernel, groups=bn // LANES, q_blocks=q_cols // bn, q_scale=q_scale),
        grid=(ncols // bn, m // bm),
        in_specs=[pl.BlockSpec((bm, kdim), lambda j, i: (i, 0)),
                  pl.BlockSpec((kdim, bn), lambda j, i: (0, j)),
                  tab, tab, tab],
        out_specs=pl.BlockSpec((bm, bn), lambda j, i: (i, j)),
        out_shape=jax.ShapeDtypeStruct((m, ncols), BF16),
        compiler_params=_params(nbytes + (4 << 20), 2),
        name="matmul_rope",
    )(x, w, cos, sa, sb)


def rope_tables(positions):
    half = ROPE_DIM // 2
    inv_freq = jnp.power(jnp.float32(ROPE_THETA), -jnp.arange(half, dtype=F32) * (2.0 / ROPE_DIM))
    ang = positions.astype(F32)[:, None] * inv_freq
    cos, sin = jnp.cos(ang), jnp.sin(ang)
    s = positions.shape[0]
    pad = LANES - ROPE_DIM
    cos_t = jnp.concatenate([cos, cos, jnp.ones((s, pad), F32)], axis=1)
    sa = jnp.concatenate([-sin, jnp.zeros((s, LANES - half), F32)], axis=1)
    sb = jnp.concatenate([jnp.zeros((s, half), F32), sin, jnp.zeros((s, pad), F32)], axis=1)
    return cos_t, sa, sb


_NT = (((1,), (1,)), ((), ()))


def _flash_softmax(s, m_ref, l_ref, rows):
    tk = s.shape[1]
    sc = [s[:, c * LANES:(c + 1) * LANES] for c in range(tk // LANES)]
    m_prev = m_ref[rows, :]
    m_new = jnp.maximum(m_prev, jnp.max(functools.reduce(jnp.maximum, sc), axis=-1, keepdims=True))
    alpha = jnp.exp2(m_prev - m_new)
    pc = [jnp.exp2(c - m_new) for c in sc]
    l_ref[rows, :] = alpha * l_ref[rows, :] + functools.reduce(jnp.add, pc)
    m_ref[rows, :] = m_new
    return jnp.concatenate([c.astype(BF16) for c in pc], axis=1), alpha


def _flash_accumulate(p, alpha, v, acc_ref, rows):
    pv = jnp.dot(p, v, preferred_element_type=F32)
    a = jnp.concatenate([alpha] * (pv.shape[1] // LANES), axis=1) if pv.shape[1] > LANES else alpha
    acc_ref[rows, :] = a * acc_ref[rows, :] + pv


def _flash_block(chains, v):
    pa = [_flash_softmax(s, m_ref, l_ref, rows) for s, m_ref, l_ref, _, rows in chains]
    for (p, alpha), (_, _, _, acc_ref, rows) in zip(pa, chains):
        _flash_accumulate(p, alpha, v, acc_ref, rows)


def _flash_init(m_ref, l_ref, acc_ref):
    m_ref[...] = jnp.full(m_ref.shape, -jnp.inf, F32)
    l_ref[...] = jnp.zeros(l_ref.shape, F32)
    acc_ref[...] = jnp.zeros(acc_ref.shape, F32)


def _flash_result(l_ref, acc_ref):
    return acc_ref[...] / jnp.sum(l_ref[...], axis=-1, keepdims=True)


def _causal_mask(tr, tk, delta):
    r = lax.broadcasted_iota(jnp.int32, (tr, tk), 0)
    c = lax.broadcasted_iota(jnp.int32, (tr, tk), 1)
    return c <= r + delta


def _diag_plan(tq, tk, tr):
    plan = []
    for co in range(0, tq, tk):
        subs = []
        for r0 in range(0, tq, tr):
            if r0 + tr - 1 < co:
                continue
            subs.append((r0, None if co + tk - 1 <= r0 else r0 - co))
        plan.append(subs)
    return plan


def _causal_sweep(qi, tq, tk, tr, block):
    full = [(r0, None) for r0 in range(0, tq, tr)]
    per_q = tq // tk

    def body(j, carry):
        block(j, full)
        return carry

    lax.fori_loop(0, qi * per_q, body, 0)
    for b, subs in enumerate(_diag_plan(tq, tk, tr)):
        block(qi * per_q + b, subs)


def _fox_kernel(q_ref, k_ref, v_ref, b_ref, o_ref, q_scr, m_ref, l_ref, acc_ref, *, tq, tk, tr, q_scale):
    q_scr[...] = (q_ref[...] * q_scale).astype(BF16)
    _flash_init(m_ref, l_ref, acc_ref)

    def block(j, subs):
        off = pl.multiple_of(j * tk, tk)
        k = k_ref[pl.ds(off, tk), :]
        v = v_ref[pl.ds(off, tk), :]
        bias = b_ref[:, pl.ds(off, tk)] * LOG2E
        chains = []
        for r0, delta in subs:
            rows = pl.ds(r0, tr)
            s = lax.dot_general(q_scr[rows, :], k, _NT, preferred_element_type=F32) + bias
            if delta is not None:
                s = jnp.where(_causal_mask(tr, tk, delta), s, -jnp.inf)
            chains.append((s, m_ref, l_ref, acc_ref, rows))
        _flash_block(chains, v)

    _causal_sweep(pl.program_id(1), tq, tk, tr, block)
    o_ref[...] = _flash_result(l_ref, acc_ref).astype(o_ref.dtype)


def _att_tiles(s, tq_max):
    tq = min(tq_max, s)
    tk = min(ATT_TK, tq)
    tr = min(ATT_SUB, tk)
    assert s % tq == 0 and tq % tk == 0 and tk % tr == 0
    return tq, tk, tr


def fox_attention(q, kv, negcum, n_heads, out_width):
    s = q.shape[0]
    tq, tk, tr = _att_tiles(s, FOX_TQ)
    hd = HEAD_DIM
    nbytes = 2 * (2 * s * hd * 2 + tq * hd * 4 + tq * hd * 2 + s * 4) + 12 * tq * tk * 4
    return pl.pallas_call(
        functools.partial(_fox_kernel, tq=tq, tk=tk, tr=tr, q_scale=hd ** -0.5 * LOG2E),
        grid=(n_heads, s // tq),
        in_specs=[pl.BlockSpec((tq, hd), lambda h, i: (i, h)),
                  pl.BlockSpec((s, hd), lambda h, i: (0, h)),
                  pl.BlockSpec((s, hd), lambda h, i: (0, n_heads + h)),
                  pl.BlockSpec((None, 1, s), lambda h, i: (h, 0, 0))],
        out_specs=pl.BlockSpec((tq, hd), lambda h, i: (i, h)),
        out_shape=jax.ShapeDtypeStruct((s, out_width), BF16),
        scratch_shapes=[pltpu.VMEM((tq, hd), BF16), pltpu.VMEM((tq, LANES), F32),
                        pltpu.VMEM((tq, LANES), F32), pltpu.VMEM((tq, hd), F32)],
        compiler_params=_params(nbytes, 2),
        name="fox_attention",
    )(q, kv, kv, negcum)


def _diff_kernel(q_ref, k_ref, v_ref, lam_ref, g_ref, o_ref, m0, l0, a0, m1, l1, a1, *, tq, tk, tr, lam_init):
    hd = HEAD_DIM
    _flash_init(m0, l0, a0)
    _flash_init(m1, l1, a1)

    def block(j, subs):
        off = pl.multiple_of(j * tk, tk)
        v = v_ref[pl.ds(off, tk), :]
        chains = []
        for lo, st in ((0, (m0, l0, a0)), (hd, (m1, l1, a1))):
            k = k_ref[pl.ds(off, tk), lo:lo + hd]
            for r0, delta in subs:
                rows = pl.ds(r0, tr)
                s = lax.dot_general(q_ref[rows, lo:lo + hd], k, _NT, preferred_element_type=F32)
                if delta is not None:
                    s = jnp.where(_causal_mask(tr, tk, delta), s, -jnp.inf)
                chains.append((s,) + st + (rows,))
        _flash_block(chains, v)

    _causal_sweep(pl.program_id(1), tq, tk, tr, block)
    lv = lam_ref[...]
    lam = (jnp.exp(jnp.sum(lv[0:1] * lv[1:2], axis=-1, keepdims=True))
           - jnp.exp(jnp.sum(lv[2:3] * lv[3:4], axis=-1, keepdims=True)) + lam_init)
    o = _flash_result(l0, a0) - lam * _flash_result(l1, a1)
    o = o * lax.rsqrt(jnp.mean(o * o, axis=-1, keepdims=True) + NORM_EPS) * g_ref[...]
    o_ref[...] = (o * (1.0 - lam_init)).astype(o_ref.dtype)


def diff_attention(qk, v, lam_vecs, subln_g, lam_init, n_heads, out_width):
    s = qk.shape[0]
    tq, tk, tr = _att_tiles(s, DIFF_TQ)
    t = tq
    w = 2 * HEAD_DIM
    nbytes = 2 * (2 * s * w * 2 + 2 * tq * w * 2) + 24 * tq * tk * 4
    stats = lambda: [pltpu.VMEM((t, LANES), F32), pltpu.VMEM((t, LANES), F32), pltpu.VMEM((t, w), F32)]
    return pl.pallas_call(
        functools.partial(_diff_kernel, tq=tq, tk=tk, tr=tr, lam_init=lam_init),
        grid=(n_heads, s // t),
        in_specs=[pl.BlockSpec((t, w), lambda h, i: (i, h)),
                  pl.BlockSpec((s, w), lambda h, i: (0, n_heads + h)),
                  pl.BlockSpec((s, w), lambda h, i: (0, h)),
                  pl.BlockSpec((4, HEAD_DIM), lambda h, i: (0, 0)),
                  pl.BlockSpec((1, w), lambda h, i: (0, 0))],
        out_specs=pl.BlockSpec((t, w), lambda h, i: (i, h)),
        out_shape=jax.ShapeDtypeStruct((s, out_width), BF16),
        scratch_shapes=stats() + stats(),
        compiler_params=_params(nbytes, 2),
        name="diff_attention",
    )(qk, qk, v, lam_vecs.astype(F32), subln_g.reshape(1, w).astype(F32))


def _mem_attn_kernel(q_ref, k_ref, v_ref, mix_ref, o_ref, *, scale):
    del mix_ref
    s = lax.dot_general(q_ref[...], k_ref[...], _NT, preferred_element_type=F32) * scale
    p = jnp.exp(s - jnp.max(s, axis=-1, keepdims=True))
    l = jnp.sum(p, axis=-1, keepdims=True)
    o = jnp.dot(p.astype(BF16), v_ref[...], preferred_element_type=F32)
    o_ref[...] = (o / l).astype(o_ref.dtype)


def mem_attention(qarr, q_col0, mem_kv, mix, out_col0):
    s = qarr.shape[0]
    mem_width = mem_kv.shape[1] // 2
    hd = mem_width // MEM_HEADS
    mt = mem_kv.shape[0]
    tq = min(MEM_TQ, s)
    q_off = q_col0 // hd
    o_off = out_col0 // hd
    assert q_col0 % hd == 0 and out_col0 % hd == 0
    return pl.pallas_call(
        functools.partial(_mem_attn_kernel, scale=hd ** -0.5),
        grid=(s // tq, MEM_HEADS),
        in_specs=[pl.BlockSpec((tq, hd), lambda i, h: (i, q_off + h)),
                  pl.BlockSpec((mt, hd), lambda i, h: (0, h)),
                  pl.BlockSpec((mt, hd), lambda i, h: (0, MEM_HEADS + h)),
                  pl.BlockSpec(memory_space=pl.ANY)],
        out_specs=pl.BlockSpec((tq, hd), lambda i, h: (i, o_off + h)),
        out_shape=jax.ShapeDtypeStruct(mix.shape, mix.dtype),
        input_output_aliases={3: 0},
        compiler_params=_params(32 * tq * hd * 4, 2),
        name="mem_attention",
    )(qarr, mem_kv, mem_kv, mix)


def _fox_cum_kernel(x_ref, b_ref, o_ref, carry, *, cb):
    @pl.when(pl.program_id(0) == 0)
    def _():
        carry[...] = jnp.zeros_like(carry)

    x = x_ref[...] + b_ref[...]
    log_f = jnp.minimum(x, 0.0) - jnp.log1p(jnp.exp(-jnp.abs(x)))
    cs = _cumsum_rows(_tri_lower(cb), log_f) + carry[...]
    o_ref[...] = cs
    carry[...] = cs[cb - 1:cb, :]


def fox_cum(logits, bias):
    s = logits.shape[0]
    cb = min(CUM_BLK, s)
    return pl.pallas_call(
        functools.partial(_fox_cum_kernel, cb=cb),
        grid=(s // cb,),
        in_specs=[pl.BlockSpec((cb, LANES), lambda i: (i, 0)),
                  pl.BlockSpec((1, LANES), lambda i: (0, 0))],
        out_specs=pl.BlockSpec((cb, LANES), lambda i: (i, 0)),
        out_shape=jax.ShapeDtypeStruct((s, LANES), F32),
        scratch_shapes=[pltpu.VMEM((1, LANES), F32)],
        compiler_params=_params(16 << 20, 1),
        name="fox_cum",
    )(logits, bias)


def _ssd_kernel(z_ref, xs_ref, b_ref, c_ref, dte_ref, dtr_ref, bias_e_ref, alog_e_ref, dskip_e_ref,
                bias_r_ref, alog_r_ref, ng_ref, o_ref, h_ref, *, heads, hdim, chunk):
    @pl.when(pl.program_id(1) == 0)
    def _():
        h_ref[...] = jnp.zeros_like(h_ref)

    L = chunk
    width = heads * hdim
    xs = xs_ref[...]
    bm = b_ref[...]
    cm = c_ref[...]
    tri = _tri_lower(L)

    dt_e = _softplus(dte_ref[...] + bias_e_ref[...])
    a_e = dt_e * (-jnp.exp(alog_e_ref[...]))
    acs_e = _cumsum_rows(tri, a_e)
    acs_last = acs_e[L - 1:L, :]
    dt_r = _softplus(dtr_ref[...] + bias_r_ref[...])
    a_r = dt_r * (-jnp.exp(alog_r_ref[...]))
    acs_r = _cumsum_lanes(a_r, tri.T)

    xd = xs * dt_e
    xd_b = xd.astype(BF16)
    cb = lax.dot_general(cm.astype(BF16), bm.astype(BF16), _NT, preferred_element_type=F32)
    causal = _causal_mask(L, L, 0)
    head_id = lax.broadcasted_iota(jnp.int32, (L, width), 1) // hdim

    y = jnp.zeros((L, width), F32)
    for r in range(heads):
        seg = acs_e[:, r * hdim:r * hdim + 1] - acs_r[r:r + 1, :]
        decay = jnp.exp(jnp.where(causal, seg, -jnp.inf))
        yr = jnp.dot((cb * decay).astype(BF16), xd_b, preferred_element_type=F32)
        y = jnp.where(head_id == r, yr, y)

    h_in = h_ref[...]
    y_off = jnp.dot(cm.astype(BF16), h_in.astype(BF16), preferred_element_type=F32) * jnp.exp(acs_e)
    to_end = jnp.exp(acs_last - acs_e)
    states = jnp.dot(bm.T.astype(BF16), (xd * to_end).astype(BF16), preferred_element_type=F32)
    h_ref[...] = h_in * jnp.exp(acs_last) + states

    y = y + y_off + xs * dskip_e_ref[...]
    g = y * _silu(z_ref[...])
    g = g * lax.rsqrt(jnp.mean(g * g, axis=-1, keepdims=True) + NORM_EPS)
    o_ref[...] = (g * ng_ref[...]).astype(o_ref.dtype)


def ssd_scan(zx, xbc, dt_e, dt_r, bias_e, alog_e, dskip_e, bias_r, alog_r, norm_g, inner, out_width):
    s = zx.shape[0]
    L = SSD_CHUNK
    G = SSD_GROUPS
    width = inner // G
    heads = width // SSD_HEAD_DIM
    N = SSD_STATE
    b_off = inner // N
    c_off = b_off + G
    row = lambda g, c: (0, g)
    return pl.pallas_call(
        functools.partial(_ssd_kernel, heads=heads, hdim=SSD_HEAD_DIM, chunk=L),
        grid=(G, s // L),
        in_specs=[pl.BlockSpec((L, width), lambda g, c: (c, g)),
                  pl.BlockSpec((L, width), lambda g, c: (c, g)),
                  pl.BlockSpec((L, N), lambda g, c: (c, b_off + g)),
                  pl.BlockSpec((L, N), lambda g, c: (c, c_off + g)),
                  pl.BlockSpec((L, width), lambda g, c: (c, g)),
                  pl.BlockSpec((None, SUBLANES, L), lambda g, c: (g, 0, c)),
                  pl.BlockSpec((1, width), row),
                  pl.BlockSpec((1, width), row),
                  pl.BlockSpec((1, width), row),
                  pl.BlockSpec((None, SUBLANES, 1), lambda g, c: (g, 0, 0)),
                  pl.BlockSpec((None, SUBLANES, 1), lambda g, c: (g, 0, 0)),
                  pl.BlockSpec((1, width), row)],
        out_specs=pl.BlockSpec((L, width), lambda g, c: (c, g)),
        out_shape=jax.ShapeDtypeStruct((s, out_width), BF16),
        scratch_shapes=[pltpu.VMEM((N, width), F32)],
        compiler_params=_params(32 << 20, 2),
        name="ssd_scan",
    )(zx, xbc, xbc, xbc, dt_e, dt_r, bias_e, alog_e, dskip_e, bias_r, alog_r, norm_g)


def _conv_rows(w, b):
    taps, c = w.shape
    return jnp.concatenate([w.astype(F32), b.reshape(1, c).astype(F32),
                            jnp.zeros((SUBLANES - taps - 1, c), F32)], axis=0)


def _pad_cols(a, n):
    return jnp.pad(a, ((0, 0), (0, n - a.shape[1])))


def kernel(x, mem, positions, norm_mix, norm_mem, w_mem_kv, w_out, norm_ffn, w_up, conv_ffn_w, conv_ffn_b,
           w_down, a_w_in, a_lambda, a_subln, b_w_in, b_forget_bias, c_w_in, c_conv_w, c_conv_b, c_dt_bias,
           c_a_log, c_d_skip, c_norm_gate, final_norm):
    batch, seq, d_model = x.shape
    depth = norm_mix.shape[0]
    mem_width = w_mem_kv.shape[2] // 2
    tok_width = d_model - mem_width
    d_ff = w_down.shape[1]
    ffp = DOWN_BK * ((d_ff + DOWN_BK - 1) // DOWN_BK)
    diff_heads = tok_width // (2 * HEAD_DIM)
    fox_heads = tok_width // HEAD_DIM
    ssd_heads = tok_width // SSD_HEAD_DIM
    heads_per_group = ssd_heads // SSD_GROUPS
    conv_ch = tok_width + 2 * SSD_GROUPS * SSD_STATE
    q_scale = HEAD_DIM ** -0.5 * LOG2E

    outs = []
    for b in range(batch):
        xb = x[b]
        mem_b = mem[b]
        cos_t, sa, sb = rope_tables(positions[b])
        for i in range(depth):
            kind, j = i % N_MIXERS, i // N_MIXERS
            h = rmsnorm(xb, norm_mix[i], BF16)
            if kind == 0:
                w = a_w_in[j].astype(BF16)
                qk = matmul_rope(h, w, 2 * tok_width, cos_t, sa, sb, tok_width, q_scale)
                vq = matmul(h, w, BF16, 2 * tok_width, tok_width + mem_width)
                lam_init = 0.8 - 0.6 * math.exp(-0.3 * i)
                tok = diff_attention(qk, vq, a_lambda[j], a_subln[j], lam_init, diff_heads, d_model)
                q_arr, q_col0 = vq, tok_width
            elif kind == 1:
                w = b_w_in[j]
                w_main = w[:, :3 * tok_width].astype(BF16)
                q = matmul(h, w_main, F32, 0, tok_width)
                kv = matmul(h, w_main, BF16, tok_width, 2 * tok_width)
                w_gate = _pad_cols(w[:, 3 * tok_width:3 * tok_width + fox_heads], LANES).astype(BF16)
                logits = matmul(h, w_gate, F32)
                q_arr = matmul(h, w[:, 3 * tok_width + fox_heads:].astype(BF16), BF16)
                q_col0 = 0
                bias = _pad_cols(b_forget_bias[j].reshape(1, fox_heads).astype(F32), LANES)
                cum = fox_cum(logits, bias)
                negcum = (-cum[:, :fox_heads]).T.reshape(fox_heads, 1, seq)
                tok = fox_attention(q, kv, negcum, fox_heads, d_model)
            else:
                w = c_w_in[j]
                n_zx = tok_width + conv_ch
                zx = matmul(h, w[:, :n_zx].astype(BF16), F32)
                dt_raw = matmul(h, _pad_cols(w[:, n_zx:n_zx + ssd_heads], LANES).astype(BF16), F32)[:, :ssd_heads]
                q_arr = matmul(h, w[:, n_zx + ssd_heads:].astype(BF16), BF16)
                q_col0 = 0
                xbc = dwconv_silu(zx, tok_width, conv_ch, _conv_rows(c_conv_w[j], c_conv_b[j]), SSD_CONV)
                rep = lambda a: jnp.repeat(a.astype(F32), SSD_HEAD_DIM, axis=-1)
                dt_e = rep(dt_raw)
                pad_h = SUBLANES - heads_per_group
                by_group = lambda a: jnp.pad(a.astype(F32).reshape(SSD_GROUPS, heads_per_group, -1),
                                             ((0, 0), (0, pad_h), (0, 0)))
                dt_r = by_group(dt_raw.T)
                tok = ssd_scan(zx, xbc, dt_e, dt_r,
                               rep(c_dt_bias[j].reshape(1, -1)), rep(c_a_log[j].reshape(1, -1)),
                               rep(c_d_skip[j].reshape(1, -1)),
                               by_group(c_dt_bias[j].reshape(-1, 1)), by_group(c_a_log[j].reshape(-1, 1)),
                               c_norm_gate[j].reshape(1, -1).astype(F32), tok_width, d_model)
            mem_n = rmsnorm(mem_b, norm_mem[i], BF16)
            mem_kv = matmul(mem_n, w_mem_kv[i].astype(BF16), BF16)
            mix = mem_attention(q_arr, q_col0, mem_kv, tok, tok_width)
            xb = matmul(mix, w_out[i].astype(BF16), F32, res=xb)

            hf = rmsnorm(xb, norm_ffn[i], BF16)
            hidden = ffn_up(hf, w_up[i].astype(BF16), _conv_rows(conv_ffn_w[i], conv_ffn_b[i]), d_ff, ffp)
            xb = matmul(hidden, w_down[i].astype(BF16), F32, res=xb, bk=DOWN_BK)
        outs.append(rmsnorm(xb, final_norm, x.dtype))
    return jnp.stack(outs, axis=0)
```

```python
import functools
import math

import jax
import jax.numpy as jnp
from jax import lax
from jax.experimental import pallas as pl
from jax.experimental.pallas import tpu as pltpu

F32 = jnp.float32
BF16 = jnp.bfloat16

HEAD_DIM = 128
MEM_HEADS = 4
SSD_HEAD_DIM = 64
SSD_GROUPS = 8
SSD_STATE = 128
SSD_CONV = 4
SSD_CHUNK = 128
ROPE_THETA = 500000.0
ROPE_DIM = HEAD_DIM // 4
FFN_CONV = 3
NORM_EPS = 1e-6
N_MIXERS = 3
LOG2E = math.log2(math.e)

LANES = 128
SUBLANES = 8
VMEM_LIMIT_CAP = 56 * 1024 * 1024

MM_BM = 1024
MM_BN = 1024
FFN_BN = 512
DOWN_BK = 2816
DIFF_TQ = 1024
FOX_TQ = 2048
ATT_TK = 512
ATT_SUB = 256
MEM_TQ = 512
ROW_BLK = 256
CUM_BLK = 256


def _vmem(nbytes):
    return int(min(VMEM_LIMIT_CAP, max(16 * 1024 * 1024, nbytes)))


def _params(nbytes, n_grid):
    return pltpu.CompilerParams(
        dimension_semantics=("arbitrary",) * n_grid, vmem_limit_bytes=_vmem(nbytes))


def _softplus(x):
    return jnp.maximum(x, 0.0) + jnp.log1p(jnp.exp(-jnp.abs(x)))


def _silu(x):
    return x / (1.0 + jnp.exp(-x))


def _split3(x):
    hi = x.astype(BF16)
    r = x - hi.astype(F32)
    mid = r.astype(BF16)
    lo = (r - mid.astype(F32)).astype(BF16)
    return hi, mid, lo


def _tri_lower(n):
    r = lax.broadcasted_iota(jnp.int32, (n, n), 0)
    c = lax.broadcasted_iota(jnp.int32, (n, n), 1)
    return jnp.where(c <= r, 1.0, 0.0).astype(BF16)


def _cumsum_rows(tri, x):
    hi, mid, lo = _split3(x)
    d = lambda a: jnp.dot(tri, a, preferred_element_type=F32)
    return d(hi) + d(mid) + d(lo)


def _cumsum_lanes(x, tri_t):
    hi, mid, lo = _split3(x)
    d = lambda a: jnp.dot(a, tri_t, preferred_element_type=F32)
    return d(hi) + d(mid) + d(lo)


def _rmsnorm_kernel(x_ref, g_ref, o_ref):
    x = x_ref[...].astype(F32)
    ms = jnp.mean(x * x, axis=-1, keepdims=True)
    o_ref[...] = (x * lax.rsqrt(ms + NORM_EPS) * g_ref[...]).astype(o_ref.dtype)


def rmsnorm(x, g, out_dtype):
    m, d = x.shape
    rb = min(ROW_BLK, m)
    return pl.pallas_call(
        _rmsnorm_kernel,
        grid=(m // rb,),
        in_specs=[pl.BlockSpec((rb, d), lambda i: (i, 0)),
                  pl.BlockSpec((1, d), lambda i: (0, 0))],
        out_specs=pl.BlockSpec((rb, d), lambda i: (i, 0)),
        out_shape=jax.ShapeDtypeStruct((m, d), out_dtype),
        compiler_params=_params(6 * rb * d * 4, 1),
        name="rmsnorm",
    )(x, g.reshape(1, d).astype(F32))


def _mm_kernel(*refs, nk, bk, k_true, has_res):
    if has_res:
        x_ref, w_ref, r_ref, o_ref = refs[:4]
        scratch = refs[4:]
    else:
        x_ref, w_ref, o_ref = refs[:3]
        r_ref = None
        scratch = refs[3:]
    def product(last):
        w = w_ref[...]
        if last and k_true is not None:
            row = lax.broadcasted_iota(jnp.int32, w.shape, 0)
            w = jnp.where(row < k_true - (nk - 1) * bk, w, jnp.zeros_like(w))
        return jnp.dot(x_ref[...], w, preferred_element_type=F32)

    if nk == 1:
        part = product(True)
        if has_res:
            part = part + r_ref[...]
        o_ref[...] = part.astype(o_ref.dtype)
        return
    acc_ref = scratch[0]
    k = pl.program_id(2)

    @pl.when(k == 0)
    def _():
        acc_ref[...] = product(False)

    @pl.when(jnp.logical_and(k > 0, k < nk - 1))
    def _():
        acc_ref[...] = acc_ref[...] + product(False)

    @pl.when(k == nk - 1)
    def _():
        tot = acc_ref[...] + product(True)
        if has_res:
            tot = tot + r_ref[...]
        o_ref[...] = tot.astype(o_ref.dtype)


def matmul(x, w, out_dtype, col0=0, ncols=None, res=None, bk=None):
    m, kdim = x.shape
    n = w.shape[1] - col0 if ncols is None else ncols
    bm = min(MM_BM, m)
    bn = min(MM_BN, n)
    bk = kdim if bk is None else bk
    nk = kdim // bk
    assert m % bm == 0 and n % bn == 0 and kdim % bk == 0 and col0 % bn == 0
    k_true = None if w.shape[0] == kdim else w.shape[0]
    assert w.shape[0] <= kdim and w.shape[0] > (nk - 1) * bk
    joff = col0 // bn
    in_specs = [pl.BlockSpec((bm, bk), lambda j, i, k: (i, k)),
                pl.BlockSpec((bk, bn), lambda j, i, k: (k, j + joff))]
    args = [x, w]
    if res is not None:
        in_specs.append(pl.BlockSpec((bm, bn), lambda j, i, k: (i, j)))
        args.append(res)
    osz = jnp.dtype(out_dtype).itemsize
    nbytes = 2 * (bm * bk * 2 + bk * bn * 2 + bm * bn * osz) + 2 * bm * bn * 4
    if res is not None:
        nbytes += 2 * bm * bn * 4
    scratch = []
    if nk > 1:
        scratch.append(pltpu.VMEM((bm, bn), F32))
        nbytes += bm * bn * 4
    return pl.pallas_call(
        functools.partial(_mm_kernel, nk=nk, bk=bk, k_true=k_true, has_res=res is not None),
        grid=(n // bn, m // bm, nk),
        in_specs=in_specs,
        out_specs=pl.BlockSpec((bm, bn), lambda j, i, k: (i, j)),
        out_shape=jax.ShapeDtypeStruct((m, n), out_dtype),
        scratch_shapes=scratch,
        compiler_params=_params(nbytes + (4 << 20), 3),
        name="matmul",
    )(*args)


def _mm_stream_kernel(*refs, layer, col0, kc, nchunks, rope):
    x_ref, w_hbm = refs[:2]
    if rope is None:
        o_ref, wbf, stage, sem = refs[2:]
    else:
        cos_ref, sa_ref, sb_ref, o_ref, wbf, stage, sem = refs[2:]
    j = pl.program_id(0)
    i = pl.program_id(1)
    nj = pl.num_programs(0)
    slot = lax.rem(j, 2)
    bn = o_ref.shape[1]

    def slab_copy(jb, c):
        rows = pl.ds(pl.multiple_of(c * kc, kc), kc)
        cols = pl.ds(pl.multiple_of(col0 + jb * bn, LANES), bn)
        return pltpu.make_async_copy(w_hbm.at[layer, rows, cols], stage, sem.at[0])

    def land(c, dst):
        wbf[dst, pl.ds(pl.multiple_of(c * kc, kc), kc), :] = stage[...].astype(BF16)

    @pl.when(jnp.logical_and(j == 0, i == 0))
    def _():
        for c in range(nchunks):
            cp = slab_copy(0, c)
            cp.start()
            cp.wait()
            land(c, 0)

    more = j + 1 < nj

    @pl.when(more)
    def _():
        slab_copy(j + 1, i).start()

    acc = jnp.dot(x_ref[...], wbf[slot], preferred_element_type=F32)
    if rope is None:
        o_ref[...] = acc.astype(o_ref.dtype)
    else:
        q_blocks, q_scale = rope
        cos = cos_ref[...]
        sa = sa_ref[...]
        sb = sb_ref[...]
        half = ROPE_DIM // 2
        scale = jnp.where(j < q_blocks, q_scale, 1.0)
        for gi in range(bn // LANES):
            t = acc[:, gi * LANES:(gi + 1) * LANES]
            up = pltpu.roll(t, LANES - half, 1)
            dn = pltpu.roll(t, half, 1)
            o_ref[:, gi * LANES:(gi + 1) * LANES] = ((t * cos + up * sa + dn * sb) * scale).astype(o_ref.dtype)

    @pl.when(more)
    def _():
        slab_copy(j + 1, i).wait()
        land(i, 1 - slot)


def matmul_stream(x, w, layer, out_dtype, col0, ncols, rope=None):
    m, kdim = x.shape
    bm = min(MM_BM, m)
    bn = MM_BN
    nm = m // bm
    assert m % bm == 0 and ncols % bn == 0 and col0 % LANES == 0 and kdim % nm == 0 and w.shape[1] == kdim
    kc = kdim // nm
    in_specs = [pl.BlockSpec((bm, kdim), lambda j, i: (i, 0)), pl.BlockSpec(memory_space=pl.ANY)]
    args = [x, w]
    rope_static = None
    if rope is not None:
        cos, sa, sb, q_cols, q_scale = rope
        assert q_cols % bn == 0
        tab = pl.BlockSpec((bm, LANES), lambda j, i: (i, 0))
        in_specs += [tab, tab, tab]
        args += [cos, sa, sb]
        rope_static = (q_cols // bn, q_scale)
    osz = jnp.dtype(out_dtype).itemsize
    nbytes = 2 * (bm * kdim * 2 + bm * bn * osz) + 2 * kdim * bn * 2 + kc * bn * 4 + 3 * bm * bn * 4
    return pl.pallas_call(
        functools.partial(_mm_stream_kernel, layer=layer, col0=col0, kc=kc, nchunks=nm, rope=rope_static),
        grid=(ncols // bn, nm),
        in_specs=in_specs,
        out_specs=pl.BlockSpec((bm, bn), lambda j, i: (i, j)),
        out_shape=jax.ShapeDtypeStruct((m, ncols), out_dtype),
        scratch_shapes=[pltpu.VMEM((2, kdim, bn), BF16), pltpu.VMEM((kc, bn), F32),
                        pltpu.SemaphoreType.DMA((1,))],
        compiler_params=_params(nbytes + (2 << 20), 2),
        name="matmul_stream",
    )(*args)


def _ffn_up_stream_kernel(x_ref, w_hbm, cg_ref, cv0_ref, cv1_ref, o_ref, tail_g, tail_v, wg_b, wv_b,
                           st_g, st_v, sem, *, layer, bm, bn, d_ff, kc, nchunks):
    j = pl.program_id(0)
    i = pl.program_id(1)
    nj = pl.num_programs(0)
    slot = lax.rem(j, 2)
    hb = bn // 2

    def slab_copies(jb, c):
        rows = pl.ds(pl.multiple_of(c * kc, kc), kc)
        g0 = pl.multiple_of(jb * bn, LANES)
        v0 = pl.multiple_of(d_ff + jb * bn, LANES)
        v1 = pl.multiple_of(jnp.minimum(d_ff + jb * bn + hb, 2 * d_ff - hb), LANES)
        return (pltpu.make_async_copy(w_hbm.at[layer, rows, pl.ds(g0, bn)], st_g, sem.at[0]),
                pltpu.make_async_copy(w_hbm.at[layer, rows, pl.ds(v0, hb)], st_v.at[:, pl.ds(0, hb)], sem.at[1]),
                pltpu.make_async_copy(w_hbm.at[layer, rows, pl.ds(v1, hb)], st_v.at[:, pl.ds(hb, hb)], sem.at[2]))

    def land(c, dst):
        rows = pl.ds(pl.multiple_of(c * kc, kc), kc)
        wg_b[dst, rows, :] = st_g[...].astype(BF16)
        wv_b[dst, rows, :] = st_v[...].astype(BF16)

    @pl.when(jnp.logical_and(j == 0, i == 0))
    def _():
        for c in range(nchunks):
            cps = slab_copies(0, c)
            for cp in cps:
                cp.start()
            for cp in cps:
                cp.wait()
            land(c, 0)

    more = j + 1 < nj

    @pl.when(more)
    def _():
        for cp in slab_copies(j + 1, i):
            cp.start()

    @pl.when(i == 0)
    def _():
        tail_g[...] = jnp.zeros_like(tail_g)
        tail_v[...] = jnp.zeros_like(tail_v)

    x = x_ref[...]

    def conv(u, c, tail):
        ext = jnp.concatenate([tail[...], u], axis=0)
        u1 = pltpu.roll(ext, 1, 0)[SUBLANES:]
        u2 = pltpu.roll(ext, 2, 0)[SUBLANES:]
        tail[...] = u[bm - SUBLANES:]
        return c[3:4] + c[2:3] * u + c[1:2] * u1 + c[0:1] * u2

    ug = jnp.dot(x, wg_b[slot], preferred_element_type=F32)
    uv = jnp.dot(x, wv_b[slot], preferred_element_type=F32)
    g = conv(ug, cg_ref[...], tail_g)
    v = conv(uv, jnp.concatenate([cv0_ref[...], cv1_ref[...]], axis=1), tail_v)
    col = lax.broadcasted_iota(jnp.int32, (bm, bn), 1) + j * bn
    o_ref[...] = jnp.where(col < d_ff, _silu(g) * v, 0.0).astype(o_ref.dtype)

    @pl.when(more)
    def _():
        for cp in slab_copies(j + 1, i):
            cp.wait()
        land(i, 1 - slot)


def ffn_up_stream(h, w_up, layer, c8, d_ff, ffp):
    m, d = h.shape
    bm = min(MM_BM, m)
    bn = FFN_BN
    hb = bn // 2
    nm = m // bm
    assert ffp % bn == 0 and d_ff % hb == 0 and d % nm == 0 and ffp <= 2 * d_ff
    kc = d // nm
    v0 = d_ff // hb
    vlast = (2 * d_ff) // hb - 1
    vmap0 = lambda j, i: (0, jnp.minimum(v0 + 2 * j, vlast))
    vmap1 = lambda j, i: (0, jnp.minimum(v0 + 2 * j + 1, vlast))
    nbytes = (2 * (bm * d * 2 + bm * bn * 2) + 2 * 2 * d * bn * 2 + 2 * kc * bn * 4 + 12 * bm * bn * 4)
    return pl.pallas_call(
        functools.partial(_ffn_up_stream_kernel, layer=layer, bm=bm, bn=bn, d_ff=d_ff, kc=kc, nchunks=nm),
        grid=(ffp // bn, nm),
        in_specs=[pl.BlockSpec((bm, d), lambda j, i: (i, 0)),
                  pl.BlockSpec(memory_space=pl.ANY),
                  pl.BlockSpec((SUBLANES, bn), lambda j, i: (0, j)),
                  pl.BlockSpec((SUBLANES, hb), vmap0),
                  pl.BlockSpec((SUBLANES, hb), vmap1)],
        out_specs=pl.BlockSpec((bm, bn), lambda j, i: (i, j)),
        out_shape=jax.ShapeDtypeStruct((m, ffp), BF16),
        scratch_shapes=[pltpu.VMEM((SUBLANES, bn), F32), pltpu.VMEM((SUBLANES, bn), F32),
                        pltpu.VMEM((2, d, bn), BF16), pltpu.VMEM((2, d, bn), BF16),
                        pltpu.VMEM((kc, bn), F32), pltpu.VMEM((kc, bn), F32),
                        pltpu.SemaphoreType.DMA((3,))],
        compiler_params=_params(nbytes, 2),
        name="ffn_up_stream",
    )(h, w_up, c8, c8, c8)


def _dwconv_silu_kernel(x_ref, w_ref, o_ref, tail, *, taps, rb):
    @pl.when(pl.program_id(1) == 0)
    def _():
        tail[...] = jnp.zeros_like(tail)

    x = x_ref[...]
    ext = jnp.concatenate([tail[...], x], axis=0)
    w = w_ref[...]
    out = w[taps:taps + 1] + w[taps - 1:taps] * x
    for k in range(taps - 1):
        d = taps - 1 - k
        out = out + pltpu.roll(ext, d, 0)[SUBLANES:] * w[k:k + 1]
    tail[...] = x[rb - SUBLANES:]
    o_ref[...] = _silu(out).astype(o_ref.dtype)


def dwconv_silu(x, col0, ncols, w8, taps):
    m = x.shape[0]
    rb = min(ROW_BLK, m)
    cb = 1024
    assert col0 % cb == 0 and ncols % cb == 0
    off = col0 // cb
    return pl.pallas_call(
        functools.partial(_dwconv_silu_kernel, taps=taps, rb=rb),
        grid=(ncols // cb, m // rb),
        in_specs=[pl.BlockSpec((rb, cb), lambda j, i: (i, j + off)),
                  pl.BlockSpec((SUBLANES, cb), lambda j, i: (0, j))],
        out_specs=pl.BlockSpec((rb, cb), lambda j, i: (i, j)),
        out_shape=jax.ShapeDtypeStruct((m, ncols), F32),
        scratch_shapes=[pltpu.VMEM((SUBLANES, cb), F32)],
        compiler_params=_params(16 * rb * cb * 4, 2),
        name="dwconv_silu",
    )(x, w8)


def rope_tables(positions):
    half = ROPE_DIM // 2
    inv_freq = jnp.power(jnp.float32(ROPE_THETA), -jnp.arange(half, dtype=F32) * (2.0 / ROPE_DIM))
    ang = positions.astype(F32)[:, None] * inv_freq
    cos, sin = jnp.cos(ang), jnp.sin(ang)
    s = positions.shape[0]
    pad = LANES - ROPE_DIM
    cos_t = jnp.concatenate([cos, cos, jnp.ones((s, pad), F32)], axis=1)
    sa = jnp.concatenate([-sin, jnp.zeros((s, LANES - half), F32)], axis=1)
    sb = jnp.concatenate([jnp.zeros((s, half), F32), sin, jnp.zeros((s, pad), F32)], axis=1)
    return cos_t, sa, sb


_NT = (((1,), (1,)), ((), ()))


def _flash_softmax(s, m_ref, l_ref, rows):
    tk = s.shape[1]
    sc = [s[:, c * LANES:(c + 1) * LANES] for c in range(tk // LANES)]
    m_prev = m_ref[rows, :]
    m_new = jnp.maximum(m_prev, jnp.max(functools.reduce(jnp.maximum, sc), axis=-1, keepdims=True))
    alpha = jnp.exp2(m_prev - m_new)
    pc = [jnp.exp2(c - m_new) for c in sc]
    l_ref[rows, :] = alpha * l_ref[rows, :] + functools.reduce(jnp.add, pc)
    m_ref[rows, :] = m_new
    return jnp.concatenate([c.astype(BF16) for c in pc], axis=1), alpha


def _flash_accumulate(p, alpha, v, acc_ref, rows):
    pv = jnp.dot(p, v, preferred_element_type=F32)
    a = jnp.concatenate([alpha] * (pv.shape[1] // LANES), axis=1) if pv.shape[1] > LANES else alpha
    acc_ref[rows, :] = a * acc_ref[rows, :] + pv


def _flash_block(chains, v):
    pa = [_flash_softmax(s, m_ref, l_ref, rows) for s, m_ref, l_ref, _, rows in chains]
    for (p, alpha), (_, _, _, acc_ref, rows) in zip(pa, chains):
        _flash_accumulate(p, alpha, v, acc_ref, rows)


def _flash_init(m_ref, l_ref, acc_ref):
    m_ref[...] = jnp.full(m_ref.shape, -jnp.inf, F32)
    l_ref[...] = jnp.zeros(l_ref.shape, F32)
    acc_ref[...] = jnp.zeros(acc_ref.shape, F32)


def _flash_result(l_ref, acc_ref):
    return acc_ref[...] / jnp.sum(l_ref[...], axis=-1, keepdims=True)


def _causal_mask(tr, tk, delta):
    r = lax.broadcasted_iota(jnp.int32, (tr, tk), 0)
    c = lax.broadcasted_iota(jnp.int32, (tr, tk), 1)
    return c <= r + delta


def _diag_plan(tq, tk, tr):
    plan = []
    for co in range(0, tq, tk):
        subs = []
        for r0 in range(0, tq, tr):
            if r0 + tr - 1 < co:
                continue
            subs.append((r0, None if co + tk - 1 <= r0 else r0 - co))
        plan.append(subs)
    return plan


def _causal_sweep(qi, tq, tk, tr, block):
    full = [(r0, None) for r0 in range(0, tq, tr)]
    per_q = tq // tk

    def body(j, carry):
        block(j, full)
        return carry

    lax.fori_loop(0, qi * per_q, body, 0)
    for b, subs in enumerate(_diag_plan(tq, tk, tr)):
        block(qi * per_q + b, subs)


def _fox_kernel(q_ref, k_ref, v_ref, b_ref, o_ref, q_scr, m_ref, l_ref, acc_ref, *, tq, tk, tr, q_scale):
    q_scr[...] = (q_ref[...] * q_scale).astype(BF16)
    _flash_init(m_ref, l_ref, acc_ref)

    def block(j, subs):
        off = pl.multiple_of(j * tk, tk)
        k = k_ref[pl.ds(off, tk), :]
        v = v_ref[pl.ds(off, tk), :]
        bias = b_ref[:, pl.ds(off, tk)] * LOG2E
        chains = []
        for r0, delta in subs:
            rows = pl.ds(r0, tr)
            s = lax.dot_general(q_scr[rows, :], k, _NT, preferred_element_type=F32) + bias
            if delta is not None:
                s = jnp.where(_causal_mask(tr, tk, delta), s, -jnp.inf)
            chains.append((s, m_ref, l_ref, acc_ref, rows))
        _flash_block(chains, v)

    _causal_sweep(pl.program_id(1), tq, tk, tr, block)
    o_ref[...] = _flash_result(l_ref, acc_ref).astype(o_ref.dtype)


def _att_tiles(s, tq_max):
    tq = min(tq_max, s)
    tk = min(ATT_TK, tq)
    tr = min(ATT_SUB, tk)
    assert s % tq == 0 and tq % tk == 0 and tk % tr == 0
    return tq, tk, tr


def fox_attention(q, kv, negcum, n_heads, out_width):
    s = q.shape[0]
    tq, tk, tr = _att_tiles(s, FOX_TQ)
    hd = HEAD_DIM
    nbytes = 2 * (2 * s * hd * 2 + tq * hd * 4 + tq * hd * 2 + s * 4) + 12 * tq * tk * 4
    return pl.pallas_call(
        functools.partial(_fox_kernel, tq=tq, tk=tk, tr=tr, q_scale=hd ** -0.5 * LOG2E),
        grid=(n_heads, s // tq),
        in_specs=[pl.BlockSpec((tq, hd), lambda h, i: (i, h)),
                  pl.BlockSpec((s, hd), lambda h, i: (0, h)),
                  pl.BlockSpec((s, hd), lambda h, i: (0, n_heads + h)),
                  pl.BlockSpec((None, 1, s), lambda h, i: (h, 0, 0))],
        out_specs=pl.BlockSpec((tq, hd), lambda h, i: (i, h)),
        out_shape=jax.ShapeDtypeStruct((s, out_width), BF16),
        scratch_shapes=[pltpu.VMEM((tq, hd), BF16), pltpu.VMEM((tq, LANES), F32),
                        pltpu.VMEM((tq, LANES), F32), pltpu.VMEM((tq, hd), F32)],
        compiler_params=_params(nbytes, 2),
        name="fox_attention",
    )(q, kv, kv, negcum)


def _diff_kernel(q_ref, k_ref, v_ref, lam_ref, g_ref, o_ref, m0, l0, a0, m1, l1, a1, *, tq, tk, tr, lam_init):
    hd = HEAD_DIM
    _flash_init(m0, l0, a0)
    _flash_init(m1, l1, a1)

    def block(j, subs):
        off = pl.multiple_of(j * tk, tk)
        v = v_ref[pl.ds(off, tk), :]
        chains = []
        for lo, st in ((0, (m0, l0, a0)), (hd, (m1, l1, a1))):
            k = k_ref[pl.ds(off, tk), lo:lo + hd]
            for r0, delta in subs:
                rows = pl.ds(r0, tr)
                s = lax.dot_general(q_ref[rows, lo:lo + hd], k, _NT, preferred_element_type=F32)
                if delta is not None:
                    s = jnp.where(_causal_mask(tr, tk, delta), s, -jnp.inf)
                chains.append((s,) + st + (rows,))
        _flash_block(chains, v)

    _causal_sweep(pl.program_id(1), tq, tk, tr, block)
    lv = lam_ref[...]
    lam = (jnp.exp(jnp.sum(lv[0:1] * lv[1:2], axis=-1, keepdims=True))
           - jnp.exp(jnp.sum(lv[2:3] * lv[3:4], axis=-1, keepdims=True)) + lam_init)
    o = _flash_result(l0, a0) - lam * _flash_result(l1, a1)
    o = o * lax.rsqrt(jnp.mean(o * o, axis=-1, keepdims=True) + NORM_EPS) * g_ref[...]
    o_ref[...] = (o * (1.0 - lam_init)).astype(o_ref.dtype)


def diff_attention(qk, v, lam_vecs, subln_g, lam_init, n_heads, out_width):
    s = qk.shape[0]
    tq, tk, tr = _att_tiles(s, DIFF_TQ)
    t = tq
    w = 2 * HEAD_DIM
    nbytes = 2 * (2 * s * w * 2 + 2 * tq * w * 2) + 24 * tq * tk * 4
    stats = lambda: [pltpu.VMEM((t, LANES), F32), pltpu.VMEM((t, LANES), F32), pltpu.VMEM((t, w), F32)]
    return pl.pallas_call(
        functools.partial(_diff_kernel, tq=tq, tk=tk, tr=tr, lam_init=lam_init),
        grid=(n_heads, s // t),
        in_specs=[pl.BlockSpec((t, w), lambda h, i: (i, h)),
                  pl.BlockSpec((s, w), lambda h, i: (0, n_heads + h)),
                  pl.BlockSpec((s, w), lambda h, i: (0, h)),
                  pl.BlockSpec((4, HEAD_DIM), lambda h, i: (0, 0)),
                  pl.BlockSpec((1, w), lambda h, i: (0, 0))],
        out_specs=pl.BlockSpec((t, w), lambda h, i: (i, h)),
        out_shape=jax.ShapeDtypeStruct((s, out_width), BF16),
        scratch_shapes=stats() + stats(),
        compiler_params=_params(nbytes, 2),
        name="diff_attention",
    )(qk, qk, v, lam_vecs.astype(F32), subln_g.reshape(1, w).astype(F32))


def _mem_attn_kernel(q_ref, k_ref, v_ref, mix_ref, o_ref, *, scale):
    del mix_ref
    s = lax.dot_general(q_ref[...], k_ref[...], _NT, preferred_element_type=F32) * scale
    p = jnp.exp(s - jnp.max(s, axis=-1, keepdims=True))
    l = jnp.sum(p, axis=-1, keepdims=True)
    o = jnp.dot(p.astype(BF16), v_ref[...], preferred_element_type=F32)
    o_ref[...] = (o / l).astype(o_ref.dtype)


def mem_attention(qarr, q_col0, mem_kv, mix, out_col0):
    s = qarr.shape[0]
    mem_width = mem_kv.shape[1] // 2
    hd = mem_width // MEM_HEADS
    mt = mem_kv.shape[0]
    tq = min(MEM_TQ, s)
    q_off = q_col0 // hd
    o_off = out_col0 // hd
    assert q_col0 % hd == 0 and out_col0 % hd == 0
    return pl.pallas_call(
        functools.partial(_mem_attn_kernel, scale=hd ** -0.5),
        grid=(s // tq, MEM_HEADS),
        in_specs=[pl.BlockSpec((tq, hd), lambda i, h: (i, q_off + h)),
                  pl.BlockSpec((mt, hd), lambda i, h: (0, h)),
                  pl.BlockSpec((mt, hd), lambda i, h: (0, MEM_HEADS + h)),
                  pl.BlockSpec(memory_space=pl.ANY)],
        out_specs=pl.BlockSpec((tq, hd), lambda i, h: (i, o_off + h)),
        out_shape=jax.ShapeDtypeStruct(mix.shape, mix.dtype),
        input_output_aliases={3: 0},
        compiler_params=_params(32 * tq * hd * 4, 2),
        name="mem_attention",
    )(qarr, mem_kv, mem_kv, mix)


def _fox_cum_kernel(x_ref, b_ref, o_ref, carry, *, cb):
    @pl.when(pl.program_id(0) == 0)
    def _():
        carry[...] = jnp.zeros_like(carry)

    x = x_ref[...] + b_ref[...]
    log_f = jnp.minimum(x, 0.0) - jnp.log1p(jnp.exp(-jnp.abs(x)))
    cs = _cumsum_rows(_tri_lower(cb), log_f) + carry[...]
    o_ref[...] = cs
    carry[...] = cs[cb - 1:cb, :]


def fox_cum(logits, bias):
    s = logits.shape[0]
    cb = min(CUM_BLK, s)
    return pl.pallas_call(
        functools.partial(_fox_cum_kernel, cb=cb),
        grid=(s // cb,),
        in_specs=[pl.BlockSpec((cb, LANES), lambda i: (i, 0)),
                  pl.BlockSpec((1, LANES), lambda i: (0, 0))],
        out_specs=pl.BlockSpec((cb, LANES), lambda i: (i, 0)),
        out_shape=jax.ShapeDtypeStruct((s, LANES), F32),
        scratch_shapes=[pltpu.VMEM((1, LANES), F32)],
        compiler_params=_params(16 << 20, 1),
        name="fox_cum",
    )(logits, bias)


def _ssd_kernel(z_ref, xs_ref, b_ref, c_ref, dte_ref, dtr_ref, bias_e_ref, alog_e_ref, dskip_e_ref,
                bias_r_ref, alog_r_ref, ng_ref, o_ref, h_ref, *, heads, hdim, chunk):
    @pl.when(pl.program_id(1) == 0)
    def _():
        h_ref[...] = jnp.zeros_like(h_ref)

    L = chunk
    width = heads * hdim
    xs = xs_ref[...]
    bm = b_ref[...]
    cm = c_ref[...]
    tri = _tri_lower(L)

    dt_e = _softplus(dte_ref[...] + bias_e_ref[...])
    a_e = dt_e * (-jnp.exp(alog_e_ref[...]))
    acs_e = _cumsum_rows(tri, a_e)
    acs_last = acs_e[L - 1:L, :]
    dt_r = _softplus(dtr_ref[...] + bias_r_ref[...])
    a_r = dt_r * (-jnp.exp(alog_r_ref[...]))
    acs_r = _cumsum_lanes(a_r, tri.T)

    xd = xs * dt_e
    xd_b = xd.astype(BF16)
    cb = lax.dot_general(cm.astype(BF16), bm.astype(BF16), _NT, preferred_element_type=F32)
    causal = _causal_mask(L, L, 0)
    head_id = lax.broadcasted_iota(jnp.int32, (L, width), 1) // hdim

    y = jnp.zeros((L, width), F32)
    for r in range(heads):
        seg = acs_e[:, r * hdim:r * hdim + 1] - acs_r[r:r + 1, :]
        decay = jnp.exp(jnp.where(causal, seg, -jnp.inf))
        yr = jnp.dot((cb * decay).astype(BF16), xd_b, preferred_element_type=F32)
        y = jnp.where(head_id == r, yr, y)

    h_in = h_ref[...]
    y_off = jnp.dot(cm.astype(BF16), h_in.astype(BF16), preferred_element_type=F32) * jnp.exp(acs_e)
    to_end = jnp.exp(acs_last - acs_e)
    states = jnp.dot(bm.T.astype(BF16), (xd * to_end).astype(BF16), preferred_element_type=F32)
    h_ref[...] = h_in * jnp.exp(acs_last) + states

    y = y + y_off + xs * dskip_e_ref[...]
    g = y * _silu(z_ref[...])
    g = g * lax.rsqrt(jnp.mean(g * g, axis=-1, keepdims=True) + NORM_EPS)
    o_ref[...] = (g * ng_ref[...]).astype(o_ref.dtype)


def ssd_scan(zx, xbc, dt_e, dt_r, bias_e, alog_e, dskip_e, bias_r, alog_r, norm_g, inner, out_width):
    s = zx.shape[0]
    L = SSD_CHUNK
    G = SSD_GROUPS
    width = inner // G
    heads = width // SSD_HEAD_DIM
    N = SSD_STATE
    b_off = inner // N
    c_off = b_off + G
    row = lambda g, c: (0, g)
    return pl.pallas_call(
        functools.partial(_ssd_kernel, heads=heads, hdim=SSD_HEAD_DIM, chunk=L),
        grid=(G, s // L),
        in_specs=[pl.BlockSpec((L, width), lambda g, c: (c, g)),
                  pl.BlockSpec((L, width), lambda g, c: (c, g)),
                  pl.BlockSpec((L, N), lambda g, c: (c, b_off + g)),
                  pl.BlockSpec((L, N), lambda g, c: (c, c_off + g)),
                  pl.BlockSpec((L, width), lambda g, c: (c, g)),
                  pl.BlockSpec((None, SUBLANES, L), lambda g, c: (g, 0, c)),
                  pl.BlockSpec((1, width), row),
                  pl.BlockSpec((1, width), row),
                  pl.BlockSpec((1, width), row),
                  pl.BlockSpec((None, SUBLANES, 1), lambda g, c: (g, 0, 0)),
                  pl.BlockSpec((None, SUBLANES, 1), lambda g, c: (g, 0, 0)),
                  pl.BlockSpec((1, width), row)],
        out_specs=pl.BlockSpec((L, width), lambda g, c: (c, g)),
        out_shape=jax.ShapeDtypeStruct((s, out_width), BF16),
        scratch_shapes=[pltpu.VMEM((N, width), F32)],
        compiler_params=_params(32 << 20, 2),
        name="ssd_scan",
    )(zx, xbc, xbc, xbc, dt_e, dt_r, bias_e, alog_e, dskip_e, bias_r, alog_r, norm_g)


def _conv_rows(w, b):
    taps, c = w.shape
    return jnp.concatenate([w.astype(F32), b.reshape(1, c).astype(F32),
                            jnp.zeros((SUBLANES - taps - 1, c), F32)], axis=0)


def _pad_cols(a, n):
    return jnp.pad(a, ((0, 0), (0, n - a.shape[1])))


def kernel(x, mem, positions, norm_mix, norm_mem, w_mem_kv, w_out, norm_ffn, w_up, conv_ffn_w, conv_ffn_b,
           w_down, a_w_in, a_lambda, a_subln, b_w_in, b_forget_bias, c_w_in, c_conv_w, c_conv_b, c_dt_bias,
           c_a_log, c_d_skip, c_norm_gate, final_norm):
    batch, seq, d_model = x.shape
    depth = norm_mix.shape[0]
    mem_width = w_mem_kv.shape[2] // 2
    tok_width = d_model - mem_width
    d_ff = w_down.shape[1]
    ffp = DOWN_BK * ((d_ff + DOWN_BK - 1) // DOWN_BK)
    diff_heads = tok_width // (2 * HEAD_DIM)
    fox_heads = tok_width // HEAD_DIM
    ssd_heads = tok_width // SSD_HEAD_DIM
    heads_per_group = ssd_heads // SSD_GROUPS
    conv_ch = tok_width + 2 * SSD_GROUPS * SSD_STATE
    q_scale = HEAD_DIM ** -0.5 * LOG2E

    outs = []
    for b in range(batch):
        xb = x[b]
        mem_b = mem[b]
        cos_t, sa, sb = rope_tables(positions[b])
        for i in range(depth):
            kind, j = i % N_MIXERS, i // N_MIXERS
            h = rmsnorm(xb, norm_mix[i], BF16)
            if kind == 0:
                qk = matmul_stream(h, a_w_in, j, BF16, 0, 2 * tok_width, rope=(cos_t, sa, sb, tok_width, q_scale))
                vq = matmul_stream(h, a_w_in, j, BF16, 2 * tok_width, tok_width + mem_width)
                lam_init = 0.8 - 0.6 * math.exp(-0.3 * i)
                tok = diff_attention(qk, vq, a_lambda[j], a_subln[j], lam_init, diff_heads, d_model)
                q_arr, q_col0 = vq, tok_width
            elif kind == 1:
                w = b_w_in[j]
                q = matmul_stream(h, b_w_in, j, F32, 0, tok_width)
                kv = matmul_stream(h, b_w_in, j, BF16, tok_width, 2 * tok_width)
                w_gate = _pad_cols(w[:, 3 * tok_width:3 * tok_width + fox_heads], LANES).astype(BF16)
                logits = matmul(h, w_gate, F32)
                q_arr = matmul(h, w[:, 3 * tok_width + fox_heads:].astype(BF16), BF16)
                q_col0 = 0
                bias = _pad_cols(b_forget_bias[j].reshape(1, fox_heads).astype(F32), LANES)
                cum = fox_cum(logits, bias)
                negcum = (-cum[:, :fox_heads]).T.reshape(fox_heads, 1, seq)
                tok = fox_attention(q, kv, negcum, fox_heads, d_model)
            else:
                w = c_w_in[j]
                n_zx = tok_width + conv_ch
                zx = matmul_stream(h, c_w_in, j, F32, 0, n_zx)
                dt_raw = matmul(h, _pad_cols(w[:, n_zx:n_zx + ssd_heads], LANES).astype(BF16), F32)[:, :ssd_heads]
                q_arr = matmul(h, w[:, n_zx + ssd_heads:].astype(BF16), BF16)
                q_col0 = 0
                xbc = dwconv_silu(zx, tok_width, conv_ch, _conv_rows(c_conv_w[j], c_conv_b[j]), SSD_CONV)
                rep = lambda a: jnp.repeat(a.astype(F32), SSD_HEAD_DIM, axis=-1)
                dt_e = rep(dt_raw)
                pad_h = SUBLANES - heads_per_group
                by_group = lambda a: jnp.pad(a.astype(F32).reshape(SSD_GROUPS, heads_per_group, -1),
                                             ((0, 0), (0, pad_h), (0, 0)))
                dt_r = by_group(dt_raw.T)
                tok = ssd_scan(zx, xbc, dt_e, dt_r,
                               rep(c_dt_bias[j].reshape(1, -1)), rep(c_a_log[j].reshape(1, -1)),
                               rep(c_d_skip[j].reshape(1, -1)),
                               by_group(c_dt_bias[j].reshape(-1, 1)), by_group(c_a_log[j].reshape(-1, 1)),
                               c_norm_gate[j].reshape(1, -1).astype(F32), tok_width, d_model)
            mem_n = rmsnorm(mem_b, norm_mem[i], BF16)
            mem_kv = matmul(mem_n, w_mem_kv[i].astype(BF16), BF16)
            mix = mem_attention(q_arr, q_col0, mem_kv, tok, tok_width)
            xb = matmul(mix, w_out[i].astype(BF16), F32, res=xb)

            hf = rmsnorm(xb, norm_ffn[i], BF16)
            hidden = ffn_up_stream(hf, w_up, i, _conv_rows(conv_ffn_w[i], conv_ffn_b[i]), d_ff, ffp)
            xb = matmul(hidden, w_down[i].astype(BF16), F32, res=xb, bk=DOWN_BK)
        outs.append(rmsnorm(xb, final_norm, x.dtype))
    return outs[0].reshape(x.shape) if batch == 1 else jnp.stack(outs, axis=0)
```

```python
import functools
import math

import jax
import jax.numpy as jnp
from jax import lax
from jax.experimental import pallas as pl
from jax.experimental.pallas import tpu as pltpu

F32 = jnp.float32
BF16 = jnp.bfloat16

HEAD_DIM = 128
MEM_HEADS = 4
SSD_HEAD_DIM = 64
SSD_GROUPS = 8
SSD_STATE = 128
SSD_CONV = 4
SSD_CHUNK = 128
ROPE_THETA = 500000.0
ROPE_DIM = HEAD_DIM // 4
FFN_CONV = 3
NORM_EPS = 1e-6
N_MIXERS = 3
LOG2E = math.log2(math.e)

LANES = 128
SUBLANES = 8
VMEM_LIMIT_CAP = 56 * 1024 * 1024

MM_BM = 1024
MM_BN = 1024
FFN_BN = 512
DOWN_BK = 2816
DIFF_TQ = 1024
FOX_TQ = 2048
ATT_TK = 512
ATT_SUB = 256
MEM_TQ = 512
ROW_BLK = 256
CUM_BLK = 256


_NT = (((1,), (1,)), ((), ()))


def _vmem(nbytes):
    return int(min(VMEM_LIMIT_CAP, max(16 * 1024 * 1024, nbytes)))


def _params(nbytes, n_grid):
    return pltpu.CompilerParams(
        dimension_semantics=("arbitrary",) * n_grid, vmem_limit_bytes=_vmem(nbytes))


def _softplus(x):
    return jnp.maximum(x, 0.0) + jnp.log1p(jnp.exp(-jnp.abs(x)))


def _silu(x):
    return x / (1.0 + jnp.exp(-x))


def _split3(x):
    hi = x.astype(BF16)
    r = x - hi.astype(F32)
    mid = r.astype(BF16)
    lo = (r - mid.astype(F32)).astype(BF16)
    return hi, mid, lo


def _tri_lower(n):
    r = lax.broadcasted_iota(jnp.int32, (n, n), 0)
    c = lax.broadcasted_iota(jnp.int32, (n, n), 1)
    return jnp.where(c <= r, 1.0, 0.0).astype(BF16)


def _cumsum_rows(tri, x):
    hi, mid, lo = _split3(x)
    d = lambda a: jnp.dot(tri, a, preferred_element_type=F32)
    return d(hi) + d(mid) + d(lo)


def _cumsum_lanes(x, tri_t):
    hi, mid, lo = _split3(x)
    d = lambda a: jnp.dot(a, tri_t, preferred_element_type=F32)
    return d(hi) + d(mid) + d(lo)


def _rmsnorm_kernel(x_ref, g_ref, o_ref):
    x = x_ref[...].astype(F32)
    ms = jnp.mean(x * x, axis=-1, keepdims=True)
    o_ref[...] = (x * lax.rsqrt(ms + NORM_EPS) * g_ref[...]).astype(o_ref.dtype)


def rmsnorm(x, g, out_dtype):
    m, d = x.shape
    rb = min(ROW_BLK, m)
    return pl.pallas_call(
        _rmsnorm_kernel,
        grid=(m // rb,),
        in_specs=[pl.BlockSpec((rb, d), lambda i: (i, 0)),
                  pl.BlockSpec((1, d), lambda i: (0, 0))],
        out_specs=pl.BlockSpec((rb, d), lambda i: (i, 0)),
        out_shape=jax.ShapeDtypeStruct((m, d), out_dtype),
        compiler_params=_params(6 * rb * d * 4, 1),
        name="rmsnorm",
    )(x, g.reshape(1, d).astype(F32))


def _mm_kernel(*refs, nk, bk, k_true, has_res):
    if has_res:
        x_ref, w_ref, r_ref, o_ref = refs[:4]
        scratch = refs[4:]
    else:
        x_ref, w_ref, o_ref = refs[:3]
        r_ref = None
        scratch = refs[3:]
    def product(last):
        w = w_ref[...]
        if last and k_true is not None:
            row = lax.broadcasted_iota(jnp.int32, w.shape, 0)
            w = jnp.where(row < k_true - (nk - 1) * bk, w, jnp.zeros_like(w))
        return jnp.dot(x_ref[...], w, preferred_element_type=F32)

    if nk == 1:
        part = product(True)
        if has_res:
            part = part + r_ref[...]
        o_ref[...] = part.astype(o_ref.dtype)
        return
    acc_ref = scratch[0]
    k = pl.program_id(2)

    @pl.when(k == 0)
    def _():
        acc_ref[...] = product(False)

    @pl.when(jnp.logical_and(k > 0, k < nk - 1))
    def _():
        acc_ref[...] = acc_ref[...] + product(False)

    @pl.when(k == nk - 1)
    def _():
        tot = acc_ref[...] + product(True)
        if has_res:
            tot = tot + r_ref[...]
        o_ref[...] = tot.astype(o_ref.dtype)


def matmul(x, w, out_dtype, col0=0, ncols=None, res=None, bk=None):
    m, kdim = x.shape
    n = w.shape[1] - col0 if ncols is None else ncols
    bm = min(MM_BM, m)
    bn = min(MM_BN, n)
    bk = kdim if bk is None else bk
    nk = kdim // bk
    assert m % bm == 0 and n % bn == 0 and kdim % bk == 0 and col0 % bn == 0
    k_true = None if w.shape[0] == kdim else w.shape[0]
    assert w.shape[0] <= kdim and w.shape[0] > (nk - 1) * bk
    joff = col0 // bn
    in_specs = [pl.BlockSpec((bm, bk), lambda j, i, k: (i, k)),
                pl.BlockSpec((bk, bn), lambda j, i, k: (k, j + joff))]
    args = [x, w]
    if res is not None:
        in_specs.append(pl.BlockSpec((bm, bn), lambda j, i, k: (i, j)))
        args.append(res)
    osz = jnp.dtype(out_dtype).itemsize
    nbytes = 2 * (bm * bk * 2 + bk * bn * 2 + bm * bn * osz) + 2 * bm * bn * 4
    if res is not None:
        nbytes += 2 * bm * bn * 4
    scratch = []
    if nk > 1:
        scratch.append(pltpu.VMEM((bm, bn), F32))
        nbytes += bm * bn * 4
    return pl.pallas_call(
        functools.partial(_mm_kernel, nk=nk, bk=bk, k_true=k_true, has_res=res is not None),
        grid=(n // bn, m // bm, nk),
        in_specs=in_specs,
        out_specs=pl.BlockSpec((bm, bn), lambda j, i, k: (i, j)),
        out_shape=jax.ShapeDtypeStruct((m, n), out_dtype),
        scratch_shapes=scratch,
        compiler_params=_params(nbytes + (4 << 20), 3),
        name="matmul",
    )(*args)


def _mm_stream_kernel(*refs, layer, col0, kc, nchunks, rope, w_t):
    x_ref, w_hbm = refs[:2]
    if rope is None:
        o_ref, wbf, stage, sem = refs[2:]
    else:
        cos_ref, sa_ref, sb_ref, o_ref, wbf, stage, sem = refs[2:]
    j = pl.program_id(0)
    i = pl.program_id(1)
    nj = pl.num_programs(0)
    slot = lax.rem(j, 2)
    bn = o_ref.shape[1]

    def slab_copy(jb, c):
        rows = pl.ds(pl.multiple_of(c * kc, kc), kc)
        cols = pl.ds(pl.multiple_of(col0 + jb * bn, LANES), bn)
        src = w_hbm.at[layer, cols, rows] if w_t else w_hbm.at[layer, rows, cols]
        return pltpu.make_async_copy(src, stage, sem.at[0])

    def land(c, dst):
        rows = pl.ds(pl.multiple_of(c * kc, kc), kc)
        if w_t:
            wbf[dst, :, rows] = stage[...].astype(BF16)
        else:
            wbf[dst, rows, :] = stage[...].astype(BF16)

    @pl.when(jnp.logical_and(j == 0, i == 0))
    def _():
        for c in range(nchunks):
            cp = slab_copy(0, c)
            cp.start()
            cp.wait()
            land(c, 0)

    more = j + 1 < nj

    @pl.when(more)
    def _():
        slab_copy(j + 1, i).start()

    if w_t:
        acc = lax.dot_general(x_ref[...], wbf[slot], _NT, preferred_element_type=F32)
    else:
        acc = jnp.dot(x_ref[...], wbf[slot], preferred_element_type=F32)
    if rope is None:
        o_ref[...] = acc.astype(o_ref.dtype)
    else:
        q_blocks, q_scale = rope
        cos = cos_ref[...]
        sa = sa_ref[...]
        sb = sb_ref[...]
        half = ROPE_DIM // 2
        scale = jnp.where(j < q_blocks, q_scale, 1.0)
        for gi in range(bn // LANES):
            t = acc[:, gi * LANES:(gi + 1) * LANES]
            up = pltpu.roll(t, LANES - half, 1)
            dn = pltpu.roll(t, half, 1)
            o_ref[:, gi * LANES:(gi + 1) * LANES] = ((t * cos + up * sa + dn * sb) * scale).astype(o_ref.dtype)

    @pl.when(more)
    def _():
        slab_copy(j + 1, i).wait()
        land(i, 1 - slot)


def matmul_stream(x, w, layer, out_dtype, col0, ncols, rope=None, w_t=False):
    m, kdim = x.shape
    bm = min(MM_BM, m)
    bn = MM_BN
    nm = m // bm
    assert m % bm == 0 and ncols % bn == 0 and col0 % LANES == 0 and kdim % nm == 0 and w.shape[2 if w_t else 1] == kdim
    kc = kdim // nm
    in_specs = [pl.BlockSpec((bm, kdim), lambda j, i: (i, 0)), pl.BlockSpec(memory_space=pl.ANY)]
    args = [x, w]
    rope_static = None
    if rope is not None:
        cos, sa, sb, q_cols, q_scale = rope
        assert q_cols % bn == 0
        tab = pl.BlockSpec((bm, LANES), lambda j, i: (i, 0))
        in_specs += [tab, tab, tab]
        args += [cos, sa, sb]
        rope_static = (q_cols // bn, q_scale)
    osz = jnp.dtype(out_dtype).itemsize
    nbytes = 2 * (bm * kdim * 2 + bm * bn * osz) + 2 * kdim * bn * 2 + kc * bn * 4 + 3 * bm * bn * 4
    return pl.pallas_call(
        functools.partial(_mm_stream_kernel, layer=layer, col0=col0, kc=kc, nchunks=nm, rope=rope_static, w_t=w_t),
        grid=(ncols // bn, nm),
        in_specs=in_specs,
        out_specs=pl.BlockSpec((bm, bn), lambda j, i: (i, j)),
        out_shape=jax.ShapeDtypeStruct((m, ncols), out_dtype),
        scratch_shapes=[pltpu.VMEM((2, bn, kdim) if w_t else (2, kdim, bn), BF16),
                        pltpu.VMEM((bn, kc) if w_t else (kc, bn), F32),
                        pltpu.SemaphoreType.DMA((1,))],
        compiler_params=_params(nbytes + (2 << 20), 2),
        name="matmul_stream",
    )(*args)


def _ffn_up_stream_kernel(x_ref, w_hbm, cg_ref, cv0_ref, cv1_ref, o_ref, tail_g, tail_v, wg_b, wv_b,
                           st_g, st_v, sem, *, layer, bm, bn, d_ff, kc, nchunks):
    j = pl.program_id(0)
    i = pl.program_id(1)
    nj = pl.num_programs(0)
    slot = lax.rem(j, 2)
    hb = bn // 2

    def slab_copies(jb, c):
        rows = pl.ds(pl.multiple_of(c * kc, kc), kc)
        g0 = pl.multiple_of(jb * bn, LANES)
        v0 = pl.multiple_of(d_ff + jb * bn, LANES)
        v1 = pl.multiple_of(jnp.minimum(d_ff + jb * bn + hb, 2 * d_ff - hb), LANES)
        return (pltpu.make_async_copy(w_hbm.at[layer, rows, pl.ds(g0, bn)], st_g, sem.at[0]),
                pltpu.make_async_copy(w_hbm.at[layer, rows, pl.ds(v0, hb)], st_v.at[:, pl.ds(0, hb)], sem.at[1]),
                pltpu.make_async_copy(w_hbm.at[layer, rows, pl.ds(v1, hb)], st_v.at[:, pl.ds(hb, hb)], sem.at[2]))

    def land(c, dst):
        rows = pl.ds(pl.multiple_of(c * kc, kc), kc)
        wg_b[dst, rows, :] = st_g[...].astype(BF16)
        wv_b[dst, rows, :] = st_v[...].astype(BF16)

    @pl.when(jnp.logical_and(j == 0, i == 0))
    def _():
        for c in range(nchunks):
            cps = slab_copies(0, c)
            for cp in cps:
                cp.start()
            for cp in cps:
                cp.wait()
            land(c, 0)

    more = j + 1 < nj

    @pl.when(more)
    def _():
        for cp in slab_copies(j + 1, i):
            cp.start()

    @pl.when(i == 0)
    def _():
        tail_g[...] = jnp.zeros_like(tail_g)
        tail_v[...] = jnp.zeros_like(tail_v)

    x = x_ref[...]

    def conv(u, c, tail):
        ext = jnp.concatenate([tail[...], u], axis=0)
        u1 = pltpu.roll(ext, 1, 0)[SUBLANES:]
        u2 = pltpu.roll(ext, 2, 0)[SUBLANES:]
        tail[...] = u[bm - SUBLANES:]
        return c[3:4] + c[2:3] * u + c[1:2] * u1 + c[0:1] * u2

    ug = jnp.dot(x, wg_b[slot], preferred_element_type=F32)
    uv = jnp.dot(x, wv_b[slot], preferred_element_type=F32)
    g = conv(ug, cg_ref[...], tail_g)
    v = conv(uv, jnp.concatenate([cv0_ref[...], cv1_ref[...]], axis=1), tail_v)
    col = lax.broadcasted_iota(jnp.int32, (bm, bn), 1) + j * bn
    o_ref[...] = jnp.where(col < d_ff, _silu(g) * v, 0.0).astype(o_ref.dtype)

    @pl.when(more)
    def _():
        for cp in slab_copies(j + 1, i):
            cp.wait()
        land(i, 1 - slot)


def ffn_up_stream(h, w_up, layer, c8, d_ff, ffp):
    m, d = h.shape
    bm = min(MM_BM, m)
    bn = FFN_BN
    hb = bn // 2
    nm = m // bm
    assert ffp % bn == 0 and d_ff % hb == 0 and d % nm == 0 and ffp <= 2 * d_ff
    kc = d // nm
    v0 = d_ff // hb
    vlast = (2 * d_ff) // hb - 1
    vmap0 = lambda j, i: (0, jnp.minimum(v0 + 2 * j, vlast))
    vmap1 = lambda j, i: (0, jnp.minimum(v0 + 2 * j + 1, vlast))
    nbytes = (2 * (bm * d * 2 + bm * bn * 2) + 2 * 2 * d * bn * 2 + 2 * kc * bn * 4 + 12 * bm * bn * 4)
    return pl.pallas_call(
        functools.partial(_ffn_up_stream_kernel, layer=layer, bm=bm, bn=bn, d_ff=d_ff, kc=kc, nchunks=nm),
        grid=(ffp // bn, nm),
        in_specs=[pl.BlockSpec((bm, d), lambda j, i: (i, 0)),
                  pl.BlockSpec(memory_space=pl.ANY),
                  pl.BlockSpec((SUBLANES, bn), lambda j, i: (0, j)),
                  pl.BlockSpec((SUBLANES, hb), vmap0),
                  pl.BlockSpec((SUBLANES, hb), vmap1)],
        out_specs=pl.BlockSpec((bm, bn), lambda j, i: (i, j)),
        out_shape=jax.ShapeDtypeStruct((m, ffp), BF16),
        scratch_shapes=[pltpu.VMEM((SUBLANES, bn), F32), pltpu.VMEM((SUBLANES, bn), F32),
                        pltpu.VMEM((2, d, bn), BF16), pltpu.VMEM((2, d, bn), BF16),
                        pltpu.VMEM((kc, bn), F32), pltpu.VMEM((kc, bn), F32),
                        pltpu.SemaphoreType.DMA((3,))],
        compiler_params=_params(nbytes, 2),
        name="ffn_up_stream",
    )(h, w_up, c8, c8, c8)


def _dwconv_silu_kernel(x_ref, w_ref, o_ref, tail, *, taps, rb):
    @pl.when(pl.program_id(1) == 0)
    def _():
        tail[...] = jnp.zeros_like(tail)

    x = x_ref[...]
    ext = jnp.concatenate([tail[...], x], axis=0)
    w = w_ref[...]
    out = w[taps:taps + 1] + w[taps - 1:taps] * x
    for k in range(taps - 1):
        d = taps - 1 - k
        out = out + pltpu.roll(ext, d, 0)[SUBLANES:] * w[k:k + 1]
    tail[...] = x[rb - SUBLANES:]
    o_ref[...] = _silu(out).astype(o_ref.dtype)


def dwconv_silu(x, col0, ncols, w8, taps):
    m = x.shape[0]
    rb = min(ROW_BLK, m)
    cb = 1024
    assert col0 % cb == 0 and ncols % cb == 0
    off = col0 // cb
    return pl.pallas_call(
        functools.partial(_dwconv_silu_kernel, taps=taps, rb=rb),
        grid=(ncols // cb, m // rb),
        in_specs=[pl.BlockSpec((rb, cb), lambda j, i: (i, j + off)),
                  pl.BlockSpec((SUBLANES, cb), lambda j, i: (0, j))],
        out_specs=pl.BlockSpec((rb, cb), lambda j, i: (i, j)),
        out_shape=jax.ShapeDtypeStruct((m, ncols), F32),
        scratch_shapes=[pltpu.VMEM((SUBLANES, cb), F32)],
        compiler_params=_params(16 * rb * cb * 4, 2),
        name="dwconv_silu",
    )(x, w8)


def rope_tables(positions):
    half = ROPE_DIM // 2
    inv_freq = jnp.power(jnp.float32(ROPE_THETA), -jnp.arange(half, dtype=F32) * (2.0 / ROPE_DIM))
    ang = positions.astype(F32)[:, None] * inv_freq
    cos, sin = jnp.cos(ang), jnp.sin(ang)
    s = positions.shape[0]
    pad = LANES - ROPE_DIM
    cos_t = jnp.concatenate([cos, cos, jnp.ones((s, pad), F32)], axis=1)
    sa = jnp.concatenate([-sin, jnp.zeros((s, LANES - half), F32)], axis=1)
    sb = jnp.concatenate([jnp.zeros((s, half), F32), sin, jnp.zeros((s, pad), F32)], axis=1)
    return cos_t, sa, sb


def _flash_softmax(s, m_ref, l_ref, rows):
    tk = s.shape[1]
    sc = [s[:, c * LANES:(c + 1) * LANES] for c in range(tk // LANES)]
    m_prev = m_ref[rows, :]
    m_new = jnp.maximum(m_prev, jnp.max(functools.reduce(jnp.maximum, sc), axis=-1, keepdims=True))
    alpha = jnp.exp2(m_prev - m_new)
    pc = [jnp.exp2(c - m_new) for c in sc]
    l_ref[rows, :] = alpha * l_ref[rows, :] + functools.reduce(jnp.add, pc)
    m_ref[rows, :] = m_new
    return jnp.concatenate([c.astype(BF16) for c in pc], axis=1), alpha


def _flash_accumulate(p, alpha, v, acc_ref, rows):
    pv = jnp.dot(p, v, preferred_element_type=F32)
    a = jnp.concatenate([alpha] * (pv.shape[1] // LANES), axis=1) if pv.shape[1] > LANES else alpha
    acc_ref[rows, :] = a * acc_ref[rows, :] + pv


def _flash_block(chains, v):
    pa = [_flash_softmax(s, m_ref, l_ref, rows) for s, m_ref, l_ref, _, rows in chains]
    for (p, alpha), (_, _, _, acc_ref, rows) in zip(pa, chains):
        _flash_accumulate(p, alpha, v, acc_ref, rows)


def _flash_init(m_ref, l_ref, acc_ref):
    m_ref[...] = jnp.full(m_ref.shape, -jnp.inf, F32)
    l_ref[...] = jnp.zeros(l_ref.shape, F32)
    acc_ref[...] = jnp.zeros(acc_ref.shape, F32)


def _flash_result(l_ref, acc_ref):
    return acc_ref[...] / jnp.sum(l_ref[...], axis=-1, keepdims=True)


def _causal_mask(tr, tk, delta):
    r = lax.broadcasted_iota(jnp.int32, (tr, tk), 0)
    c = lax.broadcasted_iota(jnp.int32, (tr, tk), 1)
    return c <= r + delta


def _diag_plan(tq, tk, tr):
    plan = []
    for co in range(0, tq, tk):
        subs = []
        for r0 in range(0, tq, tr):
            if r0 + tr - 1 < co:
                continue
            subs.append((r0, None if co + tk - 1 <= r0 else r0 - co))
        plan.append(subs)
    return plan


def _causal_sweep(qi, tq, tk, tr, block):
    full = [(r0, None) for r0 in range(0, tq, tr)]
    per_q = tq // tk

    def body(j, carry):
        block(j, full)
        return carry

    lax.fori_loop(0, qi * per_q, body, 0)
    for b, subs in enumerate(_diag_plan(tq, tk, tr)):
        block(qi * per_q + b, subs)


def _fox_kernel(q_ref, k_ref, v_ref, b_ref, mix_ref, o_ref, q_scr, m_ref, l_ref, acc_ref, *, tq, tk, tr, q_scale):
    del mix_ref
    q_scr[...] = (q_ref[...] * q_scale).astype(BF16)
    _flash_init(m_ref, l_ref, acc_ref)

    def block(j, subs):
        off = pl.multiple_of(j * tk, tk)
        k = k_ref[pl.ds(off, tk), :]
        v = v_ref[pl.ds(off, tk), :]
        bias = b_ref[:, pl.ds(off, tk)] * LOG2E
        chains = []
        for r0, delta in subs:
            rows = pl.ds(r0, tr)
            s = lax.dot_general(q_scr[rows, :], k, _NT, preferred_element_type=F32) + bias
            if delta is not None:
                s = jnp.where(_causal_mask(tr, tk, delta), s, -jnp.inf)
            chains.append((s, m_ref, l_ref, acc_ref, rows))
        _flash_block(chains, v)

    _causal_sweep(pl.program_id(1), tq, tk, tr, block)
    o_ref[...] = _flash_result(l_ref, acc_ref).astype(o_ref.dtype)


def _att_tiles(s, tq_max):
    tq = min(tq_max, s)
    tk = min(ATT_TK, tq)
    tr = min(ATT_SUB, tk)
    assert s % tq == 0 and tq % tk == 0 and tk % tr == 0
    return tq, tk, tr


def fox_attention(q, kv, negcum, n_heads, mix):
    s = q.shape[0]
    tq, tk, tr = _att_tiles(s, FOX_TQ)
    hd = HEAD_DIM
    nbytes = 2 * (2 * s * hd * 2 + tq * hd * 4 + tq * hd * 2 + s * 4) + 12 * tq * tk * 4
    return pl.pallas_call(
        functools.partial(_fox_kernel, tq=tq, tk=tk, tr=tr, q_scale=hd ** -0.5 * LOG2E),
        grid=(n_heads, s // tq),
        in_specs=[pl.BlockSpec((tq, hd), lambda h, i: (i, h)),
                  pl.BlockSpec((s, hd), lambda h, i: (0, h)),
                  pl.BlockSpec((s, hd), lambda h, i: (0, n_heads + h)),
                  pl.BlockSpec((None, 1, s), lambda h, i: (h, 0, 0)),
                  pl.BlockSpec(memory_space=pl.ANY)],
        out_specs=pl.BlockSpec((tq, hd), lambda h, i: (i, h)),
        out_shape=jax.ShapeDtypeStruct(mix.shape, mix.dtype),
        input_output_aliases={4: 0},
        scratch_shapes=[pltpu.VMEM((tq, hd), BF16), pltpu.VMEM((tq, LANES), F32),
                        pltpu.VMEM((tq, LANES), F32), pltpu.VMEM((tq, hd), F32)],
        compiler_params=_params(nbytes, 2),
        name="fox_attention",
    )(q, kv, kv, negcum, mix)


def _diff_kernel(q_ref, k_ref, v_ref, lam_ref, g_ref, mix_ref, o_ref, m0, l0, a0, m1, l1, a1,
                 *, tq, tk, tr, lam_init):
    del mix_ref
    hd = HEAD_DIM
    _flash_init(m0, l0, a0)
    _flash_init(m1, l1, a1)

    def block(j, subs):
        off = pl.multiple_of(j * tk, tk)
        v = v_ref[pl.ds(off, tk), :]
        chains = []
        for lo, st in ((0, (m0, l0, a0)), (hd, (m1, l1, a1))):
            k = k_ref[pl.ds(off, tk), lo:lo + hd]
            for r0, delta in subs:
                rows = pl.ds(r0, tr)
                s = lax.dot_general(q_ref[rows, lo:lo + hd], k, _NT, preferred_element_type=F32)
                if delta is not None:
                    s = jnp.where(_causal_mask(tr, tk, delta), s, -jnp.inf)
                chains.append((s,) + st + (rows,))
        _flash_block(chains, v)

    _causal_sweep(pl.program_id(1), tq, tk, tr, block)
    lv = lam_ref[...]
    lam = (jnp.exp(jnp.sum(lv[0:1] * lv[1:2], axis=-1, keepdims=True))
           - jnp.exp(jnp.sum(lv[2:3] * lv[3:4], axis=-1, keepdims=True)) + lam_init)
    o = _flash_result(l0, a0) - lam * _flash_result(l1, a1)
    o = o * lax.rsqrt(jnp.mean(o * o, axis=-1, keepdims=True) + NORM_EPS) * g_ref[...]
    o_ref[...] = (o * (1.0 - lam_init)).astype(o_ref.dtype)


def diff_attention(qk, v, lam_vecs, subln_g, lam_init, n_heads, mix):
    s = qk.shape[0]
    tq, tk, tr = _att_tiles(s, DIFF_TQ)
    t = tq
    w = 2 * HEAD_DIM
    nbytes = 2 * (2 * s * w * 2 + 2 * tq * w * 2) + 24 * tq * tk * 4
    stats = lambda: [pltpu.VMEM((t, LANES), F32), pltpu.VMEM((t, LANES), F32), pltpu.VMEM((t, w), F32)]
    return pl.pallas_call(
        functools.partial(_diff_kernel, tq=tq, tk=tk, tr=tr, lam_init=lam_init),
        grid=(n_heads, s // t),
        in_specs=[pl.BlockSpec((t, w), lambda h, i: (i, h)),
                  pl.BlockSpec((s, w), lambda h, i: (0, n_heads + h)),
                  pl.BlockSpec((s, w), lambda h, i: (0, h)),
                  pl.BlockSpec((4, HEAD_DIM), lambda h, i: (0, 0)),
                  pl.BlockSpec((1, w), lambda h, i: (0, 0)),
                  pl.BlockSpec(memory_space=pl.ANY)],
        out_specs=pl.BlockSpec((t, w), lambda h, i: (i, h)),
        out_shape=jax.ShapeDtypeStruct(mix.shape, mix.dtype),
        input_output_aliases={5: 0},
        scratch_shapes=stats() + stats(),
        compiler_params=_params(nbytes, 2),
        name="diff_attention",
    )(qk, qk, v, lam_vecs.astype(F32), subln_g.reshape(1, w).astype(F32), mix)


def _mem_attn_kernel(q_ref, kv_ref, o_ref, *, scale, heads, hd, ctx_col0):
    o_ref[:, :ctx_col0] = jnp.zeros((o_ref.shape[0], ctx_col0), o_ref.dtype)
    for h in range(heads):
        k = kv_ref[:, h * hd:(h + 1) * hd]
        v = kv_ref[:, (heads + h) * hd:(heads + h + 1) * hd]
        s = lax.dot_general(q_ref[:, h * hd:(h + 1) * hd], k, _NT, preferred_element_type=F32) * scale
        p = jnp.exp(s - jnp.max(s, axis=-1, keepdims=True))
        l = jnp.sum(p, axis=-1, keepdims=True)
        o = jnp.dot(p.astype(BF16), v, preferred_element_type=F32)
        o_ref[:, ctx_col0 + h * hd:ctx_col0 + (h + 1) * hd] = (o / l).astype(o_ref.dtype)


def mem_attention(qarr, q_col0, mem_kv, out_width, ctx_col0):
    s = qarr.shape[0]
    mem_width = mem_kv.shape[1] // 2
    hd = mem_width // MEM_HEADS
    mt = mem_kv.shape[0]
    tq = min(MEM_TQ, s)
    assert q_col0 % mem_width == 0 and ctx_col0 + mem_width == out_width
    q_blk = q_col0 // mem_width
    return pl.pallas_call(
        functools.partial(_mem_attn_kernel, scale=hd ** -0.5, heads=MEM_HEADS, hd=hd, ctx_col0=ctx_col0),
        grid=(s // tq,),
        in_specs=[pl.BlockSpec((tq, mem_width), lambda i: (i, q_blk)),
                  pl.BlockSpec((mt, 2 * mem_width), lambda i: (0, 0))],
        out_specs=pl.BlockSpec((tq, out_width), lambda i: (i, 0)),
        out_shape=jax.ShapeDtypeStruct((s, out_width), BF16),
        compiler_params=_params(4 * tq * out_width * 2 + 32 * tq * hd * 4, 1),
        name="mem_attention",
    )(qarr, mem_kv)


def _fox_cum_kernel(x_ref, b_ref, o_ref, carry, *, cb):
    @pl.when(pl.program_id(0) == 0)
    def _():
        carry[...] = jnp.zeros_like(carry)

    x = x_ref[...] + b_ref[...]
    log_f = jnp.minimum(x, 0.0) - jnp.log1p(jnp.exp(-jnp.abs(x)))
    cs = _cumsum_rows(_tri_lower(cb), log_f) + carry[...]
    o_ref[...] = cs
    carry[...] = cs[cb - 1:cb, :]


def fox_cum(logits, bias):
    s = logits.shape[0]
    cb = min(CUM_BLK, s)
    return pl.pallas_call(
        functools.partial(_fox_cum_kernel, cb=cb),
        grid=(s // cb,),
        in_specs=[pl.BlockSpec((cb, LANES), lambda i: (i, 0)),
                  pl.BlockSpec((1, LANES), lambda i: (0, 0))],
        out_specs=pl.BlockSpec((cb, LANES), lambda i: (i, 0)),
        out_shape=jax.ShapeDtypeStruct((s, LANES), F32),
        scratch_shapes=[pltpu.VMEM((1, LANES), F32)],
        compiler_params=_params(16 << 20, 1),
        name="fox_cum",
    )(logits, bias)


def _ssd_kernel(z_ref, xs_ref, b_ref, c_ref, dtc_ref, dtr_ref, bias_e_ref, alog_e_ref, dskip_e_ref,
                bias_r_ref, alog_r_ref, ng_ref, mix_ref, o_ref, h_ref, *, heads, hdim, chunk):
    del mix_ref
    @pl.when(pl.program_id(1) == 0)
    def _():
        h_ref[...] = jnp.zeros_like(h_ref)

    L = chunk
    width = heads * hdim
    xs = xs_ref[...]
    bm = b_ref[...]
    cm = c_ref[...]
    tri = _tri_lower(L)

    lane = lax.broadcasted_iota(jnp.int32, (LANES, width), 0)
    head = lax.broadcasted_iota(jnp.int32, (LANES, width), 1) // hdim + pl.program_id(0) * heads
    spread = jnp.where(lane == head, 1.0, 0.0).astype(BF16)
    dt_raw = sum(jnp.dot(part, spread, preferred_element_type=F32) for part in _split3(dtc_ref[...]))
    dt_e = _softplus(dt_raw + bias_e_ref[...])
    a_e = dt_e * (-jnp.exp(alog_e_ref[...]))
    acs_e = _cumsum_rows(tri, a_e)
    acs_last = acs_e[L - 1:L, :]
    dt_r = _softplus(dtr_ref[...] + bias_r_ref[...])
    a_r = dt_r * (-jnp.exp(alog_r_ref[...]))
    acs_r = _cumsum_lanes(a_r, tri.T)

    xd = xs * dt_e
    xd_b = xd.astype(BF16)
    cb = lax.dot_general(cm.astype(BF16), bm.astype(BF16), _NT, preferred_element_type=F32)
    causal = _causal_mask(L, L, 0)
    head_id = lax.broadcasted_iota(jnp.int32, (L, width), 1) // hdim

    y = jnp.zeros((L, width), F32)
    for r in range(heads):
        seg = acs_e[:, r * hdim:r * hdim + 1] - acs_r[r:r + 1, :]
        decay = jnp.exp(jnp.where(causal, seg, -jnp.inf))
        yr = jnp.dot((cb * decay).astype(BF16), xd_b, preferred_element_type=F32)
        y = jnp.where(head_id == r, yr, y)

    h_in = h_ref[...]
    y_off = jnp.dot(cm.astype(BF16), h_in.astype(BF16), preferred_element_type=F32) * jnp.exp(acs_e)
    to_end = jnp.exp(acs_last - acs_e)
    states = jnp.dot(bm.T.astype(BF16), (xd * to_end).astype(BF16), preferred_element_type=F32)
    h_ref[...] = h_in * jnp.exp(acs_last) + states

    y = y + y_off + xs * dskip_e_ref[...]
    g = y * _silu(z_ref[...])
    g = g * lax.rsqrt(jnp.mean(g * g, axis=-1, keepdims=True) + NORM_EPS)
    o_ref[...] = (g * ng_ref[...]).astype(o_ref.dtype)


def ssd_scan(zx, xbc, dt_c, dt_r, bias_e, alog_e, dskip_e, bias_r, alog_r, norm_g, inner, mix):
    s = zx.shape[0]
    L = SSD_CHUNK
    G = SSD_GROUPS
    width = inner // G
    heads = width // SSD_HEAD_DIM
    N = SSD_STATE
    b_off = inner // N
    c_off = b_off + G
    row = lambda g, c: (0, g)
    return pl.pallas_call(
        functools.partial(_ssd_kernel, heads=heads, hdim=SSD_HEAD_DIM, chunk=L),
        grid=(G, s // L),
        in_specs=[pl.BlockSpec((L, width), lambda g, c: (c, g)),
                  pl.BlockSpec((L, width), lambda g, c: (c, g)),
                  pl.BlockSpec((L, N), lambda g, c: (c, b_off + g)),
                  pl.BlockSpec((L, N), lambda g, c: (c, c_off + g)),
                  pl.BlockSpec((L, LANES), lambda g, c: (c, 0)),
                  pl.BlockSpec((None, SUBLANES, L), lambda g, c: (g, 0, c)),
                  pl.BlockSpec((1, width), row),
                  pl.BlockSpec((1, width), row),
                  pl.BlockSpec((1, width), row),
                  pl.BlockSpec((None, SUBLANES, 1), lambda g, c: (g, 0, 0)),
                  pl.BlockSpec((None, SUBLANES, 1), lambda g, c: (g, 0, 0)),
                  pl.BlockSpec((1, width), row),
                  pl.BlockSpec(memory_space=pl.ANY)],
        out_specs=pl.BlockSpec((L, width), lambda g, c: (c, g)),
        out_shape=jax.ShapeDtypeStruct(mix.shape, mix.dtype),
        input_output_aliases={12: 0},
        scratch_shapes=[pltpu.VMEM((N, width), F32)],
        compiler_params=_params(32 << 20, 2),
        name="ssd_scan",
    )(zx, xbc, xbc, xbc, dt_c, dt_r, bias_e, alog_e, dskip_e, bias_r, alog_r, norm_g, mix)


def _conv_rows(w, b):
    taps, c = w.shape
    return jnp.concatenate([w.astype(F32), b.reshape(1, c).astype(F32),
                            jnp.zeros((SUBLANES - taps - 1, c), F32)], axis=0)


def _pad_cols(a, n):
    return jnp.pad(a, ((0, 0), (0, n - a.shape[1])))


def kernel(x, mem, positions, norm_mix, norm_mem, w_mem_kv, w_out, norm_ffn, w_up, conv_ffn_w, conv_ffn_b,
           w_down, a_w_in, a_lambda, a_subln, b_w_in, b_forget_bias, c_w_in, c_conv_w, c_conv_b, c_dt_bias,
           c_a_log, c_d_skip, c_norm_gate, final_norm):
    batch, seq, d_model = x.shape
    depth = norm_mix.shape[0]
    mem_width = w_mem_kv.shape[2] // 2
    tok_width = d_model - mem_width
    d_ff = w_down.shape[1]
    ffp = DOWN_BK * ((d_ff + DOWN_BK - 1) // DOWN_BK)
    diff_heads = tok_width // (2 * HEAD_DIM)
    fox_heads = tok_width // HEAD_DIM
    ssd_heads = tok_width // SSD_HEAD_DIM
    heads_per_group = ssd_heads // SSD_GROUPS
    conv_ch = tok_width + 2 * SSD_GROUPS * SSD_STATE
    q_scale = HEAD_DIM ** -0.5 * LOG2E

    outs = []
    for b in range(batch):
        xb = x[b]
        mem_b = mem[b]
        cos_t, sa, sb = rope_tables(positions[b])
        for i in range(depth):
            kind, j = i % N_MIXERS, i // N_MIXERS
            h = rmsnorm(xb, norm_mix[i], BF16)
            mem_kv = matmul(rmsnorm(mem_b, norm_mem[i], BF16), w_mem_kv[i].astype(BF16), BF16)
            if kind == 0:
                qk = matmul_stream(h, a_w_in, j, BF16, 0, 2 * tok_width, rope=(cos_t, sa, sb, tok_width, q_scale))
                vq = matmul_stream(h, a_w_in, j, BF16, 2 * tok_width, tok_width + mem_width)
                mix = mem_attention(vq, tok_width, mem_kv, d_model, tok_width)
                lam_init = 0.8 - 0.6 * math.exp(-0.3 * i)
                mix = diff_attention(qk, vq, a_lambda[j], a_subln[j], lam_init, diff_heads, mix)
            elif kind == 1:
                w = b_w_in[j]
                w_t = jnp.swapaxes(b_w_in, 1, 2)
                q = matmul_stream(h, w_t, j, F32, 0, tok_width, w_t=True)
                kv = matmul_stream(h, w_t, j, BF16, tok_width, 2 * tok_width, w_t=True)
                w_gate = _pad_cols(w[:, 3 * tok_width:3 * tok_width + fox_heads], LANES).astype(BF16)
                logits = matmul(h, w_gate, F32)
                mem_q = matmul(h, w[:, 3 * tok_width + fox_heads:].astype(BF16), BF16)
                mix = mem_attention(mem_q, 0, mem_kv, d_model, tok_width)
                bias = _pad_cols(b_forget_bias[j].reshape(1, fox_heads).astype(F32), LANES)
                cum = fox_cum(logits, bias)
                negcum = (-cum[:, :fox_heads]).T.reshape(fox_heads, 1, seq)
                mix = fox_attention(q, kv, negcum, fox_heads, mix)
            else:
                w = c_w_in[j]
                n_zx = tok_width + conv_ch
                zx = matmul_stream(h, jnp.swapaxes(c_w_in, 1, 2), j, F32, 0, n_zx, w_t=True)
                dt_c = matmul(h, _pad_cols(w[:, n_zx:n_zx + ssd_heads], LANES).astype(BF16), F32)
                mem_q = matmul(h, w[:, n_zx + ssd_heads:].astype(BF16), BF16)
                mix = mem_attention(mem_q, 0, mem_kv, d_model, tok_width)
                xbc = dwconv_silu(zx, tok_width, conv_ch, _conv_rows(c_conv_w[j], c_conv_b[j]), SSD_CONV)
                rep = lambda a: jnp.repeat(a.astype(F32), SSD_HEAD_DIM, axis=-1)
                pad_h = SUBLANES - heads_per_group
                by_group = lambda a: jnp.pad(a.astype(F32).reshape(SSD_GROUPS, heads_per_group, -1),
                                             ((0, 0), (0, pad_h), (0, 0)))
                dt_r = by_group(dt_c[:, :ssd_heads].T)
                mix = ssd_scan(zx, xbc, dt_c, dt_r,
                               rep(c_dt_bias[j].reshape(1, -1)), rep(c_a_log[j].reshape(1, -1)),
                               rep(c_d_skip[j].reshape(1, -1)),
                               by_group(c_dt_bias[j].reshape(-1, 1)), by_group(c_a_log[j].reshape(-1, 1)),
                               c_norm_gate[j].reshape(1, -1).astype(F32), tok_width, mix)
            xb = matmul(mix, w_out[i].astype(BF16), F32, res=xb)

            hf = rmsnorm(xb, norm_ffn[i], BF16)
            hidden = ffn_up_stream(hf, w_up, i, _conv_rows(conv_ffn_w[i], conv_ffn_b[i]), d_ff, ffp)
            xb = matmul(hidden, w_down[i].astype(BF16), F32, res=xb, bk=DOWN_BK)
        outs.append(rmsnorm(xb, final_norm, x.dtype))
    return outs[0].reshape(x.shape) if batch == 1 else jnp.stack(outs, axis=0)
```

```python
import functools
import math

import jax
import jax.numpy as jnp
from jax import lax
from jax.experimental import pallas as pl
from jax.experimental.pallas import tpu as pltpu

F32 = jnp.float32
BF16 = jnp.bfloat16

HEAD_DIM = 128
MEM_HEADS = 4
SSD_HEAD_DIM = 64
SSD_GROUPS = 8
SSD_STATE = 128
SSD_CONV = 4
SSD_CHUNK = 128
ROPE_THETA = 500000.0
ROPE_DIM = HEAD_DIM // 4
FFN_CONV = 3
NORM_EPS = 1e-6
N_MIXERS = 3
LOG2E = math.log2(math.e)

LANES = 128
SUBLANES = 8
VMEM_LIMIT_CAP = 56 * 1024 * 1024

MM_BM = 1024
MM_BN = 1024
FFN_BN = 512
DOWN_BK = 2816
DIFF_TQ = 1024
FOX_TQ = 2048
ATT_TK = 512
ATT_SUB = 256
MEM_TQ = 512
ROW_BLK = 256
CUM_BLK = 256


_NT = (((1,), (1,)), ((), ()))


def _vmem(nbytes):
    return int(min(VMEM_LIMIT_CAP, max(16 * 1024 * 1024, nbytes)))


def _params(nbytes, n_grid):
    return pltpu.CompilerParams(
        dimension_semantics=("arbitrary",) * n_grid, vmem_limit_bytes=_vmem(nbytes))


def _softplus(x):
    return jnp.maximum(x, 0.0) + jnp.log1p(jnp.exp(-jnp.abs(x)))


def _silu(x):
    return x / (1.0 + jnp.exp(-x))


def _split3(x):
    hi = x.astype(BF16)
    r = x - hi.astype(F32)
    mid = r.astype(BF16)
    lo = (r - mid.astype(F32)).astype(BF16)
    return hi, mid, lo


def _tri_lower(n):
    r = lax.broadcasted_iota(jnp.int32, (n, n), 0)
    c = lax.broadcasted_iota(jnp.int32, (n, n), 1)
    return jnp.where(c <= r, 1.0, 0.0).astype(BF16)


def _cumsum_rows(tri, x):
    hi, mid, lo = _split3(x)
    d = lambda a: jnp.dot(tri, a, preferred_element_type=F32)
    return d(hi) + d(mid) + d(lo)


def _cumsum_lanes(x, tri_t):
    hi, mid, lo = _split3(x)
    d = lambda a: jnp.dot(a, tri_t, preferred_element_type=F32)
    return d(hi) + d(mid) + d(lo)


def _rmsnorm_kernel(x_ref, g_ref, o_ref):
    x = x_ref[...].astype(F32)
    ms = jnp.mean(x * x, axis=-1, keepdims=True)
    o_ref[...] = (x * lax.rsqrt(ms + NORM_EPS) * g_ref[...]).astype(o_ref.dtype)


def rmsnorm(x, g, out_dtype):
    m, d = x.shape
    rb = min(ROW_BLK, m)
    return pl.pallas_call(
        _rmsnorm_kernel,
        grid=(m // rb,),
        in_specs=[pl.BlockSpec((rb, d), lambda i: (i, 0)),
                  pl.BlockSpec((1, d), lambda i: (0, 0))],
        out_specs=pl.BlockSpec((rb, d), lambda i: (i, 0)),
        out_shape=jax.ShapeDtypeStruct((m, d), out_dtype),
        compiler_params=_params(6 * rb * d * 4, 1),
        name="rmsnorm",
    )(x, g.reshape(1, d).astype(F32))


def _mm_kernel(*refs, nk, bk, k_true, has_res):
    if has_res:
        x_ref, w_ref, r_ref, o_ref = refs[:4]
        scratch = refs[4:]
    else:
        x_ref, w_ref, o_ref = refs[:3]
        r_ref = None
        scratch = refs[3:]
    def product(last):
        w = w_ref[...]
        if last and k_true is not None:
            row = lax.broadcasted_iota(jnp.int32, w.shape, 0)
            w = jnp.where(row < k_true - (nk - 1) * bk, w, jnp.zeros_like(w))
        return jnp.dot(x_ref[...], w, preferred_element_type=F32)

    if nk == 1:
        part = product(True)
        if has_res:
            part = part + r_ref[...]
        o_ref[...] = part.astype(o_ref.dtype)
        return
    acc_ref = scratch[0]
    k = pl.program_id(2)

    @pl.when(k == 0)
    def _():
        acc_ref[...] = product(False)

    @pl.when(jnp.logical_and(k > 0, k < nk - 1))
    def _():
        acc_ref[...] = acc_ref[...] + product(False)

    @pl.when(k == nk - 1)
    def _():
        tot = acc_ref[...] + product(True)
        if has_res:
            tot = tot + r_ref[...]
        o_ref[...] = tot.astype(o_ref.dtype)


def matmul(x, w, out_dtype, col0=0, ncols=None, res=None, bk=None, layer=None):
    m, kdim = x.shape
    w_rows, w_cols = w.shape[-2:]
    n = w_cols - col0 if ncols is None else ncols
    bm = min(MM_BM, m)
    bn = min(MM_BN, n)
    bk = kdim if bk is None else bk
    nk = kdim // bk
    assert m % bm == 0 and n % bn == 0 and kdim % bk == 0 and col0 % bn == 0
    k_true = None if w_rows == kdim else w_rows
    assert w_rows <= kdim and w_rows > (nk - 1) * bk
    joff = col0 // bn
    if layer is None:
        w_spec = pl.BlockSpec((bk, bn), lambda j, i, k: (k, j + joff))
    else:
        w_spec = pl.BlockSpec((None, bk, bn), lambda j, i, k: (layer, k, j + joff))
    in_specs = [pl.BlockSpec((bm, bk), lambda j, i, k: (i, k)), w_spec]
    args = [x, w]
    if res is not None:
        in_specs.append(pl.BlockSpec((bm, bn), lambda j, i, k: (i, j)))
        args.append(res)
    osz = jnp.dtype(out_dtype).itemsize
    nbytes = 2 * (bm * bk * 2 + bk * bn * 2 + bm * bn * osz) + 2 * bm * bn * 4
    if res is not None:
        nbytes += 2 * bm * bn * 4
    scratch = []
    if nk > 1:
        scratch.append(pltpu.VMEM((bm, bn), F32))
        nbytes += bm * bn * 4
    return pl.pallas_call(
        functools.partial(_mm_kernel, nk=nk, bk=bk, k_true=k_true, has_res=res is not None),
        grid=(n // bn, m // bm, nk),
        in_specs=in_specs,
        out_specs=pl.BlockSpec((bm, bn), lambda j, i, k: (i, j)),
        out_shape=jax.ShapeDtypeStruct((m, n), out_dtype),
        scratch_shapes=scratch,
        compiler_params=_params(nbytes + (4 << 20), 3),
        name="matmul",
    )(*args)


def _mm_stream_kernel(*refs, layer, col0, kc, nchunks, rope, w_t):
    x_ref, w_hbm = refs[:2]
    if rope is None:
        o_ref, wbf, stage, sem = refs[2:]
    else:
        cos_ref, sa_ref, sb_ref, o_ref, wbf, stage, sem = refs[2:]
    j = pl.program_id(0)
    i = pl.program_id(1)
    nj = pl.num_programs(0)
    slot = lax.rem(j, 2)
    bn = o_ref.shape[1]

    def slab_copy(jb, c):
        rows = pl.ds(pl.multiple_of(c * kc, kc), kc)
        cols = pl.ds(pl.multiple_of(col0 + jb * bn, LANES), bn)
        src = w_hbm.at[layer, cols, rows] if w_t else w_hbm.at[layer, rows, cols]
        return pltpu.make_async_copy(src, stage, sem.at[0])

    def land(c, dst):
        rows = pl.ds(pl.multiple_of(c * kc, kc), kc)
        if w_t:
            wbf[dst, :, rows] = stage[...].astype(BF16)
        else:
            wbf[dst, rows, :] = stage[...].astype(BF16)

    @pl.when(jnp.logical_and(j == 0, i == 0))
    def _():
        for c in range(nchunks):
            cp = slab_copy(0, c)
            cp.start()
            cp.wait()
            land(c, 0)

    more = j + 1 < nj

    @pl.when(more)
    def _():
        slab_copy(j + 1, i).start()

    if w_t:
        acc = lax.dot_general(x_ref[...], wbf[slot], _NT, preferred_element_type=F32)
    else:
        acc = jnp.dot(x_ref[...], wbf[slot], preferred_element_type=F32)
    if rope is None:
        o_ref[...] = acc.astype(o_ref.dtype)
    else:
        q_blocks, q_scale = rope
        cos = cos_ref[...]
        sa = sa_ref[...]
        sb = sb_ref[...]
        half = ROPE_DIM // 2
        scale = jnp.where(j < q_blocks, q_scale, 1.0)
        for gi in range(bn // LANES):
            t = acc[:, gi * LANES:(gi + 1) * LANES]
            up = pltpu.roll(t, LANES - half, 1)
            dn = pltpu.roll(t, half, 1)
            o_ref[:, gi * LANES:(gi + 1) * LANES] = ((t * cos + up * sa + dn * sb) * scale).astype(o_ref.dtype)

    @pl.when(more)
    def _():
        slab_copy(j + 1, i).wait()
        land(i, 1 - slot)


def matmul_stream(x, w, layer, out_dtype, col0, ncols, rope=None, w_t=False):
    m, kdim = x.shape
    bm = min(MM_BM, m)
    bn = MM_BN
    nm = m // bm
    assert m % bm == 0 and ncols % bn == 0 and col0 % LANES == 0 and kdim % nm == 0 and w.shape[2 if w_t else 1] == kdim
    kc = kdim // nm
    in_specs = [pl.BlockSpec((bm, kdim), lambda j, i: (i, 0)), pl.BlockSpec(memory_space=pl.ANY)]
    args = [x, w]
    rope_static = None
    if rope is not None:
        cos, sa, sb, q_cols, q_scale = rope
        assert q_cols % bn == 0
        tab = pl.BlockSpec((bm, LANES), lambda j, i: (i, 0))
        in_specs += [tab, tab, tab]
        args += [cos, sa, sb]
        rope_static = (q_cols // bn, q_scale)
    osz = jnp.dtype(out_dtype).itemsize
    nbytes = 2 * (bm * kdim * 2 + bm * bn * osz) + 2 * kdim * bn * 2 + kc * bn * 4 + 3 * bm * bn * 4
    return pl.pallas_call(
        functools.partial(_mm_stream_kernel, layer=layer, col0=col0, kc=kc, nchunks=nm, rope=rope_static, w_t=w_t),
        grid=(ncols // bn, nm),
        in_specs=in_specs,
        out_specs=pl.BlockSpec((bm, bn), lambda j, i: (i, j)),
        out_shape=jax.ShapeDtypeStruct((m, ncols), out_dtype),
        scratch_shapes=[pltpu.VMEM((2, bn, kdim) if w_t else (2, kdim, bn), BF16),
                        pltpu.VMEM((bn, kc) if w_t else (kc, bn), F32),
                        pltpu.SemaphoreType.DMA((1,))],
        compiler_params=_params(nbytes + (2 << 20), 2),
        name="matmul_stream",
    )(*args)


def _ffn_up_stream_kernel(x_ref, w_hbm, cg_ref, cv0_ref, cv1_ref, o_ref, tail_g, tail_v, wg_b, wv_b,
                           st_g, st_v, sem, *, layer, bm, bn, d_ff, kc, nchunks):
    j = pl.program_id(0)
    i = pl.program_id(1)
    nj = pl.num_programs(0)
    slot = lax.rem(j, 2)
    hb = bn // 2

    def slab_copies(jb, c):
        rows = pl.ds(pl.multiple_of(c * kc, kc), kc)
        g0 = pl.multiple_of(jb * bn, LANES)
        v0 = pl.multiple_of(d_ff + jb * bn, LANES)
        v1 = pl.multiple_of(jnp.minimum(d_ff + jb * bn + hb, 2 * d_ff - hb), LANES)
        return (pltpu.make_async_copy(w_hbm.at[layer, rows, pl.ds(g0, bn)], st_g, sem.at[0]),
                pltpu.make_async_copy(w_hbm.at[layer, rows, pl.ds(v0, hb)], st_v.at[:, pl.ds(0, hb)], sem.at[1]),
                pltpu.make_async_copy(w_hbm.at[layer, rows, pl.ds(v1, hb)], st_v.at[:, pl.ds(hb, hb)], sem.at[2]))

    def land(c, dst):
        rows = pl.ds(pl.multiple_of(c * kc, kc), kc)
        wg_b[dst, rows, :] = st_g[...].astype(BF16)
        wv_b[dst, rows, :] = st_v[...].astype(BF16)

    @pl.when(jnp.logical_and(j == 0, i == 0))
    def _():
        for c in range(nchunks):
            cps = slab_copies(0, c)
            for cp in cps:
                cp.start()
            for cp in cps:
                cp.wait()
            land(c, 0)

    more = j + 1 < nj

    @pl.when(more)
    def _():
        for cp in slab_copies(j + 1, i):
            cp.start()

    @pl.when(i == 0)
    def _():
        tail_g[...] = jnp.zeros_like(tail_g)
        tail_v[...] = jnp.zeros_like(tail_v)

    x = x_ref[...]

    def conv(u, c, tail):
        ext = jnp.concatenate([tail[...], u], axis=0)
        u1 = pltpu.roll(ext, 1, 0)[SUBLANES:]
        u2 = pltpu.roll(ext, 2, 0)[SUBLANES:]
        tail[...] = u[bm - SUBLANES:]
        return c[3:4] + c[2:3] * u + c[1:2] * u1 + c[0:1] * u2

    ug = jnp.dot(x, wg_b[slot], preferred_element_type=F32)
    uv = jnp.dot(x, wv_b[slot], preferred_element_type=F32)
    g = conv(ug, cg_ref[...], tail_g)
    v = conv(uv, jnp.concatenate([cv0_ref[...], cv1_ref[...]], axis=1), tail_v)
    col = lax.broadcasted_iota(jnp.int32, (bm, bn), 1) + j * bn
    o_ref[...] = jnp.where(col < d_ff, _silu(g) * v, 0.0).astype(o_ref.dtype)

    @pl.when(more)
    def _():
        for cp in slab_copies(j + 1, i):
            cp.wait()
        land(i, 1 - slot)


def ffn_up_stream(h, w_up, layer, c8, d_ff, ffp):
    m, d = h.shape
    bm = min(MM_BM, m)
    bn = FFN_BN
    hb = bn // 2
    nm = m // bm
    assert ffp % bn == 0 and d_ff % hb == 0 and d % nm == 0 and ffp <= 2 * d_ff
    kc = d // nm
    v0 = d_ff // hb
    vlast = (2 * d_ff) // hb - 1
    vmap0 = lambda j, i: (0, jnp.minimum(v0 + 2 * j, vlast))
    vmap1 = lambda j, i: (0, jnp.minimum(v0 + 2 * j + 1, vlast))
    nbytes = (2 * (bm * d * 2 + bm * bn * 2) + 2 * 2 * d * bn * 2 + 2 * kc * bn * 4 + 12 * bm * bn * 4)
    return pl.pallas_call(
        functools.partial(_ffn_up_stream_kernel, layer=layer, bm=bm, bn=bn, d_ff=d_ff, kc=kc, nchunks=nm),
        grid=(ffp // bn, nm),
        in_specs=[pl.BlockSpec((bm, d), lambda j, i: (i, 0)),
                  pl.BlockSpec(memory_space=pl.ANY),
                  pl.BlockSpec((SUBLANES, bn), lambda j, i: (0, j)),
                  pl.BlockSpec((SUBLANES, hb), vmap0),
                  pl.BlockSpec((SUBLANES, hb), vmap1)],
        out_specs=pl.BlockSpec((bm, bn), lambda j, i: (i, j)),
        out_shape=jax.ShapeDtypeStruct((m, ffp), BF16),
        scratch_shapes=[pltpu.VMEM((SUBLANES, bn), F32), pltpu.VMEM((SUBLANES, bn), F32),
                        pltpu.VMEM((2, d, bn), BF16), pltpu.VMEM((2, d, bn), BF16),
                        pltpu.VMEM((kc, bn), F32), pltpu.VMEM((kc, bn), F32),
                        pltpu.SemaphoreType.DMA((3,))],
        compiler_params=_params(nbytes, 2),
        name="ffn_up_stream",
    )(h, w_up, c8, c8, c8)


def _dwconv_silu_kernel(x_ref, w_ref, o_ref, tail, *, taps, rb):
    @pl.when(pl.program_id(1) == 0)
    def _():
        tail[...] = jnp.zeros_like(tail)

    x = x_ref[...]
    ext = jnp.concatenate([tail[...], x], axis=0)
    w = w_ref[...]
    out = w[taps:taps + 1] + w[taps - 1:taps] * x
    for k in range(taps - 1):
        d = taps - 1 - k
        out = out + pltpu.roll(ext, d, 0)[SUBLANES:] * w[k:k + 1]
    tail[...] = x[rb - SUBLANES:]
    o_ref[...] = _silu(out).astype(o_ref.dtype)


def dwconv_silu(x, col0, ncols, w8, taps):
    m = x.shape[0]
    rb = min(ROW_BLK, m)
    cb = 1024
    assert col0 % cb == 0 and ncols % cb == 0
    off = col0 // cb
    return pl.pallas_call(
        functools.partial(_dwconv_silu_kernel, taps=taps, rb=rb),
        grid=(ncols // cb, m // rb),
        in_specs=[pl.BlockSpec((rb, cb), lambda j, i: (i, j + off)),
                  pl.BlockSpec((SUBLANES, cb), lambda j, i: (0, j))],
        out_specs=pl.BlockSpec((rb, cb), lambda j, i: (i, j)),
        out_shape=jax.ShapeDtypeStruct((m, ncols), F32),
        scratch_shapes=[pltpu.VMEM((SUBLANES, cb), F32)],
        compiler_params=_params(16 * rb * cb * 4, 2),
        name="dwconv_silu",
    )(x, w8)


def rope_tables(positions):
    half = ROPE_DIM // 2
    inv_freq = jnp.power(jnp.float32(ROPE_THETA), -jnp.arange(half, dtype=F32) * (2.0 / ROPE_DIM))
    ang = positions.astype(F32)[:, None] * inv_freq
    cos, sin = jnp.cos(ang), jnp.sin(ang)
    s = positions.shape[0]
    pad = LANES - ROPE_DIM
    cos_t = jnp.concatenate([cos, cos, jnp.ones((s, pad), F32)], axis=1)
    sa = jnp.concatenate([-sin, jnp.zeros((s, LANES - half), F32)], axis=1)
    sb = jnp.concatenate([jnp.zeros((s, half), F32), sin, jnp.zeros((s, pad), F32)], axis=1)
    return cos_t, sa, sb


def _flash_softmax(s, m_ref, l_ref, rows):
    tk = s.shape[1]
    sc = [s[:, c * LANES:(c + 1) * LANES] for c in range(tk // LANES)]
    m_prev = m_ref[rows, :]
    m_new = jnp.maximum(m_prev, jnp.max(functools.reduce(jnp.maximum, sc), axis=-1, keepdims=True))
    alpha = jnp.exp2(m_prev - m_new)
    pc = [jnp.exp2(c - m_new) for c in sc]
    l_ref[rows, :] = alpha * l_ref[rows, :] + functools.reduce(jnp.add, pc)
    m_ref[rows, :] = m_new
    return jnp.concatenate([c.astype(BF16) for c in pc], axis=1), alpha


def _flash_accumulate(p, alpha, v, acc_ref, rows):
    pv = jnp.dot(p, v, preferred_element_type=F32)
    a = jnp.concatenate([alpha] * (pv.shape[1] // LANES), axis=1) if pv.shape[1] > LANES else alpha
    acc_ref[rows, :] = a * acc_ref[rows, :] + pv


def _flash_block(chains, v):
    pa = [_flash_softmax(s, m_ref, l_ref, rows) for s, m_ref, l_ref, _, rows in chains]
    for (p, alpha), (_, _, _, acc_ref, rows) in zip(pa, chains):
        _flash_accumulate(p, alpha, v, acc_ref, rows)


def _flash_init(m_ref, l_ref, acc_ref):
    m_ref[...] = jnp.full(m_ref.shape, -jnp.inf, F32)
    l_ref[...] = jnp.zeros(l_ref.shape, F32)
    acc_ref[...] = jnp.zeros(acc_ref.shape, F32)


def _flash_result(l_ref, acc_ref):
    return acc_ref[...] / jnp.sum(l_ref[...], axis=-1, keepdims=True)


def _causal_mask(tr, tk, delta):
    r = lax.broadcasted_iota(jnp.int32, (tr, tk), 0)
    c = lax.broadcasted_iota(jnp.int32, (tr, tk), 1)
    return c <= r + delta


def _diag_plan(tq, tk, tr):
    plan = []
    for co in range(0, tq, tk):
        subs = []
        for r0 in range(0, tq, tr):
            if r0 + tr - 1 < co:
                continue
            subs.append((r0, None if co + tk - 1 <= r0 else r0 - co))
        plan.append(subs)
    return plan


def _causal_sweep(qi, tq, tk, tr, block):
    full = [(r0, None) for r0 in range(0, tq, tr)]
    per_q = tq // tk

    def body(j, carry):
        block(j, full)
        return carry

    lax.fori_loop(0, qi * per_q, body, 0)
    for b, subs in enumerate(_diag_plan(tq, tk, tr)):
        block(qi * per_q + b, subs)


def _fox_kernel(q_ref, k_ref, v_ref, b_ref, mix_ref, o_ref, q_scr, m_ref, l_ref, acc_ref, *, tq, tk, tr, q_scale):
    del mix_ref
    q_scr[...] = (q_ref[...] * q_scale).astype(BF16)
    _flash_init(m_ref, l_ref, acc_ref)

    def block(j, subs):
        off = pl.multiple_of(j * tk, tk)
        k = k_ref[pl.ds(off, tk), :]
        v = v_ref[pl.ds(off, tk), :]
        bias = b_ref[:, pl.ds(off, tk)] * LOG2E
        chains = []
        for r0, delta in subs:
            rows = pl.ds(r0, tr)
            s = lax.dot_general(q_scr[rows, :], k, _NT, preferred_element_type=F32) + bias
            if delta is not None:
                s = jnp.where(_causal_mask(tr, tk, delta), s, -jnp.inf)
            chains.append((s, m_ref, l_ref, acc_ref, rows))
        _flash_block(chains, v)

    _causal_sweep(pl.program_id(1), tq, tk, tr, block)
    o_ref[...] = _flash_result(l_ref, acc_ref).astype(o_ref.dtype)


def _att_tiles(s, tq_max):
    tq = min(tq_max, s)
    tk = min(ATT_TK, tq)
    tr = min(ATT_SUB, tk)
    assert s % tq == 0 and tq % tk == 0 and tk % tr == 0
    return tq, tk, tr


def fox_attention(q, kv, negcum, n_heads, mix):
    s = q.shape[0]
    tq, tk, tr = _att_tiles(s, FOX_TQ)
    hd = HEAD_DIM
    nbytes = 2 * (2 * s * hd * 2 + tq * hd * 4 + tq * hd * 2 + s * 4) + 12 * tq * tk * 4
    return pl.pallas_call(
        functools.partial(_fox_kernel, tq=tq, tk=tk, tr=tr, q_scale=hd ** -0.5 * LOG2E),
        grid=(n_heads, s // tq),
        in_specs=[pl.BlockSpec((tq, hd), lambda h, i: (i, h)),
                  pl.BlockSpec((s, hd), lambda h, i: (0, h)),
                  pl.BlockSpec((s, hd), lambda h, i: (0, n_heads + h)),
                  pl.BlockSpec((None, 1, s), lambda h, i: (h, 0, 0)),
                  pl.BlockSpec(memory_space=pl.ANY)],
        out_specs=pl.BlockSpec((tq, hd), lambda h, i: (i, h)),
        out_shape=jax.ShapeDtypeStruct(mix.shape, mix.dtype),
        input_output_aliases={4: 0},
        scratch_shapes=[pltpu.VMEM((tq, hd), BF16), pltpu.VMEM((tq, LANES), F32),
                        pltpu.VMEM((tq, LANES), F32), pltpu.VMEM((tq, hd), F32)],
        compiler_params=_params(nbytes, 2),
        name="fox_attention",
    )(q, kv, kv, negcum, mix)


def _diff_kernel(q_ref, k_ref, v_ref, lam_ref, g_ref, mix_ref, o_ref, m0, l0, a0, m1, l1, a1,
                 *, tq, tk, tr, lam_init):
    del mix_ref
    hd = HEAD_DIM
    _flash_init(m0, l0, a0)
    _flash_init(m1, l1, a1)

    def block(j, subs):
        off = pl.multiple_of(j * tk, tk)
        v = v_ref[pl.ds(off, tk), :]
        chains = []
        for lo, st in ((0, (m0, l0, a0)), (hd, (m1, l1, a1))):
            k = k_ref[pl.ds(off, tk), lo:lo + hd]
            for r0, delta in subs:
                rows = pl.ds(r0, tr)
                s = lax.dot_general(q_ref[rows, lo:lo + hd], k, _NT, preferred_element_type=F32)
                if delta is not None:
                    s = jnp.where(_causal_mask(tr, tk, delta), s, -jnp.inf)
                chains.append((s,) + st + (rows,))
        _flash_block(chains, v)

    _causal_sweep(pl.program_id(1), tq, tk, tr, block)
    lv = lam_ref[...]
    lam = (jnp.exp(jnp.sum(lv[0:1] * lv[1:2], axis=-1, keepdims=True))
           - jnp.exp(jnp.sum(lv[2:3] * lv[3:4], axis=-1, keepdims=True)) + lam_init)
    o = _flash_result(l0, a0) - lam * _flash_result(l1, a1)
    o = o * lax.rsqrt(jnp.mean(o * o, axis=-1, keepdims=True) + NORM_EPS) * g_ref[...]
    o_ref[...] = (o * (1.0 - lam_init)).astype(o_ref.dtype)


def diff_attention(qk, v, lam_vecs, subln_g, lam_init, n_heads, mix):
    s = qk.shape[0]
    tq, tk, tr = _att_tiles(s, DIFF_TQ)
    t = tq
    w = 2 * HEAD_DIM
    nbytes = 2 * (2 * s * w * 2 + 2 * tq * w * 2) + 24 * tq * tk * 4
    stats = lambda: [pltpu.VMEM((t, LANES), F32), pltpu.VMEM((t, LANES), F32), pltpu.VMEM((t, w), F32)]
    return pl.pallas_call(
        functools.partial(_diff_kernel, tq=tq, tk=tk, tr=tr, lam_init=lam_init),
        grid=(n_heads, s // t),
        in_specs=[pl.BlockSpec((t, w), lambda h, i: (i, h)),
                  pl.BlockSpec((s, w), lambda h, i: (0, n_heads + h)),
                  pl.BlockSpec((s, w), lambda h, i: (0, h)),
                  pl.BlockSpec((4, HEAD_DIM), lambda h, i: (0, 0)),
                  pl.BlockSpec((1, w), lambda h, i: (0, 0)),
                  pl.BlockSpec(memory_space=pl.ANY)],
        out_specs=pl.BlockSpec((t, w), lambda h, i: (i, h)),
        out_shape=jax.ShapeDtypeStruct(mix.shape, mix.dtype),
        input_output_aliases={5: 0},
        scratch_shapes=stats() + stats(),
        compiler_params=_params(nbytes, 2),
        name="diff_attention",
    )(qk, qk, v, lam_vecs.astype(F32), subln_g.reshape(1, w).astype(F32), mix)


def _mem_attn_kernel(q_ref, kv_ref, o_ref, *, scale, heads, hd, ctx_col0):
    o_ref[:, :ctx_col0] = jnp.zeros((o_ref.shape[0], ctx_col0), o_ref.dtype)
    for h in range(heads):
        k = kv_ref[:, h * hd:(h + 1) * hd]
        v = kv_ref[:, (heads + h) * hd:(heads + h + 1) * hd]
        s = lax.dot_general(q_ref[:, h * hd:(h + 1) * hd], k, _NT, preferred_element_type=F32) * scale
        p = jnp.exp(s - jnp.max(s, axis=-1, keepdims=True))
        l = jnp.sum(p, axis=-1, keepdims=True)
        o = jnp.dot(p.astype(BF16), v, preferred_element_type=F32)
        o_ref[:, ctx_col0 + h * hd:ctx_col0 + (h + 1) * hd] = (o / l).astype(o_ref.dtype)


def mem_attention(qarr, q_col0, mem_kv, out_width, ctx_col0):
    s = qarr.shape[0]
    mem_width = mem_kv.shape[1] // 2
    hd = mem_width // MEM_HEADS
    mt = mem_kv.shape[0]
    tq = min(MEM_TQ, s)
    assert q_col0 % mem_width == 0 and ctx_col0 + mem_width == out_width
    q_blk = q_col0 // mem_width
    return pl.pallas_call(
        functools.partial(_mem_attn_kernel, scale=hd ** -0.5, heads=MEM_HEADS, hd=hd, ctx_col0=ctx_col0),
        grid=(s // tq,),
        in_specs=[pl.BlockSpec((tq, mem_width), lambda i: (i, q_blk)),
                  pl.BlockSpec((mt, 2 * mem_width), lambda i: (0, 0))],
        out_specs=pl.BlockSpec((tq, out_width), lambda i: (i, 0)),
        out_shape=jax.ShapeDtypeStruct((s, out_width), BF16),
        compiler_params=_params(4 * tq * out_width * 2 + 32 * tq * hd * 4, 1),
        name="mem_attention",
    )(qarr, mem_kv)


def _fox_cum_kernel(x_ref, b_ref, o_ref, carry, *, cb):
    @pl.when(pl.program_id(0) == 0)
    def _():
        carry[...] = jnp.zeros_like(carry)

    x = x_ref[...] + b_ref[...]
    log_f = jnp.minimum(x, 0.0) - jnp.log1p(jnp.exp(-jnp.abs(x)))
    cs = _cumsum_rows(_tri_lower(cb), log_f) + carry[...]
    o_ref[...] = cs
    carry[...] = cs[cb - 1:cb, :]


def fox_cum(logits, bias):
    s = logits.shape[0]
    cb = min(CUM_BLK, s)
    return pl.pallas_call(
        functools.partial(_fox_cum_kernel, cb=cb),
        grid=(s // cb,),
        in_specs=[pl.BlockSpec((cb, LANES), lambda i: (i, 0)),
                  pl.BlockSpec((1, LANES), lambda i: (0, 0))],
        out_specs=pl.BlockSpec((cb, LANES), lambda i: (i, 0)),
        out_shape=jax.ShapeDtypeStruct((s, LANES), F32),
        scratch_shapes=[pltpu.VMEM((1, LANES), F32)],
        compiler_params=_params(16 << 20, 1),
        name="fox_cum",
    )(logits, bias)


def _ssd_kernel(z_ref, xs_ref, b_ref, c_ref, dtc_ref, dtr_ref, bias_e_ref, alog_e_ref, dskip_e_ref,
                bias_r_ref, alog_r_ref, ng_ref, mix_ref, o_ref, h_ref, *, heads, hdim, chunk):
    del mix_ref
    @pl.when(pl.program_id(1) == 0)
    def _():
        h_ref[...] = jnp.zeros_like(h_ref)

    L = chunk
    width = heads * hdim
    xs = xs_ref[...]
    bm = b_ref[...]
    cm = c_ref[...]
    tri = _tri_lower(L)

    lane = lax.broadcasted_iota(jnp.int32, (LANES, width), 0)
    head = lax.broadcasted_iota(jnp.int32, (LANES, width), 1) // hdim + pl.program_id(0) * heads
    spread = jnp.where(lane == head, 1.0, 0.0).astype(BF16)
    dt_raw = sum(jnp.dot(part, spread, preferred_element_type=F32) for part in _split3(dtc_ref[...]))
    dt_e = _softplus(dt_raw + bias_e_ref[...])
    a_e = dt_e * (-jnp.exp(alog_e_ref[...]))
    acs_e = _cumsum_rows(tri, a_e)
    acs_last = acs_e[L - 1:L, :]
    dt_r = _softplus(dtr_ref[...] + bias_r_ref[...])
    a_r = dt_r * (-jnp.exp(alog_r_ref[...]))
    acs_r = _cumsum_lanes(a_r, tri.T)

    xd = xs * dt_e
    xd_b = xd.astype(BF16)
    cb = lax.dot_general(cm.astype(BF16), bm.astype(BF16), _NT, preferred_element_type=F32)
    causal = _causal_mask(L, L, 0)
    head_id = lax.broadcasted_iota(jnp.int32, (L, width), 1) // hdim

    y = jnp.zeros((L, width), F32)
    for r in range(heads):
        seg = acs_e[:, r * hdim:r * hdim + 1] - acs_r[r:r + 1, :]
        decay = jnp.exp(jnp.where(causal, seg, -jnp.inf))
        yr = jnp.dot((cb * decay).astype(BF16), xd_b, preferred_element_type=F32)
        y = jnp.where(head_id == r, yr, y)

    h_in = h_ref[...]
    y_off = jnp.dot(cm.astype(BF16), h_in.astype(BF16), preferred_element_type=F32) * jnp.exp(acs_e)
    to_end = jnp.exp(acs_last - acs_e)
    states = jnp.dot(bm.T.astype(BF16), (xd * to_end).astype(BF16), preferred_element_type=F32)
    h_ref[...] = h_in * jnp.exp(acs_last) + states

    y = y + y_off + xs * dskip_e_ref[...]
    g = y * _silu(z_ref[...])
    g = g * lax.rsqrt(jnp.mean(g * g, axis=-1, keepdims=True) + NORM_EPS)
    o_ref[...] = (g * ng_ref[...]).astype(o_ref.dtype)


def ssd_scan(zx, xbc, dt_c, dt_r, bias_e, alog_e, dskip_e, bias_r, alog_r, norm_g, inner, mix):
    s = zx.shape[0]
    L = SSD_CHUNK
    G = SSD_GROUPS
    width = inner // G
    heads = width // SSD_HEAD_DIM
    N = SSD_STATE
    b_off = inner // N
    c_off = b_off + G
    row = lambda g, c: (0, g)
    return pl.pallas_call(
        functools.partial(_ssd_kernel, heads=heads, hdim=SSD_HEAD_DIM, chunk=L),
        grid=(G, s // L),
        in_specs=[pl.BlockSpec((L, width), lambda g, c: (c, g)),
                  pl.BlockSpec((L, width), lambda g, c: (c, g)),
                  pl.BlockSpec((L, N), lambda g, c: (c, b_off + g)),
                  pl.BlockSpec((L, N), lambda g, c: (c, c_off + g)),
                  pl.BlockSpec((L, LANES), lambda g, c: (c, 0)),
                  pl.BlockSpec((None, SUBLANES, L), lambda g, c: (g, 0, c)),
                  pl.BlockSpec((1, width), row),
                  pl.BlockSpec((1, width), row),
                  pl.BlockSpec((1, width), row),
                  pl.BlockSpec((None, SUBLANES, 1), lambda g, c: (g, 0, 0)),
                  pl.BlockSpec((None, SUBLANES, 1), lambda g, c: (g, 0, 0)),
                  pl.BlockSpec((1, width), row),
                  pl.BlockSpec(memory_space=pl.ANY)],
        out_specs=pl.BlockSpec((L, width), lambda g, c: (c, g)),
        out_shape=jax.ShapeDtypeStruct(mix.shape, mix.dtype),
        input_output_aliases={12: 0},
        scratch_shapes=[pltpu.VMEM((N, width), F32)],
        compiler_params=_params(32 << 20, 2),
        name="ssd_scan",
    )(zx, xbc, xbc, xbc, dt_c, dt_r, bias_e, alog_e, dskip_e, bias_r, alog_r, norm_g, mix)


def _conv_rows(w, b):
    taps, c = w.shape
    return jnp.concatenate([w.astype(F32), b.reshape(1, c).astype(F32),
                            jnp.zeros((SUBLANES - taps - 1, c), F32)], axis=0)


def _pad_cols(a, n):
    return jnp.pad(a, ((0, 0), (0, n - a.shape[1])))


def kernel(x, mem, positions, norm_mix, norm_mem, w_mem_kv, w_out, norm_ffn, w_up, conv_ffn_w, conv_ffn_b,
           w_down, a_w_in, a_lambda, a_subln, b_w_in, b_forget_bias, c_w_in, c_conv_w, c_conv_b, c_dt_bias,
           c_a_log, c_d_skip, c_norm_gate, final_norm):
    batch, seq, d_model = x.shape
    depth = norm_mix.shape[0]
    mem_width = w_mem_kv.shape[2] // 2
    tok_width = d_model - mem_width
    d_ff = w_down.shape[1]
    ffp = DOWN_BK * ((d_ff + DOWN_BK - 1) // DOWN_BK)
    diff_heads = tok_width // (2 * HEAD_DIM)
    fox_heads = tok_width // HEAD_DIM
    ssd_heads = tok_width // SSD_HEAD_DIM
    heads_per_group = ssd_heads // SSD_GROUPS
    conv_ch = tok_width + 2 * SSD_GROUPS * SSD_STATE
    q_scale = HEAD_DIM ** -0.5 * LOG2E

    w_out_b = w_out.astype(BF16)
    w_down_b = w_down.astype(BF16)
    w_mem_kv_b = w_mem_kv.astype(BF16)
    outs = []
    for b in range(batch):
        xb = x[b]
        mem_b = mem[b]
        cos_t, sa, sb = rope_tables(positions[b])
        for i in range(depth):
            kind, j = i % N_MIXERS, i // N_MIXERS
            h = rmsnorm(xb, norm_mix[i], BF16)
            mem_kv = matmul(rmsnorm(mem_b, norm_mem[i], BF16), w_mem_kv_b, BF16, layer=i)
            if kind == 0:
                qk = matmul_stream(h, a_w_in, j, BF16, 0, 2 * tok_width, rope=(cos_t, sa, sb, tok_width, q_scale))
                vq = matmul_stream(h, a_w_in, j, BF16, 2 * tok_width, tok_width + mem_width)
                mix = mem_attention(vq, tok_width, mem_kv, d_model, tok_width)
                lam_init = 0.8 - 0.6 * math.exp(-0.3 * i)
                mix = diff_attention(qk, vq, a_lambda[j], a_subln[j], lam_init, diff_heads, mix)
            elif kind == 1:
                w_t = jnp.swapaxes(b_w_in, 1, 2)
                q = matmul_stream(h, w_t, j, F32, 0, tok_width, w_t=True)
                kv = matmul_stream(h, w_t, j, BF16, tok_width, 2 * tok_width, w_t=True)
                w_tail = w_t[j, 3 * tok_width:].T.astype(BF16)
                logits = matmul(h, _pad_cols(w_tail[:, :fox_heads], LANES), F32)
                mem_q = matmul(h, w_tail[:, fox_heads:], BF16)
                mix = mem_attention(mem_q, 0, mem_kv, d_model, tok_width)
                bias = _pad_cols(b_forget_bias[j].reshape(1, fox_heads).astype(F32), LANES)
                cum = fox_cum(logits, bias)
                negcum = (-cum[:, :fox_heads]).T.reshape(fox_heads, 1, seq)
                mix = fox_attention(q, kv, negcum, fox_heads, mix)
            else:
                n_zx = tok_width + conv_ch
                w_t = jnp.swapaxes(c_w_in, 1, 2)
                zx = matmul_stream(h, w_t, j, F32, 0, n_zx, w_t=True)
                w_tail = w_t[j, n_zx:].T.astype(BF16)
                dt_c = matmul(h, _pad_cols(w_tail[:, :ssd_heads], LANES), F32)
                mem_q = matmul(h, w_tail[:, ssd_heads:], BF16)
                mix = mem_attention(mem_q, 0, mem_kv, d_model, tok_width)
                xbc = dwconv_silu(zx, tok_width, conv_ch, _conv_rows(c_conv_w[j], c_conv_b[j]), SSD_CONV)
                rep = lambda a: jnp.repeat(a.astype(F32), SSD_HEAD_DIM, axis=-1)
                pad_h = SUBLANES - heads_per_group
                by_group = lambda a: jnp.pad(a.astype(F32).reshape(SSD_GROUPS, heads_per_group, -1),
                                             ((0, 0), (0, pad_h), (0, 0)))
                dt_r = by_group(dt_c[:, :ssd_heads].T)
                mix = ssd_scan(zx, xbc, dt_c, dt_r,
                               rep(c_dt_bias[j].reshape(1, -1)), rep(c_a_log[j].reshape(1, -1)),
                               rep(c_d_skip[j].reshape(1, -1)),
                               by_group(c_dt_bias[j].reshape(-1, 1)), by_group(c_a_log[j].reshape(-1, 1)),
                               c_norm_gate[j].reshape(1, -1).astype(F32), tok_width, mix)
            xb = matmul(mix, w_out_b, F32, res=xb, layer=i)

            hf = rmsnorm(xb, norm_ffn[i], BF16)
            hidden = ffn_up_stream(hf, w_up, i, _conv_rows(conv_ffn_w[i], conv_ffn_b[i]), d_ff, ffp)
            xb = matmul(hidden, w_down_b, F32, res=xb, bk=DOWN_BK, layer=i)
        outs.append(rmsnorm(xb, final_norm, x.dtype))
    return outs[0].reshape(x.shape) if batch == 1 else jnp.stack(outs, axis=0)
```

```python
import functools
import math

import jax
import jax.numpy as jnp
from jax import lax
from jax.experimental import pallas as pl
from jax.experimental.pallas import tpu as pltpu

F32 = jnp.float32
BF16 = jnp.bfloat16

HEAD_DIM = 128
MEM_HEADS = 4
SSD_HEAD_DIM = 64
SSD_GROUPS = 8
SSD_STATE = 128
SSD_CONV = 4
SSD_CHUNK = 128
ROPE_THETA = 500000.0
ROPE_DIM = HEAD_DIM // 4
FFN_CONV = 3
NORM_EPS = 1e-6
N_MIXERS = 3
LOG2E = math.log2(math.e)

LANES = 128
SUBLANES = 8
VMEM_LIMIT_CAP = 56 * 1024 * 1024

MM_BM = 1024
MM_BN = 1024
FFN_BN = 512
DOWN_BK = 2816
DIFF_TQ = 1024
FOX_TQ = 2048
ATT_TK = 512
ATT_SUB = 256
MEM_TQ = 512
ROW_BLK = 256
CUM_BLK = 256


_NT = (((1,), (1,)), ((), ()))


def _vmem(nbytes):
    return int(min(VMEM_LIMIT_CAP, max(16 * 1024 * 1024, nbytes)))


def _params(nbytes, n_grid):
    return pltpu.CompilerParams(
        dimension_semantics=("arbitrary",) * n_grid, vmem_limit_bytes=_vmem(nbytes))


def _softplus(x):
    return jnp.maximum(x, 0.0) + jnp.log1p(jnp.exp(-jnp.abs(x)))


def _silu(x):
    return x / (1.0 + jnp.exp(-x))


def _split3(x):
    hi = x.astype(BF16)
    r = x - hi.astype(F32)
    mid = r.astype(BF16)
    lo = (r - mid.astype(F32)).astype(BF16)
    return hi, mid, lo


def _tri_lower(n):
    r = lax.broadcasted_iota(jnp.int32, (n, n), 0)
    c = lax.broadcasted_iota(jnp.int32, (n, n), 1)
    return jnp.where(c <= r, 1.0, 0.0).astype(BF16)


def _cumsum_rows(tri, x):
    hi, mid, lo = _split3(x)
    d = lambda a: jnp.dot(tri, a, preferred_element_type=F32)
    return d(hi) + d(mid) + d(lo)


def _cumsum_lanes(x, tri_t):
    hi, mid, lo = _split3(x)
    d = lambda a: jnp.dot(a, tri_t, preferred_element_type=F32)
    return d(hi) + d(mid) + d(lo)


def _rmsnorm_kernel(x_ref, g_ref, o_ref):
    x = x_ref[...].astype(F32)
    ms = jnp.mean(x * x, axis=-1, keepdims=True)
    o_ref[...] = (x * lax.rsqrt(ms + NORM_EPS) * g_ref[...]).astype(o_ref.dtype)


def rmsnorm(x, g, out_dtype):
    m, d = x.shape
    rb = min(ROW_BLK, m)
    return pl.pallas_call(
        _rmsnorm_kernel,
        grid=(m // rb,),
        in_specs=[pl.BlockSpec((rb, d), lambda i: (i, 0)),
                  pl.BlockSpec((1, d), lambda i: (0, 0))],
        out_specs=pl.BlockSpec((rb, d), lambda i: (i, 0)),
        out_shape=jax.ShapeDtypeStruct((m, d), out_dtype),
        compiler_params=_params(6 * rb * d * 4, 1),
        name="rmsnorm",
    )(x, g.reshape(1, d).astype(F32))


def _mm_kernel(*refs, nk, bk, k_true, has_res):
    if has_res:
        x_ref, w_ref, r_ref, o_ref = refs[:4]
        scratch = refs[4:]
    else:
        x_ref, w_ref, o_ref = refs[:3]
        r_ref = None
        scratch = refs[3:]
    def product(last):
        w = w_ref[...]
        if last and k_true is not None:
            row = lax.broadcasted_iota(jnp.int32, w.shape, 0)
            w = jnp.where(row < k_true - (nk - 1) * bk, w, jnp.zeros_like(w))
        return jnp.dot(x_ref[...], w, preferred_element_type=F32)

    if nk == 1:
        part = product(True)
        if has_res:
            part = part + r_ref[...]
        o_ref[...] = part.astype(o_ref.dtype)
        return
    acc_ref = scratch[0]
    k = pl.program_id(2)

    @pl.when(k == 0)
    def _():
        acc_ref[...] = product(False)

    @pl.when(jnp.logical_and(k > 0, k < nk - 1))
    def _():
        acc_ref[...] = acc_ref[...] + product(False)

    @pl.when(k == nk - 1)
    def _():
        tot = acc_ref[...] + product(True)
        if has_res:
            tot = tot + r_ref[...]
        o_ref[...] = tot.astype(o_ref.dtype)


def matmul(x, w, out_dtype, col0=0, ncols=None, res=None, bk=None, layer=None):
    m, kdim = x.shape
    w_rows, w_cols = w.shape[-2:]
    n = w_cols - col0 if ncols is None else ncols
    bm = min(MM_BM, m)
    bn = min(MM_BN, n)
    bk = kdim if bk is None else bk
    nk = kdim // bk
    assert m % bm == 0 and n % bn == 0 and kdim % bk == 0 and col0 % bn == 0
    k_true = None if w_rows == kdim else w_rows
    assert w_rows <= kdim and w_rows > (nk - 1) * bk
    joff = col0 // bn
    if layer is None:
        w_spec = pl.BlockSpec((bk, bn), lambda j, i, k: (k, j + joff))
    else:
        w_spec = pl.BlockSpec((None, bk, bn), lambda j, i, k: (layer, k, j + joff))
    in_specs = [pl.BlockSpec((bm, bk), lambda j, i, k: (i, k)), w_spec]
    args = [x, w]
    if res is not None:
        in_specs.append(pl.BlockSpec((bm, bn), lambda j, i, k: (i, j)))
        args.append(res)
    osz = jnp.dtype(out_dtype).itemsize
    nbytes = 2 * (bm * bk * 2 + bk * bn * 2 + bm * bn * osz) + 2 * bm * bn * 4
    if res is not None:
        nbytes += 2 * bm * bn * 4
    scratch = []
    if nk > 1:
        scratch.append(pltpu.VMEM((bm, bn), F32))
        nbytes += bm * bn * 4
    return pl.pallas_call(
        functools.partial(_mm_kernel, nk=nk, bk=bk, k_true=k_true, has_res=res is not None),
        grid=(n // bn, m // bm, nk),
        in_specs=in_specs,
        out_specs=pl.BlockSpec((bm, bn), lambda j, i, k: (i, j)),
        out_shape=jax.ShapeDtypeStruct((m, n), out_dtype),
        scratch_shapes=scratch,
        compiler_params=_params(nbytes + (4 << 20), 3),
        name="matmul",
    )(*args)


def _mm_stream_kernel(*refs, layer, col0, kc, nchunks, rope, w_t):
    x_ref, w_hbm = refs[:2]
    if rope is None:
        o_ref, wbf, stage, sem = refs[2:]
    else:
        cos_ref, sa_ref, sb_ref, o_ref, wbf, stage, sem = refs[2:]
    j = pl.program_id(0)
    i = pl.program_id(1)
    nj = pl.num_programs(0)
    slot = lax.rem(j, 2)
    bn = o_ref.shape[1]

    def slab_copy(jb, c):
        rows = pl.ds(pl.multiple_of(c * kc, kc), kc)
        cols = pl.ds(pl.multiple_of(col0 + jb * bn, SUBLANES if w_t else LANES), bn)
        src = w_hbm.at[layer, cols, rows] if w_t else w_hbm.at[layer, rows, cols]
        return pltpu.make_async_copy(src, stage, sem.at[0])

    def land(c, dst):
        rows = pl.ds(pl.multiple_of(c * kc, kc), kc)
        if w_t:
            wbf[dst, :, rows] = stage[...].astype(BF16)
        else:
            wbf[dst, rows, :] = stage[...].astype(BF16)

    @pl.when(jnp.logical_and(j == 0, i == 0))
    def _():
        for c in range(nchunks):
            cp = slab_copy(0, c)
            cp.start()
            cp.wait()
            land(c, 0)

    more = j + 1 < nj

    @pl.when(more)
    def _():
        slab_copy(j + 1, i).start()

    if w_t:
        acc = lax.dot_general(x_ref[...], wbf[slot], _NT, preferred_element_type=F32)
    else:
        acc = jnp.dot(x_ref[...], wbf[slot], preferred_element_type=F32)
    if rope is None:
        o_ref[...] = acc.astype(o_ref.dtype)
    else:
        q_blocks, q_scale = rope
        cos = cos_ref[...]
        sa = sa_ref[...]
        sb = sb_ref[...]
        half = ROPE_DIM // 2
        scale = jnp.where(j < q_blocks, q_scale, 1.0)
        for gi in range(bn // LANES):
            t = acc[:, gi * LANES:(gi + 1) * LANES]
            up = pltpu.roll(t, LANES - half, 1)
            dn = pltpu.roll(t, half, 1)
            o_ref[:, gi * LANES:(gi + 1) * LANES] = ((t * cos + up * sa + dn * sb) * scale).astype(o_ref.dtype)

    @pl.when(more)
    def _():
        slab_copy(j + 1, i).wait()
        land(i, 1 - slot)


def matmul_stream(x, w, layer, out_dtype, col0, ncols, rope=None, w_t=False):
    m, kdim = x.shape
    bm = min(MM_BM, m)
    bn = min(MM_BN, ncols)
    nm = m // bm
    assert m % bm == 0 and ncols % bn == 0 and kdim % nm == 0 and w.shape[2 if w_t else 1] == kdim
    assert col0 % (SUBLANES if w_t else LANES) == 0
    kc = kdim // nm
    in_specs = [pl.BlockSpec((bm, kdim), lambda j, i: (i, 0)), pl.BlockSpec(memory_space=pl.ANY)]
    args = [x, w]
    rope_static = None
    if rope is not None:
        cos, sa, sb, q_cols, q_scale = rope
        assert q_cols % bn == 0
        tab = pl.BlockSpec((bm, LANES), lambda j, i: (i, 0))
        in_specs += [tab, tab, tab]
        args += [cos, sa, sb]
        rope_static = (q_cols // bn, q_scale)
    osz = jnp.dtype(out_dtype).itemsize
    nbytes = 2 * (bm * kdim * 2 + bm * bn * osz) + 2 * kdim * bn * 2 + kc * bn * 4 + 3 * bm * bn * 4
    return pl.pallas_call(
        functools.partial(_mm_stream_kernel, layer=layer, col0=col0, kc=kc, nchunks=nm, rope=rope_static, w_t=w_t),
        grid=(ncols // bn, nm),
        in_specs=in_specs,
        out_specs=pl.BlockSpec((bm, bn), lambda j, i: (i, j)),
        out_shape=jax.ShapeDtypeStruct((m, ncols), out_dtype),
        scratch_shapes=[pltpu.VMEM((2, bn, kdim) if w_t else (2, kdim, bn), BF16),
                        pltpu.VMEM((bn, kc) if w_t else (kc, bn), F32),
                        pltpu.SemaphoreType.DMA((1,))],
        compiler_params=_params(nbytes + (2 << 20), 2),
        name="matmul_stream",
    )(*args)


def _ffn_up_stream_kernel(x_ref, w_hbm, cg_ref, cv0_ref, cv1_ref, o_ref, tail_g, tail_v, wg_b, wv_b,
                           st_g, st_v, sem, *, layer, bm, bn, d_ff, kc, nchunks):
    j = pl.program_id(0)
    i = pl.program_id(1)
    nj = pl.num_programs(0)
    slot = lax.rem(j, 2)
    hb = bn // 2

    def slab_copies(jb, c):
        rows = pl.ds(pl.multiple_of(c * kc, kc), kc)
        g0 = pl.multiple_of(jb * bn, LANES)
        v0 = pl.multiple_of(d_ff + jb * bn, LANES)
        v1 = pl.multiple_of(jnp.minimum(d_ff + jb * bn + hb, 2 * d_ff - hb), LANES)
        return (pltpu.make_async_copy(w_hbm.at[layer, rows, pl.ds(g0, bn)], st_g, sem.at[0]),
                pltpu.make_async_copy(w_hbm.at[layer, rows, pl.ds(v0, hb)], st_v.at[:, pl.ds(0, hb)], sem.at[1]),
                pltpu.make_async_copy(w_hbm.at[layer, rows, pl.ds(v1, hb)], st_v.at[:, pl.ds(hb, hb)], sem.at[2]))

    def land(c, dst):
        rows = pl.ds(pl.multiple_of(c * kc, kc), kc)
        wg_b[dst, rows, :] = st_g[...].astype(BF16)
        wv_b[dst, rows, :] = st_v[...].astype(BF16)

    @pl.when(jnp.logical_and(j == 0, i == 0))
    def _():
        for c in range(nchunks):
            cps = slab_copies(0, c)
            for cp in cps:
                cp.start()
            for cp in cps:
                cp.wait()
            land(c, 0)

    more = j + 1 < nj

    @pl.when(more)
    def _():
        for cp in slab_copies(j + 1, i):
            cp.start()

    @pl.when(i == 0)
    def _():
        tail_g[...] = jnp.zeros_like(tail_g)
        tail_v[...] = jnp.zeros_like(tail_v)

    x = x_ref[...]

    def conv(u, c, tail):
        ext = jnp.concatenate([tail[...], u], axis=0)
        u1 = pltpu.roll(ext, 1, 0)[SUBLANES:]
        u2 = pltpu.roll(ext, 2, 0)[SUBLANES:]
        tail[...] = u[bm - SUBLANES:]
        return c[3:4] + c[2:3] * u + c[1:2] * u1 + c[0:1] * u2

    ug = jnp.dot(x, wg_b[slot], preferred_element_type=F32)
    uv = jnp.dot(x, wv_b[slot], preferred_element_type=F32)
    g = conv(ug, cg_ref[...], tail_g)
    v = conv(uv, jnp.concatenate([cv0_ref[...], cv1_ref[...]], axis=1), tail_v)
    col = lax.broadcasted_iota(jnp.int32, (bm, bn), 1) + j * bn
    o_ref[...] = jnp.where(col < d_ff, _silu(g) * v, 0.0).astype(o_ref.dtype)

    @pl.when(more)
    def _():
        for cp in slab_copies(j + 1, i):
            cp.wait()
        land(i, 1 - slot)


def ffn_up_stream(h, w_up, layer, c8, d_ff, ffp):
    m, d = h.shape
    bm = min(MM_BM, m)
    bn = FFN_BN
    hb = bn // 2
    nm = m // bm
    assert ffp % bn == 0 and d_ff % hb == 0 and d % nm == 0 and ffp <= 2 * d_ff
    kc = d // nm
    v0 = d_ff // hb
    vlast = (2 * d_ff) // hb - 1
    vmap0 = lambda j, i: (0, jnp.minimum(v0 + 2 * j, vlast))
    vmap1 = lambda j, i: (0, jnp.minimum(v0 + 2 * j + 1, vlast))
    nbytes = (2 * (bm * d * 2 + bm * bn * 2) + 2 * 2 * d * bn * 2 + 2 * kc * bn * 4 + 12 * bm * bn * 4)
    return pl.pallas_call(
        functools.partial(_ffn_up_stream_kernel, layer=layer, bm=bm, bn=bn, d_ff=d_ff, kc=kc, nchunks=nm),
        grid=(ffp // bn, nm),
        in_specs=[pl.BlockSpec((bm, d), lambda j, i: (i, 0)),
                  pl.BlockSpec(memory_space=pl.ANY),
                  pl.BlockSpec((SUBLANES, bn), lambda j, i: (0, j)),
                  pl.BlockSpec((SUBLANES, hb), vmap0),
                  pl.BlockSpec((SUBLANES, hb), vmap1)],
        out_specs=pl.BlockSpec((bm, bn), lambda j, i: (i, j)),
        out_shape=jax.ShapeDtypeStruct((m, ffp), BF16),
        scratch_shapes=[pltpu.VMEM((SUBLANES, bn), F32), pltpu.VMEM((SUBLANES, bn), F32),
                        pltpu.VMEM((2, d, bn), BF16), pltpu.VMEM((2, d, bn), BF16),
                        pltpu.VMEM((kc, bn), F32), pltpu.VMEM((kc, bn), F32),
                        pltpu.SemaphoreType.DMA((3,))],
        compiler_params=_params(nbytes, 2),
        name="ffn_up_stream",
    )(h, w_up, c8, c8, c8)


def _dwconv_silu_kernel(x_ref, w_ref, o_ref, tail, *, taps, rb):
    @pl.when(pl.program_id(1) == 0)
    def _():
        tail[...] = jnp.zeros_like(tail)

    x = x_ref[...]
    ext = jnp.concatenate([tail[...], x], axis=0)
    w = w_ref[...]
    out = w[taps:taps + 1] + w[taps - 1:taps] * x
    for k in range(taps - 1):
        d = taps - 1 - k
        out = out + pltpu.roll(ext, d, 0)[SUBLANES:] * w[k:k + 1]
    tail[...] = x[rb - SUBLANES:]
    o_ref[...] = _silu(out).astype(o_ref.dtype)


def dwconv_silu(x, col0, ncols, w8, taps):
    m = x.shape[0]
    rb = min(ROW_BLK, m)
    cb = 1024
    assert col0 % cb == 0 and ncols % cb == 0
    off = col0 // cb
    return pl.pallas_call(
        functools.partial(_dwconv_silu_kernel, taps=taps, rb=rb),
        grid=(ncols // cb, m // rb),
        in_specs=[pl.BlockSpec((rb, cb), lambda j, i: (i, j + off)),
                  pl.BlockSpec((SUBLANES, cb), lambda j, i: (0, j))],
        out_specs=pl.BlockSpec((rb, cb), lambda j, i: (i, j)),
        out_shape=jax.ShapeDtypeStruct((m, ncols), F32),
        scratch_shapes=[pltpu.VMEM((SUBLANES, cb), F32)],
        compiler_params=_params(16 * rb * cb * 4, 2),
        name="dwconv_silu",
    )(x, w8)


def rope_tables(positions):
    half = ROPE_DIM // 2
    inv_freq = jnp.power(jnp.float32(ROPE_THETA), -jnp.arange(half, dtype=F32) * (2.0 / ROPE_DIM))
    ang = positions.astype(F32)[:, None] * inv_freq
    cos, sin = jnp.cos(ang), jnp.sin(ang)
    s = positions.shape[0]
    pad = LANES - ROPE_DIM
    cos_t = jnp.concatenate([cos, cos, jnp.ones((s, pad), F32)], axis=1)
    sa = jnp.concatenate([-sin, jnp.zeros((s, LANES - half), F32)], axis=1)
    sb = jnp.concatenate([jnp.zeros((s, half), F32), sin, jnp.zeros((s, pad), F32)], axis=1)
    return cos_t, sa, sb


def _flash_softmax(s, m_ref, l_ref, rows):
    tk = s.shape[1]
    sc = [s[:, c * LANES:(c + 1) * LANES] for c in range(tk // LANES)]
    m_prev = m_ref[rows, :]
    m_new = jnp.maximum(m_prev, jnp.max(functools.reduce(jnp.maximum, sc), axis=-1, keepdims=True))
    alpha = jnp.exp2(m_prev - m_new)
    pc = [jnp.exp2(c - m_new) for c in sc]
    l_ref[rows, :] = alpha * l_ref[rows, :] + functools.reduce(jnp.add, pc)
    m_ref[rows, :] = m_new
    return jnp.concatenate([c.astype(BF16) for c in pc], axis=1), alpha


def _flash_accumulate(p, alpha, v, acc_ref, rows):
    pv = jnp.dot(p, v, preferred_element_type=F32)
    a = jnp.concatenate([alpha] * (pv.shape[1] // LANES), axis=1) if pv.shape[1] > LANES else alpha
    acc_ref[rows, :] = a * acc_ref[rows, :] + pv


def _flash_block(chains, v):
    pa = [_flash_softmax(s, m_ref, l_ref, rows) for s, m_ref, l_ref, _, rows in chains]
    for (p, alpha), (_, _, _, acc_ref, rows) in zip(pa, chains):
        _flash_accumulate(p, alpha, v, acc_ref, rows)


def _flash_init(m_ref, l_ref, acc_ref):
    m_ref[...] = jnp.full(m_ref.shape, -jnp.inf, F32)
    l_ref[...] = jnp.zeros(l_ref.shape, F32)
    acc_ref[...] = jnp.zeros(acc_ref.shape, F32)


def _flash_result(l_ref, acc_ref):
    return acc_ref[...] / jnp.sum(l_ref[...], axis=-1, keepdims=True)


def _causal_mask(tr, tk, delta):
    r = lax.broadcasted_iota(jnp.int32, (tr, tk), 0)
    c = lax.broadcasted_iota(jnp.int32, (tr, tk), 1)
    return c <= r + delta


def _diag_plan(tq, tk, tr):
    plan = []
    for co in range(0, tq, tk):
        subs = []
        for r0 in range(0, tq, tr):
            if r0 + tr - 1 < co:
                continue
            subs.append((r0, None if co + tk - 1 <= r0 else r0 - co))
        plan.append(subs)
    return plan


def _causal_sweep(qi, tq, tk, tr, block):
    full = [(r0, None) for r0 in range(0, tq, tr)]
    per_q = tq // tk

    def body(j, carry):
        block(j, full)
        return carry

    lax.fori_loop(0, qi * per_q, body, 0)
    for b, subs in enumerate(_diag_plan(tq, tk, tr)):
        block(qi * per_q + b, subs)


def _fox_kernel(q_ref, k_ref, v_ref, b_ref, mix_ref, o_ref, q_scr, m_ref, l_ref, acc_ref, *, tq, tk, tr, q_scale):
    del mix_ref
    q_scr[...] = (q_ref[...] * q_scale).astype(BF16)
    _flash_init(m_ref, l_ref, acc_ref)

    def block(j, subs):
        off = pl.multiple_of(j * tk, tk)
        k = k_ref[pl.ds(off, tk), :]
        v = v_ref[pl.ds(off, tk), :]
        bias = b_ref[:, pl.ds(off, tk)] * LOG2E
        chains = []
        for r0, delta in subs:
            rows = pl.ds(r0, tr)
            s = lax.dot_general(q_scr[rows, :], k, _NT, preferred_element_type=F32) + bias
            if delta is not None:
                s = jnp.where(_causal_mask(tr, tk, delta), s, -jnp.inf)
            chains.append((s, m_ref, l_ref, acc_ref, rows))
        _flash_block(chains, v)

    _causal_sweep(pl.program_id(1), tq, tk, tr, block)
    o_ref[...] = _flash_result(l_ref, acc_ref).astype(o_ref.dtype)


def _att_tiles(s, tq_max):
    tq = min(tq_max, s)
    tk = min(ATT_TK, tq)
    tr = min(ATT_SUB, tk)
    assert s % tq == 0 and tq % tk == 0 and tk % tr == 0
    return tq, tk, tr


def fox_attention(q, kv, negcum, n_heads, mix):
    s = q.shape[0]
    tq, tk, tr = _att_tiles(s, FOX_TQ)
    hd = HEAD_DIM
    nbytes = 2 * (2 * s * hd * 2 + tq * hd * 4 + tq * hd * 2 + s * 4) + 12 * tq * tk * 4
    return pl.pallas_call(
        functools.partial(_fox_kernel, tq=tq, tk=tk, tr=tr, q_scale=hd ** -0.5 * LOG2E),
        grid=(n_heads, s // tq),
        in_specs=[pl.BlockSpec((tq, hd), lambda h, i: (i, h)),
                  pl.BlockSpec((s, hd), lambda h, i: (0, h)),
                  pl.BlockSpec((s, hd), lambda h, i: (0, n_heads + h)),
                  pl.BlockSpec((None, 1, s), lambda h, i: (h, 0, 0)),
                  pl.BlockSpec(memory_space=pl.ANY)],
        out_specs=pl.BlockSpec((tq, hd), lambda h, i: (i, h)),
        out_shape=jax.ShapeDtypeStruct(mix.shape, mix.dtype),
        input_output_aliases={4: 0},
        scratch_shapes=[pltpu.VMEM((tq, hd), BF16), pltpu.VMEM((tq, LANES), F32),
                        pltpu.VMEM((tq, LANES), F32), pltpu.VMEM((tq, hd), F32)],
        compiler_params=_params(nbytes, 2),
        name="fox_attention",
    )(q, kv, kv, negcum, mix)


def _diff_kernel(q_ref, k_ref, v_ref, lam_ref, g_ref, mix_ref, o_ref, m0, l0, a0, m1, l1, a1,
                 *, tq, tk, tr, lam_init):
    del mix_ref
    hd = HEAD_DIM
    _flash_init(m0, l0, a0)
    _flash_init(m1, l1, a1)

    def block(j, subs):
        off = pl.multiple_of(j * tk, tk)
        v = v_ref[pl.ds(off, tk), :]
        chains = []
        for lo, st in ((0, (m0, l0, a0)), (hd, (m1, l1, a1))):
            k = k_ref[pl.ds(off, tk), lo:lo + hd]
            for r0, delta in subs:
                rows = pl.ds(r0, tr)
                s = lax.dot_general(q_ref[rows, lo:lo + hd], k, _NT, preferred_element_type=F32)
                if delta is not None:
                    s = jnp.where(_causal_mask(tr, tk, delta), s, -jnp.inf)
                chains.append((s,) + st + (rows,))
        _flash_block(chains, v)

    _causal_sweep(pl.program_id(1), tq, tk, tr, block)
    lv = lam_ref[...]
    lam = (jnp.exp(jnp.sum(lv[0:1] * lv[1:2], axis=-1, keepdims=True))
           - jnp.exp(jnp.sum(lv[2:3] * lv[3:4], axis=-1, keepdims=True)) + lam_init)
    o = _flash_result(l0, a0) - lam * _flash_result(l1, a1)
    o = o * lax.rsqrt(jnp.mean(o * o, axis=-1, keepdims=True) + NORM_EPS) * g_ref[...]
    o_ref[...] = (o * (1.0 - lam_init)).astype(o_ref.dtype)


def diff_attention(qk, v, lam_vecs, subln_g, lam_init, n_heads, mix):
    s = qk.shape[0]
    tq, tk, tr = _att_tiles(s, DIFF_TQ)
    t = tq
    w = 2 * HEAD_DIM
    nbytes = 2 * (2 * s * w * 2 + 2 * tq * w * 2) + 24 * tq * tk * 4
    stats = lambda: [pltpu.VMEM((t, LANES), F32), pltpu.VMEM((t, LANES), F32), pltpu.VMEM((t, w), F32)]
    return pl.pallas_call(
        functools.partial(_diff_kernel, tq=tq, tk=tk, tr=tr, lam_init=lam_init),
        grid=(n_heads, s // t),
        in_specs=[pl.BlockSpec((t, w), lambda h, i: (i, h)),
                  pl.BlockSpec((s, w), lambda h, i: (0, n_heads + h)),
                  pl.BlockSpec((s, w), lambda h, i: (0, h)),
                  pl.BlockSpec((4, HEAD_DIM), lambda h, i: (0, 0)),
                  pl.BlockSpec((1, w), lambda h, i: (0, 0)),
                  pl.BlockSpec(memory_space=pl.ANY)],
        out_specs=pl.BlockSpec((t, w), lambda h, i: (i, h)),
        out_shape=jax.ShapeDtypeStruct(mix.shape, mix.dtype),
        input_output_aliases={5: 0},
        scratch_shapes=stats() + stats(),
        compiler_params=_params(nbytes, 2),
        name="diff_attention",
    )(qk, qk, v, lam_vecs.astype(F32), subln_g.reshape(1, w).astype(F32), mix)


def _mem_attn_kernel(q_ref, kv_ref, o_ref, *, scale, heads, hd, ctx_col0):
    o_ref[:, :ctx_col0] = jnp.zeros((o_ref.shape[0], ctx_col0), o_ref.dtype)
    for h in range(heads):
        k = kv_ref[:, h * hd:(h + 1) * hd]
        v = kv_ref[:, (heads + h) * hd:(heads + h + 1) * hd]
        s = lax.dot_general(q_ref[:, h * hd:(h + 1) * hd], k, _NT, preferred_element_type=F32) * scale
        p = jnp.exp(s - jnp.max(s, axis=-1, keepdims=True))
        l = jnp.sum(p, axis=-1, keepdims=True)
        o = jnp.dot(p.astype(BF16), v, preferred_element_type=F32)
        o_ref[:, ctx_col0 + h * hd:ctx_col0 + (h + 1) * hd] = (o / l).astype(o_ref.dtype)


def mem_attention(qarr, q_col0, mem_kv, out_width, ctx_col0):
    s = qarr.shape[0]
    mem_width = mem_kv.shape[1] // 2
    hd = mem_width // MEM_HEADS
    mt = mem_kv.shape[0]
    tq = min(MEM_TQ, s)
    assert q_col0 % mem_width == 0 and ctx_col0 + mem_width == out_width
    q_blk = q_col0 // mem_width
    return pl.pallas_call(
        functools.partial(_mem_attn_kernel, scale=hd ** -0.5, heads=MEM_HEADS, hd=hd, ctx_col0=ctx_col0),
        grid=(s // tq,),
        in_specs=[pl.BlockSpec((tq, mem_width), lambda i: (i, q_blk)),
                  pl.BlockSpec((mt, 2 * mem_width), lambda i: (0, 0))],
        out_specs=pl.BlockSpec((tq, out_width), lambda i: (i, 0)),
        out_shape=jax.ShapeDtypeStruct((s, out_width), BF16),
        compiler_params=_params(4 * tq * out_width * 2 + 32 * tq * hd * 4, 1),
        name="mem_attention",
    )(qarr, mem_kv)


def _fox_cum_kernel(x_ref, b_ref, o_ref, carry, *, cb):
    @pl.when(pl.program_id(0) == 0)
    def _():
        carry[...] = jnp.zeros_like(carry)

    x = x_ref[...] + b_ref[...]
    log_f = jnp.minimum(x, 0.0) - jnp.log1p(jnp.exp(-jnp.abs(x)))
    cs = _cumsum_rows(_tri_lower(cb), log_f) + carry[...]
    o_ref[...] = cs
    carry[...] = cs[cb - 1:cb, :]


def fox_cum(logits, bias):
    s = logits.shape[0]
    cb = min(CUM_BLK, s)
    return pl.pallas_call(
        functools.partial(_fox_cum_kernel, cb=cb),
        grid=(s // cb,),
        in_specs=[pl.BlockSpec((cb, LANES), lambda i: (i, 0)),
                  pl.BlockSpec((1, LANES), lambda i: (0, 0))],
        out_specs=pl.BlockSpec((cb, LANES), lambda i: (i, 0)),
        out_shape=jax.ShapeDtypeStruct((s, LANES), F32),
        scratch_shapes=[pltpu.VMEM((1, LANES), F32)],
        compiler_params=_params(16 << 20, 1),
        name="fox_cum",
    )(logits, bias)


def _ssd_kernel(z_ref, xs_ref, b_ref, c_ref, dtc_ref, dtr_ref, bias_e_ref, alog_e_ref, dskip_e_ref,
                bias_r_ref, alog_r_ref, ng_ref, mix_ref, o_ref, h_ref, *, heads, hdim, chunk):
    del mix_ref
    @pl.when(pl.program_id(1) == 0)
    def _():
        h_ref[...] = jnp.zeros_like(h_ref)

    L = chunk
    width = heads * hdim
    xs = xs_ref[...]
    bm = b_ref[...]
    cm = c_ref[...]
    tri = _tri_lower(L)

    lane = lax.broadcasted_iota(jnp.int32, (LANES, width), 0)
    head = lax.broadcasted_iota(jnp.int32, (LANES, width), 1) // hdim + pl.program_id(0) * heads
    spread = jnp.where(lane == head, 1.0, 0.0).astype(BF16)
    dt_raw = sum(jnp.dot(part, spread, preferred_element_type=F32) for part in _split3(dtc_ref[...]))
    dt_e = _softplus(dt_raw + bias_e_ref[...])
    a_e = dt_e * (-jnp.exp(alog_e_ref[...]))
    acs_e = _cumsum_rows(tri, a_e)
    acs_last = acs_e[L - 1:L, :]
    dt_r = _softplus(dtr_ref[...] + bias_r_ref[...])
    a_r = dt_r * (-jnp.exp(alog_r_ref[...]))
    acs_r = _cumsum_lanes(a_r, tri.T)

    xd = xs * dt_e
    xd_b = xd.astype(BF16)
    cb = lax.dot_general(cm.astype(BF16), bm.astype(BF16), _NT, preferred_element_type=F32)
    causal = _causal_mask(L, L, 0)
    head_id = lax.broadcasted_iota(jnp.int32, (L, width), 1) // hdim

    y = jnp.zeros((L, width), F32)
    for r in range(heads):
        seg = acs_e[:, r * hdim:r * hdim + 1] - acs_r[r:r + 1, :]
        decay = jnp.exp(jnp.where(causal, seg, -jnp.inf))
        yr = jnp.dot((cb * decay).astype(BF16), xd_b, preferred_element_type=F32)
        y = jnp.where(head_id == r, yr, y)

    h_in = h_ref[...]
    y_off = jnp.dot(cm.astype(BF16), h_in.astype(BF16), preferred_element_type=F32) * jnp.exp(acs_e)
    to_end = jnp.exp(acs_last - acs_e)
    states = jnp.dot(bm.T.astype(BF16), (xd * to_end).astype(BF16), preferred_element_type=F32)
    h_ref[...] = h_in * jnp.exp(acs_last) + states

    y = y + y_off + xs * dskip_e_ref[...]
    g = y * _silu(z_ref[...])
    g = g * lax.rsqrt(jnp.mean(g * g, axis=-1, keepdims=True) + NORM_EPS)
    o_ref[...] = (g * ng_ref[...]).astype(o_ref.dtype)


def ssd_scan(zx, xbc, dt_c, dt_r, bias_e, alog_e, dskip_e, bias_r, alog_r, norm_g, inner, mix):
    s = zx.shape[0]
    L = SSD_CHUNK
    G = SSD_GROUPS
    width = inner // G
    heads = width // SSD_HEAD_DIM
    N = SSD_STATE
    b_off = inner // N
    c_off = b_off + G
    row = lambda g, c: (0, g)
    return pl.pallas_call(
        functools.partial(_ssd_kernel, heads=heads, hdim=SSD_HEAD_DIM, chunk=L),
        grid=(G, s // L),
        in_specs=[pl.BlockSpec((L, width), lambda g, c: (c, g)),
                  pl.BlockSpec((L, width), lambda g, c: (c, g)),
                  pl.BlockSpec((L, N), lambda g, c: (c, b_off + g)),
                  pl.BlockSpec((L, N), lambda g, c: (c, c_off + g)),
                  pl.BlockSpec((L, LANES), lambda g, c: (c, 0)),
                  pl.BlockSpec((None, SUBLANES, L), lambda g, c: (g, 0, c)),
                  pl.BlockSpec((1, width), row),
                  pl.BlockSpec((1, width), row),
                  pl.BlockSpec((1, width), row),
                  pl.BlockSpec((None, SUBLANES, 1), lambda g, c: (g, 0, 0)),
                  pl.BlockSpec((None, SUBLANES, 1), lambda g, c: (g, 0, 0)),
                  pl.BlockSpec((1, width), row),
                  pl.BlockSpec(memory_space=pl.ANY)],
        out_specs=pl.BlockSpec((L, width), lambda g, c: (c, g)),
        out_shape=jax.ShapeDtypeStruct(mix.shape, mix.dtype),
        input_output_aliases={12: 0},
        scratch_shapes=[pltpu.VMEM((N, width), F32)],
        compiler_params=_params(32 << 20, 2),
        name="ssd_scan",
    )(zx, xbc, xbc, xbc, dt_c, dt_r, bias_e, alog_e, dskip_e, bias_r, alog_r, norm_g, mix)


def _conv_rows(w, b):
    taps, c = w.shape
    return jnp.concatenate([w.astype(F32), b.reshape(1, c).astype(F32),
                            jnp.zeros((SUBLANES - taps - 1, c), F32)], axis=0)


def _pad_cols(a, n):
    return jnp.pad(a, ((0, 0), (0, n - a.shape[1])))


def kernel(x, mem, positions, norm_mix, norm_mem, w_mem_kv, w_out, norm_ffn, w_up, conv_ffn_w, conv_ffn_b,
           w_down, a_w_in, a_lambda, a_subln, b_w_in, b_forget_bias, c_w_in, c_conv_w, c_conv_b, c_dt_bias,
           c_a_log, c_d_skip, c_norm_gate, final_norm):
    batch, seq, d_model = x.shape
    depth = norm_mix.shape[0]
    mem_width = w_mem_kv.shape[2] // 2
    tok_width = d_model - mem_width
    d_ff = w_down.shape[1]
    ffp = DOWN_BK * ((d_ff + DOWN_BK - 1) // DOWN_BK)
    diff_heads = tok_width // (2 * HEAD_DIM)
    fox_heads = tok_width // HEAD_DIM
    ssd_heads = tok_width // SSD_HEAD_DIM
    heads_per_group = ssd_heads // SSD_GROUPS
    conv_ch = tok_width + 2 * SSD_GROUPS * SSD_STATE
    q_scale = HEAD_DIM ** -0.5 * LOG2E

    w_out_b = w_out.astype(BF16)
    w_down_b = w_down.astype(BF16)
    w_mem_kv_b = w_mem_kv.astype(BF16)
    outs = []
    for b in range(batch):
        xb = x[b]
        mem_b = mem[b]
        cos_t, sa, sb = rope_tables(positions[b])
        for i in range(depth):
            kind, j = i % N_MIXERS, i // N_MIXERS
            h = rmsnorm(xb, norm_mix[i], BF16)
            mem_kv = matmul(rmsnorm(mem_b, norm_mem[i], BF16), w_mem_kv_b, BF16, layer=i)
            if kind == 0:
                qk = matmul_stream(h, a_w_in, j, BF16, 0, 2 * tok_width, rope=(cos_t, sa, sb, tok_width, q_scale))
                vq = matmul_stream(h, a_w_in, j, BF16, 2 * tok_width, tok_width + mem_width)
                mix = mem_attention(vq, tok_width, mem_kv, d_model, tok_width)
                lam_init = 0.8 - 0.6 * math.exp(-0.3 * i)
                mix = diff_attention(qk, vq, a_lambda[j], a_subln[j], lam_init, diff_heads, mix)
            elif kind == 1:
                w_t = jnp.swapaxes(b_w_in, 1, 2)
                q = matmul_stream(h, w_t, j, F32, 0, tok_width, w_t=True)
                kv = matmul_stream(h, w_t, j, BF16, tok_width, 2 * tok_width, w_t=True)
                logits = matmul_stream(h, w_t, j, F32, 3 * tok_width, LANES, w_t=True)
                mem_q = matmul_stream(h, w_t, j, BF16, 3 * tok_width + fox_heads, mem_width, w_t=True)
                mix = mem_attention(mem_q, 0, mem_kv, d_model, tok_width)
                bias = _pad_cols(b_forget_bias[j].reshape(1, fox_heads).astype(F32), LANES)
                cum = fox_cum(logits, bias)
                negcum = (-cum[:, :fox_heads]).T.reshape(fox_heads, 1, seq)
                mix = fox_attention(q, kv, negcum, fox_heads, mix)
            else:
                n_zx = tok_width + conv_ch
                w_t = jnp.swapaxes(c_w_in, 1, 2)
                zx = matmul_stream(h, w_t, j, F32, 0, n_zx, w_t=True)
                dt_c = matmul_stream(h, w_t, j, F32, n_zx, LANES, w_t=True)
                mem_q = matmul_stream(h, w_t, j, BF16, n_zx + ssd_heads, mem_width, w_t=True)
                mix = mem_attention(mem_q, 0, mem_kv, d_model, tok_width)
                xbc = dwconv_silu(zx, tok_width, conv_ch, _conv_rows(c_conv_w[j], c_conv_b[j]), SSD_CONV)
                rep = lambda a: jnp.repeat(a.astype(F32), SSD_HEAD_DIM, axis=-1)
                pad_h = SUBLANES - heads_per_group
                by_group = lambda a: jnp.pad(a.astype(F32).reshape(SSD_GROUPS, heads_per_group, -1),
                                             ((0, 0), (0, pad_h), (0, 0)))
                dt_r = by_group(dt_c[:, :ssd_heads].T)
                mix = ssd_scan(zx, xbc, dt_c, dt_r,
                               rep(c_dt_bias[j].reshape(1, -1)), rep(c_a_log[j].reshape(1, -1)),
                               rep(c_d_skip[j].reshape(1, -1)),
                               by_group(c_dt_bias[j].reshape(-1, 1)), by_group(c_a_log[j].reshape(-1, 1)),
                               c_norm_gate[j].reshape(1, -1).astype(F32), tok_width, mix)
            xb = matmul(mix, w_out_b, F32, res=xb, layer=i)

            hf = rmsnorm(xb, norm_ffn[i], BF16)
            hidden = ffn_up_stream(hf, w_up, i, _conv_rows(conv_ffn_w[i], conv_ffn_b[i]), d_ff, ffp)
            xb = matmul(hidden, w_down_b, F32, res=xb, bk=DOWN_BK, layer=i)
        outs.append(rmsnorm(xb, final_norm, x.dtype))
    return outs[0].reshape(x.shape) if batch == 1 else jnp.stack(outs, axis=0)
```

```python
import functools
import math

import jax
import jax.numpy as jnp
from jax import lax
from jax.experimental import pallas as pl
from jax.experimental.pallas import tpu as pltpu

F32 = jnp.float32
BF16 = jnp.bfloat16

HEAD_DIM = 128
MEM_HEADS = 4
SSD_HEAD_DIM = 64
SSD_GROUPS = 8
SSD_STATE = 128
SSD_CONV = 4
SSD_CHUNK = 128
ROPE_THETA = 500000.0
ROPE_DIM = HEAD_DIM // 4
FFN_CONV = 3
NORM_EPS = 1e-6
N_MIXERS = 3
LOG2E = math.log2(math.e)

LANES = 128
SUBLANES = 8
VMEM_LIMIT_CAP = 56 * 1024 * 1024

MM_BM = 1024
MM_BN = 1024
FFN_BN = 512
DOWN_BK = 2816
ATT_TQ = 2048
ATT_TK = 512
ATT_SUB = 512
MEM_TQ = 512
ROW_BLK = 256
CUM_BLK = 256


_NT = (((1,), (1,)), ((), ()))


def _vmem(nbytes):
    return int(min(VMEM_LIMIT_CAP, max(16 * 1024 * 1024, nbytes)))


def _params(nbytes, n_grid):
    return pltpu.CompilerParams(
        dimension_semantics=("arbitrary",) * n_grid, vmem_limit_bytes=_vmem(nbytes))


def _softplus(x):
    return jnp.maximum(x, 0.0) + jnp.log1p(jnp.exp(-jnp.abs(x)))


def _silu(x):
    return x / (1.0 + jnp.exp(-x))


def _split3(x):
    hi = x.astype(BF16)
    r = x - hi.astype(F32)
    mid = r.astype(BF16)
    lo = (r - mid.astype(F32)).astype(BF16)
    return hi, mid, lo


def _tri_lower(n):
    r = lax.broadcasted_iota(jnp.int32, (n, n), 0)
    c = lax.broadcasted_iota(jnp.int32, (n, n), 1)
    return jnp.where(c <= r, 1.0, 0.0).astype(BF16)


def _cumsum_rows(tri, x):
    hi, mid, lo = _split3(x)
    d = lambda a: jnp.dot(tri, a, preferred_element_type=F32)
    return d(hi) + d(mid) + d(lo)


def _cumsum_lanes(x, tri_t):
    hi, mid, lo = _split3(x)
    d = lambda a: jnp.dot(a, tri_t, preferred_element_type=F32)
    return d(hi) + d(mid) + d(lo)


def _rmsnorm_kernel(x_ref, g_ref, o_ref):
    x = x_ref[...].astype(F32)
    ms = jnp.mean(x * x, axis=-1, keepdims=True)
    o_ref[...] = (x * lax.rsqrt(ms + NORM_EPS) * g_ref[...]).astype(o_ref.dtype)


def rmsnorm(x, g, out_dtype):
    m, d = x.shape
    rb = min(ROW_BLK, m)
    return pl.pallas_call(
        _rmsnorm_kernel,
        grid=(m // rb,),
        in_specs=[pl.BlockSpec((rb, d), lambda i: (i, 0)),
                  pl.BlockSpec((1, d), lambda i: (0, 0))],
        out_specs=pl.BlockSpec((rb, d), lambda i: (i, 0)),
        out_shape=jax.ShapeDtypeStruct((m, d), out_dtype),
        compiler_params=_params(6 * rb * d * 4, 1),
        name="rmsnorm",
    )(x, g.reshape(1, d).astype(F32))


def _mm_kernel(*refs, nk, bk, k_true, has_res):
    if has_res:
        x_ref, w_ref, r_ref, o_ref = refs[:4]
        scratch = refs[4:]
    else:
        x_ref, w_ref, o_ref = refs[:3]
        r_ref = None
        scratch = refs[3:]
    def product(last):
        w = w_ref[...]
        if last and k_true is not None:
            row = lax.broadcasted_iota(jnp.int32, w.shape, 0)
            w = jnp.where(row < k_true - (nk - 1) * bk, w, jnp.zeros_like(w))
        return jnp.dot(x_ref[...], w, preferred_element_type=F32)

    if nk == 1:
        part = product(True)
        if has_res:
            part = part + r_ref[...]
        o_ref[...] = part.astype(o_ref.dtype)
        return
    acc_ref = scratch[0]
    k = pl.program_id(2)

    @pl.when(k == 0)
    def _():
        acc_ref[...] = product(False)

    @pl.when(jnp.logical_and(k > 0, k < nk - 1))
    def _():
        acc_ref[...] = acc_ref[...] + product(False)

    @pl.when(k == nk - 1)
    def _():
        tot = acc_ref[...] + product(True)
        if has_res:
            tot = tot + r_ref[...]
        o_ref[...] = tot.astype(o_ref.dtype)


def matmul(x, w, out_dtype, col0=0, ncols=None, res=None, bk=None, layer=None):
    m, kdim = x.shape
    w_rows, w_cols = w.shape[-2:]
    n = w_cols - col0 if ncols is None else ncols
    bm = min(MM_BM, m)
    bn = min(MM_BN, n)
    bk = kdim if bk is None else bk
    nk = kdim // bk
    assert m % bm == 0 and n % bn == 0 and kdim % bk == 0 and col0 % bn == 0
    k_true = None if w_rows == kdim else w_rows
    assert w_rows <= kdim and w_rows > (nk - 1) * bk
    joff = col0 // bn
    if layer is None:
        w_spec = pl.BlockSpec((bk, bn), lambda j, i, k: (k, j + joff))
    else:
        w_spec = pl.BlockSpec((None, bk, bn), lambda j, i, k: (layer, k, j + joff))
    in_specs = [pl.BlockSpec((bm, bk), lambda j, i, k: (i, k)), w_spec]
    args = [x, w]
    if res is not None:
        in_specs.append(pl.BlockSpec((bm, bn), lambda j, i, k: (i, j)))
        args.append(res)
    osz = jnp.dtype(out_dtype).itemsize
    nbytes = 2 * (bm * bk * 2 + bk * bn * 2 + bm * bn * osz) + 2 * bm * bn * 4
    if res is not None:
        nbytes += 2 * bm * bn * 4
    scratch = []
    if nk > 1:
        scratch.append(pltpu.VMEM((bm, bn), F32))
        nbytes += bm * bn * 4
    return pl.pallas_call(
        functools.partial(_mm_kernel, nk=nk, bk=bk, k_true=k_true, has_res=res is not None),
        grid=(n // bn, m // bm, nk),
        in_specs=in_specs,
        out_specs=pl.BlockSpec((bm, bn), lambda j, i, k: (i, j)),
        out_shape=jax.ShapeDtypeStruct((m, n), out_dtype),
        scratch_shapes=scratch,
        compiler_params=_params(nbytes + (4 << 20), 3),
        name="matmul",
    )(*args)


def _mm_stream_kernel(*refs, layer, col0, kc, nchunks, rope, w_t):
    x_ref, w_hbm = refs[:2]
    if rope is None:
        o_ref, wbf, stage, sem = refs[2:]
    else:
        cos_ref, sa_ref, sb_ref, o_ref, wbf, stage, sem = refs[2:]
    j = pl.program_id(0)
    i = pl.program_id(1)
    nj = pl.num_programs(0)
    slot = lax.rem(j, 2)
    bn = o_ref.shape[1]

    def slab_copy(jb, c):
        rows = pl.ds(pl.multiple_of(c * kc, kc), kc)
        cols = pl.ds(pl.multiple_of(col0 + jb * bn, SUBLANES if w_t else LANES), bn)
        src = w_hbm.at[layer, cols, rows] if w_t else w_hbm.at[layer, rows, cols]
        return pltpu.make_async_copy(src, stage, sem.at[0])

    def land(c, dst):
        rows = pl.ds(pl.multiple_of(c * kc, kc), kc)
        if w_t:
            wbf[dst, :, rows] = stage[...].astype(BF16)
        else:
            wbf[dst, rows, :] = stage[...].astype(BF16)

    @pl.when(jnp.logical_and(j == 0, i == 0))
    def _():
        for c in range(nchunks):
            cp = slab_copy(0, c)
            cp.start()
            cp.wait()
            land(c, 0)

    more = j + 1 < nj

    @pl.when(more)
    def _():
        slab_copy(j + 1, i).start()

    if w_t:
        acc = lax.dot_general(x_ref[...], wbf[slot], _NT, preferred_element_type=F32)
    else:
        acc = jnp.dot(x_ref[...], wbf[slot], preferred_element_type=F32)
    if rope is None:
        o_ref[...] = acc.astype(o_ref.dtype)
    else:
        q_blocks, q_scale = rope
        cos = cos_ref[...]
        sa = sa_ref[...]
        sb = sb_ref[...]
        half = ROPE_DIM // 2
        scale = jnp.where(j < q_blocks, q_scale, 1.0)
        for gi in range(bn // LANES):
            t = acc[:, gi * LANES:(gi + 1) * LANES]
            up = pltpu.roll(t, LANES - half, 1)
            dn = pltpu.roll(t, half, 1)
            o_ref[:, gi * LANES:(gi + 1) * LANES] = ((t * cos + up * sa + dn * sb) * scale).astype(o_ref.dtype)

    @pl.when(more)
    def _():
        slab_copy(j + 1, i).wait()
        land(i, 1 - slot)


def matmul_stream(x, w, layer, out_dtype, col0, ncols, rope=None, w_t=False):
    m, kdim = x.shape
    bm = min(MM_BM, m)
    bn = min(MM_BN, ncols)
    nm = m // bm
    assert m % bm == 0 and ncols % bn == 0 and kdim % nm == 0 and w.shape[2 if w_t else 1] == kdim
    assert col0 % (SUBLANES if w_t else LANES) == 0
    kc = kdim // nm
    in_specs = [pl.BlockSpec((bm, kdim), lambda j, i: (i, 0)), pl.BlockSpec(memory_space=pl.ANY)]
    args = [x, w]
    rope_static = None
    if rope is not None:
        cos, sa, sb, q_cols, q_scale = rope
        assert q_cols % bn == 0
        tab = pl.BlockSpec((bm, LANES), lambda j, i: (i, 0))
        in_specs += [tab, tab, tab]
        args += [cos, sa, sb]
        rope_static = (q_cols // bn, q_scale)
    osz = jnp.dtype(out_dtype).itemsize
    nbytes = 2 * (bm * kdim * 2 + bm * bn * osz) + 2 * kdim * bn * 2 + kc * bn * 4 + 3 * bm * bn * 4
    return pl.pallas_call(
        functools.partial(_mm_stream_kernel, layer=layer, col0=col0, kc=kc, nchunks=nm, rope=rope_static, w_t=w_t),
        grid=(ncols // bn, nm),
        in_specs=in_specs,
        out_specs=pl.BlockSpec((bm, bn), lambda j, i: (i, j)),
        out_shape=jax.ShapeDtypeStruct((m, ncols), out_dtype),
        scratch_shapes=[pltpu.VMEM((2, bn, kdim) if w_t else (2, kdim, bn), BF16),
                        pltpu.VMEM((bn, kc) if w_t else (kc, bn), F32),
                        pltpu.SemaphoreType.DMA((1,))],
        compiler_params=_params(nbytes + (2 << 20), 2),
        name="matmul_stream",
    )(*args)


def _ffn_up_stream_kernel(x_ref, w_hbm, cg_ref, cv0_ref, cv1_ref, o_ref, tail_g, tail_v, wg_b, wv_b,
                           st_g, st_v, sem, *, layer, bm, bn, d_ff, kc, nchunks):
    j = pl.program_id(0)
    i = pl.program_id(1)
    nj = pl.num_programs(0)
    slot = lax.rem(j, 2)
    hb = bn // 2

    def slab_copies(jb, c):
        rows = pl.ds(pl.multiple_of(c * kc, kc), kc)
        g0 = pl.multiple_of(jb * bn, LANES)
        v0 = pl.multiple_of(d_ff + jb * bn, LANES)
        v1 = pl.multiple_of(jnp.minimum(d_ff + jb * bn + hb, 2 * d_ff - hb), LANES)
        return (pltpu.make_async_copy(w_hbm.at[layer, rows, pl.ds(g0, bn)], st_g, sem.at[0]),
                pltpu.make_async_copy(w_hbm.at[layer, rows, pl.ds(v0, hb)], st_v.at[:, pl.ds(0, hb)], sem.at[1]),
                pltpu.make_async_copy(w_hbm.at[layer, rows, pl.ds(v1, hb)], st_v.at[:, pl.ds(hb, hb)], sem.at[2]))

    def land(c, dst):
        rows = pl.ds(pl.multiple_of(c * kc, kc), kc)
        wg_b[dst, rows, :] = st_g[...].astype(BF16)
        wv_b[dst, rows, :] = st_v[...].astype(BF16)

    @pl.when(jnp.logical_and(j == 0, i == 0))
    def _():
        for c in range(nchunks):
            cps = slab_copies(0, c)
            for cp in cps:
                cp.start()
            for cp in cps:
                cp.wait()
            land(c, 0)

    more = j + 1 < nj

    @pl.when(more)
    def _():
        for cp in slab_copies(j + 1, i):
            cp.start()

    @pl.when(i == 0)
    def _():
        tail_g[...] = jnp.zeros_like(tail_g)
        tail_v[...] = jnp.zeros_like(tail_v)

    x = x_ref[...]

    def conv(u, c, tail):
        ext = jnp.concatenate([tail[...], u], axis=0)
        u1 = pltpu.roll(ext, 1, 0)[SUBLANES:]
        u2 = pltpu.roll(ext, 2, 0)[SUBLANES:]
        tail[...] = u[bm - SUBLANES:]
        return c[3:4] + c[2:3] * u + c[1:2] * u1 + c[0:1] * u2

    ug = jnp.dot(x, wg_b[slot], preferred_element_type=F32)
    uv = jnp.dot(x, wv_b[slot], preferred_element_type=F32)
    g = conv(ug, cg_ref[...], tail_g)
    v = conv(uv, jnp.concatenate([cv0_ref[...], cv1_ref[...]], axis=1), tail_v)
    col = lax.broadcasted_iota(jnp.int32, (bm, bn), 1) + j * bn
    o_ref[...] = jnp.where(col < d_ff, _silu(g) * v, 0.0).astype(o_ref.dtype)

    @pl.when(more)
    def _():
        for cp in slab_copies(j + 1, i):
            cp.wait()
        land(i, 1 - slot)


def ffn_up_stream(h, w_up, layer, c8, d_ff, ffp):
    m, d = h.shape
    bm = min(MM_BM, m)
    bn = FFN_BN
    hb = bn // 2
    nm = m // bm
    assert ffp % bn == 0 and d_ff % hb == 0 and d % nm == 0 and ffp <= 2 * d_ff
    kc = d // nm
    v0 = d_ff // hb
    vlast = (2 * d_ff) // hb - 1
    vmap0 = lambda j, i: (0, jnp.minimum(v0 + 2 * j, vlast))
    vmap1 = lambda j, i: (0, jnp.minimum(v0 + 2 * j + 1, vlast))
    nbytes = (2 * (bm * d * 2 + bm * bn * 2) + 2 * 2 * d * bn * 2 + 2 * kc * bn * 4 + 12 * bm * bn * 4)
    return pl.pallas_call(
        functools.partial(_ffn_up_stream_kernel, layer=layer, bm=bm, bn=bn, d_ff=d_ff, kc=kc, nchunks=nm),
        grid=(ffp // bn, nm),
        in_specs=[pl.BlockSpec((bm, d), lambda j, i: (i, 0)),
                  pl.BlockSpec(memory_space=pl.ANY),
                  pl.BlockSpec((SUBLANES, bn), lambda j, i: (0, j)),
                  pl.BlockSpec((SUBLANES, hb), vmap0),
                  pl.BlockSpec((SUBLANES, hb), vmap1)],
        out_specs=pl.BlockSpec((bm, bn), lambda j, i: (i, j)),
        out_shape=jax.ShapeDtypeStruct((m, ffp), BF16),
        scratch_shapes=[pltpu.VMEM((SUBLANES, bn), F32), pltpu.VMEM((SUBLANES, bn), F32),
                        pltpu.VMEM((2, d, bn), BF16), pltpu.VMEM((2, d, bn), BF16),
                        pltpu.VMEM((kc, bn), F32), pltpu.VMEM((kc, bn), F32),
                        pltpu.SemaphoreType.DMA((3,))],
        compiler_params=_params(nbytes, 2),
        name="ffn_up_stream",
    )(h, w_up, c8, c8, c8)


def _dwconv_silu_kernel(x_ref, w_ref, o_ref, tail, *, taps, rb):
    @pl.when(pl.program_id(1) == 0)
    def _():
        tail[...] = jnp.zeros_like(tail)

    x = x_ref[...]
    ext = jnp.concatenate([tail[...], x], axis=0)
    w = w_ref[...]
    out = w[taps:taps + 1] + w[taps - 1:taps] * x
    for k in range(taps - 1):
        d = taps - 1 - k
        out = out + pltpu.roll(ext, d, 0)[SUBLANES:] * w[k:k + 1]
    tail[...] = x[rb - SUBLANES:]
    o_ref[...] = _silu(out).astype(o_ref.dtype)


def dwconv_silu(x, col0, ncols, w8, taps):
    m = x.shape[0]
    rb = min(ROW_BLK, m)
    cb = 1024
    assert col0 % cb == 0 and ncols % cb == 0
    off = col0 // cb
    return pl.pallas_call(
        functools.partial(_dwconv_silu_kernel, taps=taps, rb=rb),
        grid=(ncols // cb, m // rb),
        in_specs=[pl.BlockSpec((rb, cb), lambda j, i: (i, j + off)),
                  pl.BlockSpec((SUBLANES, cb), lambda j, i: (0, j))],
        out_specs=pl.BlockSpec((rb, cb), lambda j, i: (i, j)),
        out_shape=jax.ShapeDtypeStruct((m, ncols), F32),
        scratch_shapes=[pltpu.VMEM((SUBLANES, cb), F32)],
        compiler_params=_params(16 * rb * cb * 4, 2),
        name="dwconv_silu",
    )(x, w8)


def rope_tables(positions):
    half = ROPE_DIM // 2
    inv_freq = jnp.power(jnp.float32(ROPE_THETA), -jnp.arange(half, dtype=F32) * (2.0 / ROPE_DIM))
    ang = positions.astype(F32)[:, None] * inv_freq
    cos, sin = jnp.cos(ang), jnp.sin(ang)
    s = positions.shape[0]
    pad = LANES - ROPE_DIM
    cos_t = jnp.concatenate([cos, cos, jnp.ones((s, pad), F32)], axis=1)
    sa = jnp.concatenate([-sin, jnp.zeros((s, LANES - half), F32)], axis=1)
    sb = jnp.concatenate([jnp.zeros((s, half), F32), sin, jnp.zeros((s, pad), F32)], axis=1)
    return cos_t, sa, sb


def _flash_softmax(s, m_ref, l_ref, rows):
    tk = s.shape[1]
    sc = [s[:, c * LANES:(c + 1) * LANES] for c in range(tk // LANES)]
    m_prev = m_ref[rows, :]
    m_new = jnp.maximum(m_prev, jnp.max(functools.reduce(jnp.maximum, sc), axis=-1, keepdims=True))
    alpha = jnp.exp2(m_prev - m_new)
    pc = [jnp.exp2(c - m_new) for c in sc]
    l_ref[rows, :] = alpha * l_ref[rows, :] + functools.reduce(jnp.add, pc)
    m_ref[rows, :] = m_new
    return jnp.concatenate([c.astype(BF16) for c in pc], axis=1), alpha


def _flash_accumulate(p, alpha, v, acc_ref, rows):
    pv = jnp.dot(p, v, preferred_element_type=F32)
    a = jnp.concatenate([alpha] * (pv.shape[1] // LANES), axis=1) if pv.shape[1] > LANES else alpha
    acc_ref[rows, :] = a * acc_ref[rows, :] + pv


def _flash_block(chains, v):
    pa = [_flash_softmax(s, m_ref, l_ref, rows) for s, m_ref, l_ref, _, rows in chains]
    for (p, alpha), (_, _, _, acc_ref, rows) in zip(pa, chains):
        _flash_accumulate(p, alpha, v, acc_ref, rows)


def _flash_init(m_ref, l_ref, acc_ref):
    m_ref[...] = jnp.full(m_ref.shape, -jnp.inf, F32)
    l_ref[...] = jnp.zeros(l_ref.shape, F32)
    acc_ref[...] = jnp.zeros(acc_ref.shape, F32)


def _flash_result(l_ref, acc_ref):
    return acc_ref[...] / jnp.sum(l_ref[...], axis=-1, keepdims=True)


def _causal_mask(tr, tk, delta):
    r = lax.broadcasted_iota(jnp.int32, (tr, tk), 0)
    c = lax.broadcasted_iota(jnp.int32, (tr, tk), 1)
    return c <= r + delta


def _diag_plan(tq, tk, tr):
    plan = []
    for co in range(0, tq, tk):
        subs = []
        for r0 in range(0, tq, tr):
            if r0 + tr - 1 < co:
                continue
            subs.append((r0, None if co + tk - 1 <= r0 else r0 - co))
        plan.append(subs)
    return plan


def _causal_sweep(qi, tq, tk, tr, block):
    full = [(r0, None) for r0 in range(0, tq, tr)]
    per_q = tq // tk

    def body(j, carry):
        block(j, full)
        return carry

    lax.fori_loop(0, qi * per_q, body, 0)
    for b, subs in enumerate(_diag_plan(tq, tk, tr)):
        block(qi * per_q + b, subs)


def _fox_kernel(q_ref, k_ref, v_ref, b_ref, mix_ref, o_ref, q_scr, m_ref, l_ref, acc_ref, *, tq, tk, tr, q_scale):
    del mix_ref
    q_scr[...] = (q_ref[...] * q_scale).astype(BF16)
    _flash_init(m_ref, l_ref, acc_ref)

    def block(j, subs):
        off = pl.multiple_of(j * tk, tk)
        k = k_ref[pl.ds(off, tk), :]
        v = v_ref[pl.ds(off, tk), :]
        bias = b_ref[:, pl.ds(off, tk)] * LOG2E
        chains = []
        for r0, delta in subs:
            rows = pl.ds(r0, tr)
            s = lax.dot_general(q_scr[rows, :], k, _NT, preferred_element_type=F32) + bias
            if delta is not None:
                s = jnp.where(_causal_mask(tr, tk, delta), s, -jnp.inf)
            chains.append((s, m_ref, l_ref, acc_ref, rows))
        _flash_block(chains, v)

    _causal_sweep(pl.program_id(1), tq, tk, tr, block)
    o_ref[...] = _flash_result(l_ref, acc_ref).astype(o_ref.dtype)


def _att_tiles(s):
    tq = min(ATT_TQ, s)
    tk = min(ATT_TK, tq)
    tr = min(ATT_SUB, tk)
    assert s % tq == 0 and tq % tk == 0 and tk % tr == 0
    return tq, tk, tr


def fox_attention(q, kv, negcum, n_heads, mix):
    s = q.shape[0]
    tq, tk, tr = _att_tiles(s)
    hd = HEAD_DIM
    nbytes = 2 * (2 * s * hd * 2 + tq * hd * 4 + tq * hd * 2 + s * 4) + 12 * tq * tk * 4
    return pl.pallas_call(
        functools.partial(_fox_kernel, tq=tq, tk=tk, tr=tr, q_scale=hd ** -0.5 * LOG2E),
        grid=(n_heads, s // tq),
        in_specs=[pl.BlockSpec((tq, hd), lambda h, i: (i, h)),
                  pl.BlockSpec((s, hd), lambda h, i: (0, h)),
                  pl.BlockSpec((s, hd), lambda h, i: (0, n_heads + h)),
                  pl.BlockSpec((None, 1, s), lambda h, i: (h, 0, 0)),
                  pl.BlockSpec(memory_space=pl.ANY)],
        out_specs=pl.BlockSpec((tq, hd), lambda h, i: (i, h)),
        out_shape=jax.ShapeDtypeStruct(mix.shape, mix.dtype),
        input_output_aliases={4: 0},
        scratch_shapes=[pltpu.VMEM((tq, hd), BF16), pltpu.VMEM((tq, LANES), F32),
                        pltpu.VMEM((tq, LANES), F32), pltpu.VMEM((tq, hd), F32)],
        compiler_params=_params(nbytes, 2),
        name="fox_attention",
    )(q, kv, kv, negcum, mix)


def _diff_kernel(q_ref, k_ref, v_ref, lam_ref, g_ref, mix_ref, o_ref, m0, l0, a0, m1, l1, a1,
                 *, tq, tk, tr, lam_init):
    del mix_ref
    hd = HEAD_DIM
    _flash_init(m0, l0, a0)
    _flash_init(m1, l1, a1)

    def block(j, subs):
        off = pl.multiple_of(j * tk, tk)
        v = v_ref[pl.ds(off, tk), :]
        chains = []
        for lo, st in ((0, (m0, l0, a0)), (hd, (m1, l1, a1))):
            k = k_ref[pl.ds(off, tk), lo:lo + hd]
            for r0, delta in subs:
                rows = pl.ds(r0, tr)
                s = lax.dot_general(q_ref[rows, lo:lo + hd], k, _NT, preferred_element_type=F32)
                if delta is not None:
                    s = jnp.where(_causal_mask(tr, tk, delta), s, -jnp.inf)
                chains.append((s,) + st + (rows,))
        _flash_block(chains, v)

    _causal_sweep(pl.program_id(1), tq, tk, tr, block)
    lv = lam_ref[...]
    lam = (jnp.exp(jnp.sum(lv[0:1] * lv[1:2], axis=-1, keepdims=True))
           - jnp.exp(jnp.sum(lv[2:3] * lv[3:4], axis=-1, keepdims=True)) + lam_init)
    o = _flash_result(l0, a0) - lam * _flash_result(l1, a1)
    o = o * lax.rsqrt(jnp.mean(o * o, axis=-1, keepdims=True) + NORM_EPS) * g_ref[...]
    o_ref[...] = (o * (1.0 - lam_init)).astype(o_ref.dtype)


def diff_attention(qk, v, lam_vecs, subln_g, lam_init, n_heads, mix):
    s = qk.shape[0]
    tq, tk, tr = _att_tiles(s)
    t = tq
    w = 2 * HEAD_DIM
    nbytes = 2 * (2 * s * w * 2 + 2 * tq * w * 2) + 24 * tq * tk * 4
    stats = lambda: [pltpu.VMEM((t, LANES), F32), pltpu.VMEM((t, LANES), F32), pltpu.VMEM((t, w), F32)]
    return pl.pallas_call(
        functools.partial(_diff_kernel, tq=tq, tk=tk, tr=tr, lam_init=lam_init),
        grid=(n_heads, s // t),
        in_specs=[pl.BlockSpec((t, w), lambda h, i: (i, h)),
                  pl.BlockSpec((s, w), lambda h, i: (0, n_heads + h)),
                  pl.BlockSpec((s, w), lambda h, i: (0, h)),
                  pl.BlockSpec((4, HEAD_DIM), lambda h, i: (0, 0)),
                  pl.BlockSpec((1, w), lambda h, i: (0, 0)),
                  pl.BlockSpec(memory_space=pl.ANY)],
        out_specs=pl.BlockSpec((t, w), lambda h, i: (i, h)),
        out_shape=jax.ShapeDtypeStruct(mix.shape, mix.dtype),
        input_output_aliases={5: 0},
        scratch_shapes=stats() + stats(),
        compiler_params=_params(nbytes, 2),
        name="diff_attention",
    )(qk, qk, v, lam_vecs.astype(F32), subln_g.reshape(1, w).astype(F32), mix)


def _mem_attn_kernel(q_ref, kv_ref, o_ref, *, scale, heads, hd, ctx_col0):
    o_ref[:, :ctx_col0] = jnp.zeros((o_ref.shape[0], ctx_col0), o_ref.dtype)
    for h in range(heads):
        k = kv_ref[:, h * hd:(h + 1) * hd]
        v = kv_ref[:, (heads + h) * hd:(heads + h + 1) * hd]
        s = lax.dot_general(q_ref[:, h * hd:(h + 1) * hd], k, _NT, preferred_element_type=F32) * scale
        p = jnp.exp(s - jnp.max(s, axis=-1, keepdims=True))
        l = jnp.sum(p, axis=-1, keepdims=True)
        o = jnp.dot(p.astype(BF16), v, preferred_element_type=F32)
        o_ref[:, ctx_col0 + h * hd:ctx_col0 + (h + 1) * hd] = (o / l).astype(o_ref.dtype)


def mem_attention(qarr, q_col0, mem_kv, out_width, ctx_col0):
    s = qarr.shape[0]
    mem_width = mem_kv.shape[1] // 2
    hd = mem_width // MEM_HEADS
    mt = mem_kv.shape[0]
    tq = min(MEM_TQ, s)
    assert q_col0 % mem_width == 0 and ctx_col0 + mem_width == out_width
    q_blk = q_col0 // mem_width
    return pl.pallas_call(
        functools.partial(_mem_attn_kernel, scale=hd ** -0.5, heads=MEM_HEADS, hd=hd, ctx_col0=ctx_col0),
        grid=(s // tq,),
        in_specs=[pl.BlockSpec((tq, mem_width), lambda i: (i, q_blk)),
                  pl.BlockSpec((mt, 2 * mem_width), lambda i: (0, 0))],
        out_specs=pl.BlockSpec((tq, out_width), lambda i: (i, 0)),
        out_shape=jax.ShapeDtypeStruct((s, out_width), BF16),
        compiler_params=_params(4 * tq * out_width * 2 + 32 * tq * hd * 4, 1),
        name="mem_attention",
    )(qarr, mem_kv)


def _fox_cum_kernel(x_ref, b_ref, o_ref, carry, *, cb):
    @pl.when(pl.program_id(0) == 0)
    def _():
        carry[...] = jnp.zeros_like(carry)

    x = x_ref[...] + b_ref[...]
    log_f = jnp.minimum(x, 0.0) - jnp.log1p(jnp.exp(-jnp.abs(x)))
    cs = _cumsum_rows(_tri_lower(cb), log_f) + carry[...]
    o_ref[...] = cs
    carry[...] = cs[cb - 1:cb, :]


def fox_cum(logits, bias):
    s = logits.shape[0]
    cb = min(CUM_BLK, s)
    return pl.pallas_call(
        functools.partial(_fox_cum_kernel, cb=cb),
        grid=(s // cb,),
        in_specs=[pl.BlockSpec((cb, LANES), lambda i: (i, 0)),
                  pl.BlockSpec((1, LANES), lambda i: (0, 0))],
        out_specs=pl.BlockSpec((cb, LANES), lambda i: (i, 0)),
        out_shape=jax.ShapeDtypeStruct((s, LANES), F32),
        scratch_shapes=[pltpu.VMEM((1, LANES), F32)],
        compiler_params=_params(16 << 20, 1),
        name="fox_cum",
    )(logits, bias)


def _ssd_kernel(z_ref, xs_ref, b_ref, c_ref, dtc_ref, dtr_ref, bias_e_ref, alog_e_ref, dskip_e_ref,
                bias_r_ref, alog_r_ref, ng_ref, mix_ref, o_ref, h_ref, *, heads, hdim, chunk):
    del mix_ref
    @pl.when(pl.program_id(1) == 0)
    def _():
        h_ref[...] = jnp.zeros_like(h_ref)

    L = chunk
    width = heads * hdim
    xs = xs_ref[...]
    bm = b_ref[...]
    cm = c_ref[...]
    tri = _tri_lower(L)

    lane = lax.broadcasted_iota(jnp.int32, (LANES, width), 0)
    head = lax.broadcasted_iota(jnp.int32, (LANES, width), 1) // hdim + pl.program_id(0) * heads
    spread = jnp.where(lane == head, 1.0, 0.0).astype(BF16)
    dt_raw = sum(jnp.dot(part, spread, preferred_element_type=F32) for part in _split3(dtc_ref[...]))
    dt_e = _softplus(dt_raw + bias_e_ref[...])
    a_e = dt_e * (-jnp.exp(alog_e_ref[...]))
    acs_e = _cumsum_rows(tri, a_e)
    acs_last = acs_e[L - 1:L, :]
    dt_r = _softplus(dtr_ref[...] + bias_r_ref[...])
    a_r = dt_r * (-jnp.exp(alog_r_ref[...]))
    acs_r = _cumsum_lanes(a_r, tri.T)

    xd = xs * dt_e
    xd_b = xd.astype(BF16)
    cb = lax.dot_general(cm.astype(BF16), bm.astype(BF16), _NT, preferred_element_type=F32)
    causal = _causal_mask(L, L, 0)
    head_id = lax.broadcasted_iota(jnp.int32, (L, width), 1) // hdim

    y = jnp.zeros((L, width), F32)
    for r in range(heads):
        seg = acs_e[:, r * hdim:r * hdim + 1] - acs_r[r:r + 1, :]
        decay = jnp.exp(jnp.where(causal, seg, -jnp.inf))
        yr = jnp.dot((cb * decay).astype(BF16), xd_b, preferred_element_type=F32)
        y = jnp.where(head_id == r, yr, y)

    h_in = h_ref[...]
    y_off = jnp.dot(cm.astype(BF16), h_in.astype(BF16), preferred_element_type=F32) * jnp.exp(acs_e)
    to_end = jnp.exp(acs_last - acs_e)
    states = jnp.dot(bm.T.astype(BF16), (xd * to_end).astype(BF16), preferred_element_type=F32)
    h_ref[...] = h_in * jnp.exp(acs_last) + states

    y = y + y_off + xs * dskip_e_ref[...]
    g = y * _silu(z_ref[...])
    g = g * lax.rsqrt(jnp.mean(g * g, axis=-1, keepdims=True) + NORM_EPS)
    o_ref[...] = (g * ng_ref[...]).astype(o_ref.dtype)


def ssd_scan(zx, xbc, dt_c, dt_r, bias_e, alog_e, dskip_e, bias_r, alog_r, norm_g, inner, mix):
    s = zx.shape[0]
    L = SSD_CHUNK
    G = SSD_GROUPS
    width = inner // G
    heads = width // SSD_HEAD_DIM
    N = SSD_STATE
    b_off = inner // N
    c_off = b_off + G
    row = lambda g, c: (0, g)
    return pl.pallas_call(
        functools.partial(_ssd_kernel, heads=heads, hdim=SSD_HEAD_DIM, chunk=L),
        grid=(G, s // L),
        in_specs=[pl.BlockSpec((L, width), lambda g, c: (c, g)),
                  pl.BlockSpec((L, width), lambda g, c: (c, g)),
                  pl.BlockSpec((L, N), lambda g, c: (c, b_off + g)),
                  pl.BlockSpec((L, N), lambda g, c: (c, c_off + g)),
                  pl.BlockSpec((L, LANES), lambda g, c: (c, 0)),
                  pl.BlockSpec((None, SUBLANES, L), lambda g, c: (g, 0, c)),
                  pl.BlockSpec((1, width), row),
                  pl.BlockSpec((1, width), row),
                  pl.BlockSpec((1, width), row),
                  pl.BlockSpec((None, SUBLANES, 1), lambda g, c: (g, 0, 0)),
                  pl.BlockSpec((None, SUBLANES, 1), lambda g, c: (g, 0, 0)),
                  pl.BlockSpec((1, width), row),
                  pl.BlockSpec(memory_space=pl.ANY)],
        out_specs=pl.BlockSpec((L, width), lambda g, c: (c, g)),
        out_shape=jax.ShapeDtypeStruct(mix.shape, mix.dtype),
        input_output_aliases={12: 0},
        scratch_shapes=[pltpu.VMEM((N, width), F32)],
        compiler_params=_params(32 << 20, 2),
        name="ssd_scan",
    )(zx, xbc, xbc, xbc, dt_c, dt_r, bias_e, alog_e, dskip_e, bias_r, alog_r, norm_g, mix)


def _conv_rows(w, b):
    taps, c = w.shape
    return jnp.concatenate([w.astype(F32), b.reshape(1, c).astype(F32),
                            jnp.zeros((SUBLANES - taps - 1, c), F32)], axis=0)


def _pad_cols(a, n):
    return jnp.pad(a, ((0, 0), (0, n - a.shape[1])))


def kernel(x, mem, positions, norm_mix, norm_mem, w_mem_kv, w_out, norm_ffn, w_up, conv_ffn_w, conv_ffn_b,
           w_down, a_w_in, a_lambda, a_subln, b_w_in, b_forget_bias, c_w_in, c_conv_w, c_conv_b, c_dt_bias,
           c_a_log, c_d_skip, c_norm_gate, final_norm):
    batch, seq, d_model = x.shape
    depth = norm_mix.shape[0]
    mem_width = w_mem_kv.shape[2] // 2
    tok_width = d_model - mem_width
    d_ff = w_down.shape[1]
    ffp = DOWN_BK * ((d_ff + DOWN_BK - 1) // DOWN_BK)
    diff_heads = tok_width // (2 * HEAD_DIM)
    fox_heads = tok_width // HEAD_DIM
    ssd_heads = tok_width // SSD_HEAD_DIM
    heads_per_group = ssd_heads // SSD_GROUPS
    conv_ch = tok_width + 2 * SSD_GROUPS * SSD_STATE
    q_scale = HEAD_DIM ** -0.5 * LOG2E

    w_out_b = w_out.astype(BF16)
    w_down_b = w_down.astype(BF16)
    w_mem_kv_b = w_mem_kv.astype(BF16)
    outs = []
    for b in range(batch):
        xb = x[b]
        mem_b = mem[b]
        cos_t, sa, sb = rope_tables(positions[b])
        for i in range(depth):
            kind, j = i % N_MIXERS, i // N_MIXERS
            h = rmsnorm(xb, norm_mix[i], BF16)
            mem_kv = matmul(rmsnorm(mem_b, norm_mem[i], BF16), w_mem_kv_b, BF16, layer=i)
            if kind == 0:
                qk = matmul_stream(h, a_w_in, j, BF16, 0, 2 * tok_width, rope=(cos_t, sa, sb, tok_width, q_scale))
                vq = matmul_stream(h, a_w_in, j, BF16, 2 * tok_width, tok_width + mem_width)
                mix = mem_attention(vq, tok_width, mem_kv, d_model, tok_width)
                lam_init = 0.8 - 0.6 * math.exp(-0.3 * i)
                mix = diff_attention(qk, vq, a_lambda[j], a_subln[j], lam_init, diff_heads, mix)
            elif kind == 1:
                w_t = jnp.swapaxes(b_w_in, 1, 2)
                q = matmul_stream(h, w_t, j, F32, 0, tok_width, w_t=True)
                kv = matmul_stream(h, w_t, j, BF16, tok_width, 2 * tok_width, w_t=True)
                logits = matmul_stream(h, w_t, j, F32, 3 * tok_width, LANES, w_t=True)
                mem_q = matmul_stream(h, w_t, j, BF16, 3 * tok_width + fox_heads, mem_width, w_t=True)
                mix = mem_attention(mem_q, 0, mem_kv, d_model, tok_width)
                bias = _pad_cols(b_forget_bias[j].reshape(1, fox_heads).astype(F32), LANES)
                cum = fox_cum(logits, bias)
                negcum = (-cum[:, :fox_heads]).T.reshape(fox_heads, 1, seq)
                mix = fox_attention(q, kv, negcum, fox_heads, mix)
            else:
                n_zx = tok_width + conv_ch
                w_t = jnp.swapaxes(c_w_in, 1, 2)
                zx = matmul_stream(h, w_t, j, F32, 0, n_zx, w_t=True)
                dt_c = matmul_stream(h, w_t, j, F32, n_zx, LANES, w_t=True)
                mem_q = matmul_stream(h, w_t, j, BF16, n_zx + ssd_heads, mem_width, w_t=True)
                mix = mem_attention(mem_q, 0, mem_kv, d_model, tok_width)
                xbc = dwconv_silu(zx, tok_width, conv_ch, _conv_rows(c_conv_w[j], c_conv_b[j]), SSD_CONV)
                rep = lambda a: jnp.repeat(a.astype(F32), SSD_HEAD_DIM, axis=-1)
                pad_h = SUBLANES - heads_per_group
                by_group = lambda a: jnp.pad(a.astype(F32).reshape(SSD_GROUPS, heads_per_group, -1),
                                             ((0, 0), (0, pad_h), (0, 0)))
                dt_r = by_group(dt_c[:, :ssd_heads].T)
                mix = ssd_scan(zx, xbc, dt_c, dt_r,
                               rep(c_dt_bias[j].reshape(1, -1)), rep(c_a_log[j].reshape(1, -1)),
                               rep(c_d_skip[j].reshape(1, -1)),
                               by_group(c_dt_bias[j].reshape(-1, 1)), by_group(c_a_log[j].reshape(-1, 1)),
                               c_norm_gate[j].reshape(1, -1).astype(F32), tok_width, mix)
            xb = matmul(mix, w_out_b, F32, res=xb, layer=i)

            hf = rmsnorm(xb, norm_ffn[i], BF16)
            hidden = ffn_up_stream(hf, w_up, i, _conv_rows(conv_ffn_w[i], conv_ffn_b[i]), d_ff, ffp)
            xb = matmul(hidden, w_down_b, F32, res=xb, bk=DOWN_BK, layer=i)
        outs.append(rmsnorm(xb, final_norm, x.dtype))
    return outs[0].reshape(x.shape) if batch == 1 else jnp.stack(outs, axis=0)
```

```python
import functools
import math

import jax
import jax.numpy as jnp
from jax import lax
from jax.experimental import pallas as pl
from jax.experimental.pallas import tpu as pltpu

F32 = jnp.float32
BF16 = jnp.bfloat16

HEAD_DIM = 128
MEM_HEADS = 4
SSD_HEAD_DIM = 64
SSD_GROUPS = 8
SSD_STATE = 128
SSD_CONV = 4
SSD_CHUNK = 128
ROPE_THETA = 500000.0
ROPE_DIM = HEAD_DIM // 4
FFN_CONV = 3
NORM_EPS = 1e-6
N_MIXERS = 3
LOG2E = math.log2(math.e)

LANES = 128
SUBLANES = 8
VMEM_LIMIT_CAP = 56 * 1024 * 1024

MM_BM = 1024
MM_BN = 1024
FFN_BN = 512
DOWN_BK = 2816
ATT_TQ = 2048
ATT_TK = 512
ATT_SUB = 512
MEM_TQ = 512
ROW_BLK = 256
CUM_BLK = 256
SSD_STEP_CHUNKS = 4


_NT = (((1,), (1,)), ((), ()))


def _vmem(nbytes):
    return int(min(VMEM_LIMIT_CAP, max(16 * 1024 * 1024, nbytes)))


def _params(nbytes, n_grid):
    return pltpu.CompilerParams(
        dimension_semantics=("arbitrary",) * n_grid, vmem_limit_bytes=_vmem(nbytes))


def _softplus(x):
    return jnp.maximum(x, 0.0) + jnp.log1p(jnp.exp(-jnp.abs(x)))


def _silu(x):
    return x / (1.0 + jnp.exp(-x))


def _split3(x):
    hi = x.astype(BF16)
    r = x - hi.astype(F32)
    mid = r.astype(BF16)
    lo = (r - mid.astype(F32)).astype(BF16)
    return hi, mid, lo


def _tri_lower(n):
    r = lax.broadcasted_iota(jnp.int32, (n, n), 0)
    c = lax.broadcasted_iota(jnp.int32, (n, n), 1)
    return jnp.where(c <= r, 1.0, 0.0).astype(BF16)


def _cumsum_rows(tri, x):
    hi, mid, lo = _split3(x)
    d = lambda a: jnp.dot(tri, a, preferred_element_type=F32)
    return d(hi) + d(mid) + d(lo)


def _cumsum_lanes(x, tri_t):
    hi, mid, lo = _split3(x)
    d = lambda a: jnp.dot(a, tri_t, preferred_element_type=F32)
    return d(hi) + d(mid) + d(lo)


def _rmsnorm_kernel(x_ref, g_ref, o_ref):
    x = x_ref[...].astype(F32)
    ms = jnp.mean(x * x, axis=-1, keepdims=True)
    o_ref[...] = (x * lax.rsqrt(ms + NORM_EPS) * g_ref[...]).astype(o_ref.dtype)


def rmsnorm(x, g, out_dtype):
    m, d = x.shape
    rb = min(ROW_BLK, m)
    return pl.pallas_call(
        _rmsnorm_kernel,
        grid=(m // rb,),
        in_specs=[pl.BlockSpec((rb, d), lambda i: (i, 0)),
                  pl.BlockSpec((1, d), lambda i: (0, 0))],
        out_specs=pl.BlockSpec((rb, d), lambda i: (i, 0)),
        out_shape=jax.ShapeDtypeStruct((m, d), out_dtype),
        compiler_params=_params(6 * rb * d * 4, 1),
        name="rmsnorm",
    )(x, g.reshape(1, d).astype(F32))


def _mm_kernel(*refs, nk, bk, k_true, has_res):
    if has_res:
        x_ref, w_ref, r_ref, o_ref = refs[:4]
        scratch = refs[4:]
    else:
        x_ref, w_ref, o_ref = refs[:3]
        r_ref = None
        scratch = refs[3:]
    def product(last):
        w = w_ref[...]
        if last and k_true is not None:
            row = lax.broadcasted_iota(jnp.int32, w.shape, 0)
            w = jnp.where(row < k_true - (nk - 1) * bk, w, jnp.zeros_like(w))
        return jnp.dot(x_ref[...], w, preferred_element_type=F32)

    if nk == 1:
        part = product(True)
        if has_res:
            part = part + r_ref[...]
        o_ref[...] = part.astype(o_ref.dtype)
        return
    acc_ref = scratch[0]
    k = pl.program_id(2)

    @pl.when(k == 0)
    def _():
        acc_ref[...] = product(False)

    @pl.when(jnp.logical_and(k > 0, k < nk - 1))
    def _():
        acc_ref[...] = acc_ref[...] + product(False)

    @pl.when(k == nk - 1)
    def _():
        tot = acc_ref[...] + product(True)
        if has_res:
            tot = tot + r_ref[...]
        o_ref[...] = tot.astype(o_ref.dtype)


def matmul(x, w, out_dtype, col0=0, ncols=None, res=None, bk=None, layer=None):
    m, kdim = x.shape
    w_rows, w_cols = w.shape[-2:]
    n = w_cols - col0 if ncols is None else ncols
    bm = min(MM_BM, m)
    bn = min(MM_BN, n)
    bk = kdim if bk is None else bk
    nk = kdim // bk
    assert m % bm == 0 and n % bn == 0 and kdim % bk == 0 and col0 % bn == 0
    k_true = None if w_rows == kdim else w_rows
    assert w_rows <= kdim and w_rows > (nk - 1) * bk
    joff = col0 // bn
    if layer is None:
        w_spec = pl.BlockSpec((bk, bn), lambda j, i, k: (k, j + joff))
    else:
        w_spec = pl.BlockSpec((None, bk, bn), lambda j, i, k: (layer, k, j + joff))
    in_specs = [pl.BlockSpec((bm, bk), lambda j, i, k: (i, k)), w_spec]
    args = [x, w]
    if res is not None:
        in_specs.append(pl.BlockSpec((bm, bn), lambda j, i, k: (i, j)))
        args.append(res)
    osz = jnp.dtype(out_dtype).itemsize
    nbytes = 2 * (bm * bk * 2 + bk * bn * 2 + bm * bn * osz) + 2 * bm * bn * 4
    if res is not None:
        nbytes += 2 * bm * bn * 4
    scratch = []
    if nk > 1:
        scratch.append(pltpu.VMEM((bm, bn), F32))
        nbytes += bm * bn * 4
    return pl.pallas_call(
        functools.partial(_mm_kernel, nk=nk, bk=bk, k_true=k_true, has_res=res is not None),
        grid=(n // bn, m // bm, nk),
        in_specs=in_specs,
        out_specs=pl.BlockSpec((bm, bn), lambda j, i, k: (i, j)),
        out_shape=jax.ShapeDtypeStruct((m, n), out_dtype),
        scratch_shapes=scratch,
        compiler_params=_params(nbytes + (4 << 20), 3),
        name="matmul",
    )(*args)


def _mm_stream_kernel(*refs, layer, col0, kc, nchunks, rope, w_t):
    x_ref, w_hbm = refs[:2]
    if rope is None:
        o_ref, wbf, stage, sem = refs[2:]
    else:
        cos_ref, sa_ref, sb_ref, o_ref, wbf, stage, sem = refs[2:]
    j = pl.program_id(0)
    i = pl.program_id(1)
    nj = pl.num_programs(0)
    slot = lax.rem(j, 2)
    bn = o_ref.shape[1]

    def slab_copy(jb, c):
        rows = pl.ds(pl.multiple_of(c * kc, kc), kc)
        cols = pl.ds(pl.multiple_of(col0 + jb * bn, SUBLANES if w_t else LANES), bn)
        src = w_hbm.at[layer, cols, rows] if w_t else w_hbm.at[layer, rows, cols]
        return pltpu.make_async_copy(src, stage, sem.at[0])

    def land(c, dst):
        rows = pl.ds(pl.multiple_of(c * kc, kc), kc)
        if w_t:
            wbf[dst, :, rows] = stage[...].astype(BF16)
        else:
            wbf[dst, rows, :] = stage[...].astype(BF16)

    @pl.when(jnp.logical_and(j == 0, i == 0))
    def _():
        for c in range(nchunks):
            cp = slab_copy(0, c)
            cp.start()
            cp.wait()
            land(c, 0)

    more = j + 1 < nj

    @pl.when(more)
    def _():
        slab_copy(j + 1, i).start()

    if w_t:
        acc = lax.dot_general(x_ref[...], wbf[slot], _NT, preferred_element_type=F32)
    else:
        acc = jnp.dot(x_ref[...], wbf[slot], preferred_element_type=F32)
    if rope is None:
        o_ref[...] = acc.astype(o_ref.dtype)
    else:
        q_blocks, q_scale = rope
        cos = cos_ref[...]
        sa = sa_ref[...]
        sb = sb_ref[...]
        half = ROPE_DIM // 2
        scale = jnp.where(j < q_blocks, q_scale, 1.0)
        for gi in range(bn // LANES):
            t = acc[:, gi * LANES:(gi + 1) * LANES]
            up = pltpu.roll(t, LANES - half, 1)
            dn = pltpu.roll(t, half, 1)
            o_ref[:, gi * LANES:(gi + 1) * LANES] = ((t * cos + up * sa + dn * sb) * scale).astype(o_ref.dtype)

    @pl.when(more)
    def _():
        slab_copy(j + 1, i).wait()
        land(i, 1 - slot)


def matmul_stream(x, w, layer, out_dtype, col0, ncols, rope=None, w_t=False):
    m, kdim = x.shape
    bm = min(MM_BM, m)
    bn = min(MM_BN, ncols)
    nm = m // bm
    assert m % bm == 0 and ncols % bn == 0 and kdim % nm == 0 and w.shape[2 if w_t else 1] == kdim
    assert col0 % (SUBLANES if w_t else LANES) == 0
    kc = kdim // nm
    in_specs = [pl.BlockSpec((bm, kdim), lambda j, i: (i, 0)), pl.BlockSpec(memory_space=pl.ANY)]
    args = [x, w]
    rope_static = None
    if rope is not None:
        cos, sa, sb, q_cols, q_scale = rope
        assert q_cols % bn == 0
        tab = pl.BlockSpec((bm, LANES), lambda j, i: (i, 0))
        in_specs += [tab, tab, tab]
        args += [cos, sa, sb]
        rope_static = (q_cols // bn, q_scale)
    osz = jnp.dtype(out_dtype).itemsize
    nbytes = 2 * (bm * kdim * 2 + bm * bn * osz) + 2 * kdim * bn * 2 + kc * bn * 4 + 3 * bm * bn * 4
    return pl.pallas_call(
        functools.partial(_mm_stream_kernel, layer=layer, col0=col0, kc=kc, nchunks=nm, rope=rope_static, w_t=w_t),
        grid=(ncols // bn, nm),
        in_specs=in_specs,
        out_specs=pl.BlockSpec((bm, bn), lambda j, i: (i, j)),
        out_shape=jax.ShapeDtypeStruct((m, ncols), out_dtype),
        scratch_shapes=[pltpu.VMEM((2, bn, kdim) if w_t else (2, kdim, bn), BF16),
                        pltpu.VMEM((bn, kc) if w_t else (kc, bn), F32),
                        pltpu.SemaphoreType.DMA((1,))],
        compiler_params=_params(nbytes + (2 << 20), 2),
        name="matmul_stream",
    )(*args)


def _ffn_up_stream_kernel(x_ref, w_hbm, cg_ref, cv0_ref, cv1_ref, o_ref, tail_g, tail_v, wg_b, wv_b,
                           st_g, st_v, sem, *, layer, bm, bn, d_ff, kc, nchunks):
    j = pl.program_id(0)
    i = pl.program_id(1)
    nj = pl.num_programs(0)
    slot = lax.rem(j, 2)
    hb = bn // 2

    def slab_copies(jb, c):
        rows = pl.ds(pl.multiple_of(c * kc, kc), kc)
        g0 = pl.multiple_of(jb * bn, LANES)
        v0 = pl.multiple_of(d_ff + jb * bn, LANES)
        v1 = pl.multiple_of(jnp.minimum(d_ff + jb * bn + hb, 2 * d_ff - hb), LANES)
        return (pltpu.make_async_copy(w_hbm.at[layer, rows, pl.ds(g0, bn)], st_g, sem.at[0]),
                pltpu.make_async_copy(w_hbm.at[layer, rows, pl.ds(v0, hb)], st_v.at[:, pl.ds(0, hb)], sem.at[1]),
                pltpu.make_async_copy(w_hbm.at[layer, rows, pl.ds(v1, hb)], st_v.at[:, pl.ds(hb, hb)], sem.at[2]))

    def land(c, dst):
        rows = pl.ds(pl.multiple_of(c * kc, kc), kc)
        wg_b[dst, rows, :] = st_g[...].astype(BF16)
        wv_b[dst, rows, :] = st_v[...].astype(BF16)

    @pl.when(jnp.logical_and(j == 0, i == 0))
    def _():
        for c in range(nchunks):
            cps = slab_copies(0, c)
            for cp in cps:
                cp.start()
            for cp in cps:
                cp.wait()
            land(c, 0)

    more = j + 1 < nj

    @pl.when(more)
    def _():
        for cp in slab_copies(j + 1, i):
            cp.start()

    @pl.when(i == 0)
    def _():
        tail_g[...] = jnp.zeros_like(tail_g)
        tail_v[...] = jnp.zeros_like(tail_v)

    x = x_ref[...]

    def conv(u, c, tail):
        ext = jnp.concatenate([tail[...], u], axis=0)
        u1 = pltpu.roll(ext, 1, 0)[SUBLANES:]
        u2 = pltpu.roll(ext, 2, 0)[SUBLANES:]
        tail[...] = u[bm - SUBLANES:]
        return c[3:4] + c[2:3] * u + c[1:2] * u1 + c[0:1] * u2

    ug = jnp.dot(x, wg_b[slot], preferred_element_type=F32)
    uv = jnp.dot(x, wv_b[slot], preferred_element_type=F32)
    g = conv(ug, cg_ref[...], tail_g)
    v = conv(uv, jnp.concatenate([cv0_ref[...], cv1_ref[...]], axis=1), tail_v)
    col = lax.broadcasted_iota(jnp.int32, (bm, bn), 1) + j * bn
    o_ref[...] = jnp.where(col < d_ff, _silu(g) * v, 0.0).astype(o_ref.dtype)

    @pl.when(more)
    def _():
        for cp in slab_copies(j + 1, i):
            cp.wait()
        land(i, 1 - slot)


def ffn_up_stream(h, w_up, layer, c8, d_ff, ffp):
    m, d = h.shape
    bm = min(MM_BM, m)
    bn = FFN_BN
    hb = bn // 2
    nm = m // bm
    assert ffp % bn == 0 and d_ff % hb == 0 and d % nm == 0 and ffp <= 2 * d_ff
    kc = d // nm
    v0 = d_ff // hb
    vlast = (2 * d_ff) // hb - 1
    vmap0 = lambda j, i: (0, jnp.minimum(v0 + 2 * j, vlast))
    vmap1 = lambda j, i: (0, jnp.minimum(v0 + 2 * j + 1, vlast))
    nbytes = (2 * (bm * d * 2 + bm * bn * 2) + 2 * 2 * d * bn * 2 + 2 * kc * bn * 4 + 12 * bm * bn * 4)
    return pl.pallas_call(
        functools.partial(_ffn_up_stream_kernel, layer=layer, bm=bm, bn=bn, d_ff=d_ff, kc=kc, nchunks=nm),
        grid=(ffp // bn, nm),
        in_specs=[pl.BlockSpec((bm, d), lambda j, i: (i, 0)),
                  pl.BlockSpec(memory_space=pl.ANY),
                  pl.BlockSpec((SUBLANES, bn), lambda j, i: (0, j)),
                  pl.BlockSpec((SUBLANES, hb), vmap0),
                  pl.BlockSpec((SUBLANES, hb), vmap1)],
        out_specs=pl.BlockSpec((bm, bn), lambda j, i: (i, j)),
        out_shape=jax.ShapeDtypeStruct((m, ffp), BF16),
        scratch_shapes=[pltpu.VMEM((SUBLANES, bn), F32), pltpu.VMEM((SUBLANES, bn), F32),
                        pltpu.VMEM((2, d, bn), BF16), pltpu.VMEM((2, d, bn), BF16),
                        pltpu.VMEM((kc, bn), F32), pltpu.VMEM((kc, bn), F32),
                        pltpu.SemaphoreType.DMA((3,))],
        compiler_params=_params(nbytes, 2),
        name="ffn_up_stream",
    )(h, w_up, c8, c8, c8)


def _dwconv_silu_kernel(x_ref, w_ref, o_ref, tail, *, taps, rb):
    @pl.when(pl.program_id(1) == 0)
    def _():
        tail[...] = jnp.zeros_like(tail)

    x = x_ref[...]
    ext = jnp.concatenate([tail[...], x], axis=0)
    w = w_ref[...]
    out = w[taps:taps + 1] + w[taps - 1:taps] * x
    for k in range(taps - 1):
        d = taps - 1 - k
        out = out + pltpu.roll(ext, d, 0)[SUBLANES:] * w[k:k + 1]
    tail[...] = x[rb - SUBLANES:]
    o_ref[...] = _silu(out).astype(o_ref.dtype)


def dwconv_silu(x, col0, ncols, w8, taps):
    m = x.shape[0]
    rb = min(ROW_BLK, m)
    cb = 1024
    assert col0 % cb == 0 and ncols % cb == 0
    off = col0 // cb
    return pl.pallas_call(
        functools.partial(_dwconv_silu_kernel, taps=taps, rb=rb),
        grid=(ncols // cb, m // rb),
        in_specs=[pl.BlockSpec((rb, cb), lambda j, i: (i, j + off)),
                  pl.BlockSpec((SUBLANES, cb), lambda j, i: (0, j))],
        out_specs=pl.BlockSpec((rb, cb), lambda j, i: (i, j)),
        out_shape=jax.ShapeDtypeStruct((m, ncols), F32),
        scratch_shapes=[pltpu.VMEM((SUBLANES, cb), F32)],
        compiler_params=_params(16 * rb * cb * 4, 2),
        name="dwconv_silu",
    )(x, w8)


def rope_tables(positions):
    half = ROPE_DIM // 2
    inv_freq = jnp.power(jnp.float32(ROPE_THETA), -jnp.arange(half, dtype=F32) * (2.0 / ROPE_DIM))
    ang = positions.astype(F32)[:, None] * inv_freq
    cos, sin = jnp.cos(ang), jnp.sin(ang)
    s = positions.shape[0]
    pad = LANES - ROPE_DIM
    cos_t = jnp.concatenate([cos, cos, jnp.ones((s, pad), F32)], axis=1)
    sa = jnp.concatenate([-sin, jnp.zeros((s, LANES - half), F32)], axis=1)
    sb = jnp.concatenate([jnp.zeros((s, half), F32), sin, jnp.zeros((s, pad), F32)], axis=1)
    return cos_t, sa, sb


def _flash_softmax(s, m_ref, l_ref, rows):
    tk = s.shape[1]
    sc = [s[:, c * LANES:(c + 1) * LANES] for c in range(tk // LANES)]
    m_prev = m_ref[rows, :]
    m_new = jnp.maximum(m_prev, jnp.max(functools.reduce(jnp.maximum, sc), axis=-1, keepdims=True))
    alpha = jnp.exp2(m_prev - m_new)
    pc = [jnp.exp2(c - m_new) for c in sc]
    l_ref[rows, :] = alpha * l_ref[rows, :] + functools.reduce(jnp.add, pc)
    m_ref[rows, :] = m_new
    return jnp.concatenate([c.astype(BF16) for c in pc], axis=1), alpha


def _flash_accumulate(p, alpha, v, acc_ref, rows):
    pv = jnp.dot(p, v, preferred_element_type=F32)
    a = jnp.concatenate([alpha] * (pv.shape[1] // LANES), axis=1) if pv.shape[1] > LANES else alpha
    acc_ref[rows, :] = a * acc_ref[rows, :] + pv


def _flash_block(chains, v):
    pa = [_flash_softmax(s, m_ref, l_ref, rows) for s, m_ref, l_ref, _, rows in chains]
    for (p, alpha), (_, _, _, acc_ref, rows) in zip(pa, chains):
        _flash_accumulate(p, alpha, v, acc_ref, rows)


def _flash_init(m_ref, l_ref, acc_ref):
    m_ref[...] = jnp.full(m_ref.shape, -jnp.inf, F32)
    l_ref[...] = jnp.zeros(l_ref.shape, F32)
    acc_ref[...] = jnp.zeros(acc_ref.shape, F32)


def _flash_result(l_ref, acc_ref):
    return acc_ref[...] / jnp.sum(l_ref[...], axis=-1, keepdims=True)


def _causal_mask(tr, tk, delta):
    r = lax.broadcasted_iota(jnp.int32, (tr, tk), 0)
    c = lax.broadcasted_iota(jnp.int32, (tr, tk), 1)
    return c <= r + delta


def _diag_plan(tq, tk, tr):
    plan = []
    for co in range(0, tq, tk):
        subs = []
        for r0 in range(0, tq, tr):
            if r0 + tr - 1 < co:
                continue
            subs.append((r0, None if co + tk - 1 <= r0 else r0 - co))
        plan.append(subs)
    return plan


def _causal_sweep(qi, tq, tk, tr, block):
    full = [(r0, None) for r0 in range(0, tq, tr)]
    per_q = tq // tk

    def body(j, carry):
        block(j, full)
        return carry

    lax.fori_loop(0, qi * per_q, body, 0)
    for b, subs in enumerate(_diag_plan(tq, tk, tr)):
        block(qi * per_q + b, subs)


def _fox_kernel(q_ref, k_ref, v_ref, b_ref, mix_ref, o_ref, q_scr, m_ref, l_ref, acc_ref, *, tq, tk, tr, q_scale):
    del mix_ref
    q_scr[...] = (q_ref[...] * q_scale).astype(BF16)
    _flash_init(m_ref, l_ref, acc_ref)

    def block(j, subs):
        off = pl.multiple_of(j * tk, tk)
        k = k_ref[pl.ds(off, tk), :]
        v = v_ref[pl.ds(off, tk), :]
        bias = b_ref[:, pl.ds(off, tk)] * LOG2E
        chains = []
        for r0, delta in subs:
            rows = pl.ds(r0, tr)
            s = lax.dot_general(q_scr[rows, :], k, _NT, preferred_element_type=F32) + bias
            if delta is not None:
                s = jnp.where(_causal_mask(tr, tk, delta), s, -jnp.inf)
            chains.append((s, m_ref, l_ref, acc_ref, rows))
        _flash_block(chains, v)

    _causal_sweep(pl.program_id(1), tq, tk, tr, block)
    o_ref[...] = _flash_result(l_ref, acc_ref).astype(o_ref.dtype)


def _att_tiles(s):
    tq = min(ATT_TQ, s)
    tk = min(ATT_TK, tq)
    tr = min(ATT_SUB, tk)
    assert s % tq == 0 and tq % tk == 0 and tk % tr == 0
    return tq, tk, tr


def fox_attention(q, kv, negcum, n_heads, mix):
    s = q.shape[0]
    tq, tk, tr = _att_tiles(s)
    hd = HEAD_DIM
    nbytes = 2 * (2 * s * hd * 2 + tq * hd * 4 + tq * hd * 2 + s * 4) + 12 * tq * tk * 4
    return pl.pallas_call(
        functools.partial(_fox_kernel, tq=tq, tk=tk, tr=tr, q_scale=hd ** -0.5 * LOG2E),
        grid=(n_heads, s // tq),
        in_specs=[pl.BlockSpec((tq, hd), lambda h, i: (i, h)),
                  pl.BlockSpec((s, hd), lambda h, i: (0, h)),
                  pl.BlockSpec((s, hd), lambda h, i: (0, n_heads + h)),
                  pl.BlockSpec((None, 1, s), lambda h, i: (h, 0, 0)),
                  pl.BlockSpec(memory_space=pl.ANY)],
        out_specs=pl.BlockSpec((tq, hd), lambda h, i: (i, h)),
        out_shape=jax.ShapeDtypeStruct(mix.shape, mix.dtype),
        input_output_aliases={4: 0},
        scratch_shapes=[pltpu.VMEM((tq, hd), BF16), pltpu.VMEM((tq, LANES), F32),
                        pltpu.VMEM((tq, LANES), F32), pltpu.VMEM((tq, hd), F32)],
        compiler_params=_params(nbytes, 2),
        name="fox_attention",
    )(q, kv, kv, negcum, mix)


def _diff_kernel(q_ref, k_ref, v_ref, lam_ref, g_ref, mix_ref, o_ref, m0, l0, a0, m1, l1, a1,
                 *, tq, tk, tr, lam_init):
    del mix_ref
    hd = HEAD_DIM
    _flash_init(m0, l0, a0)
    _flash_init(m1, l1, a1)

    def block(j, subs):
        off = pl.multiple_of(j * tk, tk)
        v = v_ref[pl.ds(off, tk), :]
        chains = []
        for lo, st in ((0, (m0, l0, a0)), (hd, (m1, l1, a1))):
            k = k_ref[pl.ds(off, tk), lo:lo + hd]
            for r0, delta in subs:
                rows = pl.ds(r0, tr)
                s = lax.dot_general(q_ref[rows, lo:lo + hd], k, _NT, preferred_element_type=F32)
                if delta is not None:
                    s = jnp.where(_causal_mask(tr, tk, delta), s, -jnp.inf)
                chains.append((s,) + st + (rows,))
        _flash_block(chains, v)

    _causal_sweep(pl.program_id(1), tq, tk, tr, block)
    lv = lam_ref[...]
    lam = (jnp.exp(jnp.sum(lv[0:1] * lv[1:2], axis=-1, keepdims=True))
           - jnp.exp(jnp.sum(lv[2:3] * lv[3:4], axis=-1, keepdims=True)) + lam_init)
    o = _flash_result(l0, a0) - lam * _flash_result(l1, a1)
    o = o * lax.rsqrt(jnp.mean(o * o, axis=-1, keepdims=True) + NORM_EPS) * g_ref[...]
    o_ref[...] = (o * (1.0 - lam_init)).astype(o_ref.dtype)


def diff_attention(qk, v, lam_vecs, subln_g, lam_init, n_heads, mix):
    s = qk.shape[0]
    tq, tk, tr = _att_tiles(s)
    t = tq
    w = 2 * HEAD_DIM
    nbytes = 2 * (2 * s * w * 2 + 2 * tq * w * 2) + 24 * tq * tk * 4
    stats = lambda: [pltpu.VMEM((t, LANES), F32), pltpu.VMEM((t, LANES), F32), pltpu.VMEM((t, w), F32)]
    return pl.pallas_call(
        functools.partial(_diff_kernel, tq=tq, tk=tk, tr=tr, lam_init=lam_init),
        grid=(n_heads, s // t),
        in_specs=[pl.BlockSpec((t, w), lambda h, i: (i, h)),
                  pl.BlockSpec((s, w), lambda h, i: (0, n_heads + h)),
                  pl.BlockSpec((s, w), lambda h, i: (0, h)),
                  pl.BlockSpec((4, HEAD_DIM), lambda h, i: (0, 0)),
                  pl.BlockSpec((1, w), lambda h, i: (0, 0)),
                  pl.BlockSpec(memory_space=pl.ANY)],
        out_specs=pl.BlockSpec((t, w), lambda h, i: (i, h)),
        out_shape=jax.ShapeDtypeStruct(mix.shape, mix.dtype),
        input_output_aliases={5: 0},
        scratch_shapes=stats() + stats(),
        compiler_params=_params(nbytes, 2),
        name="diff_attention",
    )(qk, qk, v, lam_vecs.astype(F32), subln_g.reshape(1, w).astype(F32), mix)


def _mem_attn_kernel(q_ref, kv_ref, o_ref, *, scale, heads, hd, ctx_col0):
    o_ref[:, :ctx_col0] = jnp.zeros((o_ref.shape[0], ctx_col0), o_ref.dtype)
    for h in range(heads):
        k = kv_ref[:, h * hd:(h + 1) * hd]
        v = kv_ref[:, (heads + h) * hd:(heads + h + 1) * hd]
        s = lax.dot_general(q_ref[:, h * hd:(h + 1) * hd], k, _NT, preferred_element_type=F32) * scale
        p = jnp.exp(s - jnp.max(s, axis=-1, keepdims=True))
        l = jnp.sum(p, axis=-1, keepdims=True)
        o = jnp.dot(p.astype(BF16), v, preferred_element_type=F32)
        o_ref[:, ctx_col0 + h * hd:ctx_col0 + (h + 1) * hd] = (o / l).astype(o_ref.dtype)


def mem_attention(qarr, q_col0, mem_kv, out_width, ctx_col0):
    s = qarr.shape[0]
    mem_width = mem_kv.shape[1] // 2
    hd = mem_width // MEM_HEADS
    mt = mem_kv.shape[0]
    tq = min(MEM_TQ, s)
    assert q_col0 % mem_width == 0 and ctx_col0 + mem_width == out_width
    q_blk = q_col0 // mem_width
    return pl.pallas_call(
        functools.partial(_mem_attn_kernel, scale=hd ** -0.5, heads=MEM_HEADS, hd=hd, ctx_col0=ctx_col0),
        grid=(s // tq,),
        in_specs=[pl.BlockSpec((tq, mem_width), lambda i: (i, q_blk)),
                  pl.BlockSpec((mt, 2 * mem_width), lambda i: (0, 0))],
        out_specs=pl.BlockSpec((tq, out_width), lambda i: (i, 0)),
        out_shape=jax.ShapeDtypeStruct((s, out_width), BF16),
        compiler_params=_params(4 * tq * out_width * 2 + 32 * tq * hd * 4, 1),
        name="mem_attention",
    )(qarr, mem_kv)


def _fox_cum_kernel(x_ref, b_ref, o_ref, carry, *, cb):
    @pl.when(pl.program_id(0) == 0)
    def _():
        carry[...] = jnp.zeros_like(carry)

    x = x_ref[...] + b_ref[...]
    log_f = jnp.minimum(x, 0.0) - jnp.log1p(jnp.exp(-jnp.abs(x)))
    cs = _cumsum_rows(_tri_lower(cb), log_f) + carry[...]
    o_ref[...] = cs
    carry[...] = cs[cb - 1:cb, :]


def fox_cum(logits, bias):
    s = logits.shape[0]
    cb = min(CUM_BLK, s)
    return pl.pallas_call(
        functools.partial(_fox_cum_kernel, cb=cb),
        grid=(s // cb,),
        in_specs=[pl.BlockSpec((cb, LANES), lambda i: (i, 0)),
                  pl.BlockSpec((1, LANES), lambda i: (0, 0))],
        out_specs=pl.BlockSpec((cb, LANES), lambda i: (i, 0)),
        out_shape=jax.ShapeDtypeStruct((s, LANES), F32),
        scratch_shapes=[pltpu.VMEM((1, LANES), F32)],
        compiler_params=_params(16 << 20, 1),
        name="fox_cum",
    )(logits, bias)


def _ssd_kernel(z_ref, xs_ref, b_ref, c_ref, dtc_ref, dtr_ref, bias_e_ref, alog_e_ref, dskip_e_ref,
                bias_r_ref, alog_r_ref, ng_ref, mix_ref, o_ref, h_ref, *, heads, hdim, chunk):
    del mix_ref
    @pl.when(pl.program_id(1) == 0)
    def _():
        h_ref[...] = jnp.zeros_like(h_ref)

    L = chunk
    width = heads * hdim
    tri = _tri_lower(L)
    causal = _causal_mask(L, L, 0)
    head_id = lax.broadcasted_iota(jnp.int32, (L, width), 1) // hdim
    lane = lax.broadcasted_iota(jnp.int32, (LANES, width), 0)
    head = lax.broadcasted_iota(jnp.int32, (LANES, width), 1) // hdim + pl.program_id(0) * heads
    spread = jnp.where(lane == head, 1.0, 0.0).astype(BF16)

    def local(rows):
        xs = xs_ref[rows, :]
        bm = b_ref[rows, :]
        cm_b = c_ref[rows, :].astype(BF16)
        dt_raw = sum(jnp.dot(part, spread, preferred_element_type=F32) for part in _split3(dtc_ref[rows, :]))
        dt_e = _softplus(dt_raw + bias_e_ref[...])
        a_e = dt_e * (-jnp.exp(alog_e_ref[...]))
        acs_e = _cumsum_rows(tri, a_e)
        acs_last = acs_e[L - 1:L, :]
        dt_r = _softplus(dtr_ref[:, rows] + bias_r_ref[...])
        a_r = dt_r * (-jnp.exp(alog_r_ref[...]))
        acs_r = _cumsum_lanes(a_r, tri.T)

        xd = xs * dt_e
        xd_b = xd.astype(BF16)
        cb = lax.dot_general(cm_b, bm.astype(BF16), _NT, preferred_element_type=F32)
        y = xs * dskip_e_ref[...]
        for r in range(heads):
            seg = acs_e[:, r * hdim:r * hdim + 1] - acs_r[r:r + 1, :]
            decay = jnp.exp(jnp.where(causal, seg, -jnp.inf))
            yr = jnp.dot((cb * decay).astype(BF16), xd_b, preferred_element_type=F32)
            y = y + jnp.where(head_id == r, yr, 0.0)
        to_end = jnp.exp(acs_last - acs_e)
        states = jnp.dot(bm.T.astype(BF16), (xd * to_end).astype(BF16), preferred_element_type=F32)
        return y, cm_b, jnp.exp(acs_e), jnp.exp(acs_last), states

    row_blocks = [pl.ds(c * L, L) for c in range(o_ref.shape[0] // L)]
    parts = [local(rows) for rows in row_blocks]
    h = h_ref[...]
    for rows, (y, cm_b, grow, grow_last, states) in zip(row_blocks, parts):
        y = y + jnp.dot(cm_b, h.astype(BF16), preferred_element_type=F32) * grow
        h = h * grow_last + states
        g = y * _silu(z_ref[rows, :])
        g = g * lax.rsqrt(jnp.mean(g * g, axis=-1, keepdims=True) + NORM_EPS)
        o_ref[rows, :] = (g * ng_ref[...]).astype(o_ref.dtype)
    h_ref[...] = h


def ssd_scan(zx, xbc, dt_c, dt_r, bias_e, alog_e, dskip_e, bias_r, alog_r, norm_g, inner, mix):
    s = zx.shape[0]
    L = SSD_CHUNK
    rows = min(SSD_STEP_CHUNKS * L, s)
    G = SSD_GROUPS
    width = inner // G
    heads = width // SSD_HEAD_DIM
    N = SSD_STATE
    b_off = inner // N
    c_off = b_off + G
    row = lambda g, c: (0, g)
    return pl.pallas_call(
        functools.partial(_ssd_kernel, heads=heads, hdim=SSD_HEAD_DIM, chunk=L),
        grid=(G, s // rows),
        in_specs=[pl.BlockSpec((rows, width), lambda g, c: (c, g)),
                  pl.BlockSpec((rows, width), lambda g, c: (c, g)),
                  pl.BlockSpec((rows, N), lambda g, c: (c, b_off + g)),
                  pl.BlockSpec((rows, N), lambda g, c: (c, c_off + g)),
                  pl.BlockSpec((rows, LANES), lambda g, c: (c, 0)),
                  pl.BlockSpec((None, SUBLANES, rows), lambda g, c: (g, 0, c)),
                  pl.BlockSpec((1, width), row),
                  pl.BlockSpec((1, width), row),
                  pl.BlockSpec((1, width), row),
                  pl.BlockSpec((None, SUBLANES, 1), lambda g, c: (g, 0, 0)),
                  pl.BlockSpec((None, SUBLANES, 1), lambda g, c: (g, 0, 0)),
                  pl.BlockSpec((1, width), row),
                  pl.BlockSpec(memory_space=pl.ANY)],
        out_specs=pl.BlockSpec((rows, width), lambda g, c: (c, g)),
        out_shape=jax.ShapeDtypeStruct(mix.shape, mix.dtype),
        input_output_aliases={12: 0},
        scratch_shapes=[pltpu.VMEM((N, width), F32)],
        compiler_params=_params(32 << 20, 2),
        name="ssd_scan",
    )(zx, xbc, xbc, xbc, dt_c, dt_r, bias_e, alog_e, dskip_e, bias_r, alog_r, norm_g, mix)


def _conv_rows(w, b):
    taps, c = w.shape
    return jnp.concatenate([w.astype(F32), b.reshape(1, c).astype(F32),
                            jnp.zeros((SUBLANES - taps - 1, c), F32)], axis=0)


def _pad_cols(a, n):
    return jnp.pad(a, ((0, 0), (0, n - a.shape[1])))


def kernel(x, mem, positions, norm_mix, norm_mem, w_mem_kv, w_out, norm_ffn, w_up, conv_ffn_w, conv_ffn_b,
           w_down, a_w_in, a_lambda, a_subln, b_w_in, b_forget_bias, c_w_in, c_conv_w, c_conv_b, c_dt_bias,
           c_a_log, c_d_skip, c_norm_gate, final_norm):
    batch, seq, d_model = x.shape
    depth = norm_mix.shape[0]
    mem_width = w_mem_kv.shape[2] // 2
    tok_width = d_model - mem_width
    d_ff = w_down.shape[1]
    ffp = DOWN_BK * ((d_ff + DOWN_BK - 1) // DOWN_BK)
    diff_heads = tok_width // (2 * HEAD_DIM)
    fox_heads = tok_width // HEAD_DIM
    ssd_heads = tok_width // SSD_HEAD_DIM
    heads_per_group = ssd_heads // SSD_GROUPS
    conv_ch = tok_width + 2 * SSD_GROUPS * SSD_STATE
    q_scale = HEAD_DIM ** -0.5 * LOG2E

    w_out_b = w_out.astype(BF16)
    w_down_b = w_down.astype(BF16)
    w_mem_kv_b = w_mem_kv.astype(BF16)
    outs = []
    for b in range(batch):
        xb = x[b]
        mem_b = mem[b]
        cos_t, sa, sb = rope_tables(positions[b])
        for i in range(depth):
            kind, j = i % N_MIXERS, i // N_MIXERS
            h = rmsnorm(xb, norm_mix[i], BF16)
            mem_kv = matmul(rmsnorm(mem_b, norm_mem[i], BF16), w_mem_kv_b, BF16, layer=i)
            if kind == 0:
                qk = matmul_stream(h, a_w_in, j, BF16, 0, 2 * tok_width, rope=(cos_t, sa, sb, tok_width, q_scale))
                vq = matmul_stream(h, a_w_in, j, BF16, 2 * tok_width, tok_width + mem_width)
                mix = mem_attention(vq, tok_width, mem_kv, d_model, tok_width)
                lam_init = 0.8 - 0.6 * math.exp(-0.3 * i)
                mix = diff_attention(qk, vq, a_lambda[j], a_subln[j], lam_init, diff_heads, mix)
            elif kind == 1:
                w_t = jnp.swapaxes(b_w_in, 1, 2)
                q = matmul_stream(h, w_t, j, F32, 0, tok_width, w_t=True)
                kv = matmul_stream(h, w_t, j, BF16, tok_width, 2 * tok_width, w_t=True)
                logits = matmul_stream(h, w_t, j, F32, 3 * tok_width, LANES, w_t=True)
                mem_q = matmul_stream(h, w_t, j, BF16, 3 * tok_width + fox_heads, mem_width, w_t=True)
                mix = mem_attention(mem_q, 0, mem_kv, d_model, tok_width)
                bias = _pad_cols(b_forget_bias[j].reshape(1, fox_heads).astype(F32), LANES)
                cum = fox_cum(logits, bias)
                negcum = (-cum[:, :fox_heads]).T.reshape(fox_heads, 1, seq)
                mix = fox_attention(q, kv, negcum, fox_heads, mix)
            else:
                n_zx = tok_width + conv_ch
                w_t = jnp.swapaxes(c_w_in, 1, 2)
                zx = matmul_stream(h, w_t, j, F32, 0, n_zx, w_t=True)
                dt_c = matmul_stream(h, w_t, j, F32, n_zx, LANES, w_t=True)
                mem_q = matmul_stream(h, w_t, j, BF16, n_zx + ssd_heads, mem_width, w_t=True)
                mix = mem_attention(mem_q, 0, mem_kv, d_model, tok_width)
                xbc = dwconv_silu(zx, tok_width, conv_ch, _conv_rows(c_conv_w[j], c_conv_b[j]), SSD_CONV)
                rep = lambda a: jnp.repeat(a.astype(F32), SSD_HEAD_DIM, axis=-1)
                pad_h = SUBLANES - heads_per_group
                by_group = lambda a: jnp.pad(a.astype(F32).reshape(SSD_GROUPS, heads_per_group, -1),
                                             ((0, 0), (0, pad_h), (0, 0)))
                dt_r = by_group(dt_c[:, :ssd_heads].T)
                mix = ssd_scan(zx, xbc, dt_c, dt_r,
                               rep(c_dt_bias[j].reshape(1, -1)), rep(c_a_log[j].reshape(1, -1)),
                               rep(c_d_skip[j].reshape(1, -1)),
                               by_group(c_dt_bias[j].reshape(-1, 1)), by_group(c_a_log[j].reshape(-1, 1)),
                               c_norm_gate[j].reshape(1, -1).astype(F32), tok_width, mix)
            xb = matmul(mix, w_out_b, F32, res=xb, layer=i)

            hf = rmsnorm(xb, norm_ffn[i], BF16)
            hidden = ffn_up_stream(hf, w_up, i, _conv_rows(conv_ffn_w[i], conv_ffn_b[i]), d_ff, ffp)
            xb = matmul(hidden, w_down_b, F32, res=xb, bk=DOWN_BK, layer=i)
        outs.append(rmsnorm(xb, final_norm, x.dtype))
    return outs[0].reshape(x.shape) if batch == 1 else jnp.stack(outs, axis=0)
```
